```python
import math
import jax, jax.numpy as jnp
from jax import lax
import numpy as np

D_MODEL = 4096
BATCH = 4
SEQ = 4096
DEPTH = 1

HEAD_DIM = 128
N_MOBA_HEADS = D_MODEL // (2 * HEAD_DIM)
N_SB_HEADS = D_MODEL // (2 * HEAD_DIM)
MOBA_WIDTH = N_MOBA_HEADS * HEAD_DIM
SB_WIDTH = N_SB_HEADS * HEAD_DIM
MIX_WIDTH = MOBA_WIDTH + SB_WIDTH
MOBA_BLOCK = 256
MOBA_TOPK = 3
MOBA_Q_CHUNK = 16
SB_Q_BLOCK = 128
N_MEM = 256
N_XATTN_HEADS = 4
XATTN_WIDTH = N_XATTN_HEADS * HEAD_DIM
PEER_HEADS = 8
PEER_N_KEYS = 128
PEER_N_EXPERTS = PEER_N_KEYS * PEER_N_KEYS
PEER_KEY_DIM = 256
PEER_HALF = PEER_KEY_DIM // 2
PEER_TOPK = 16
RMS_EPS = 1e-6
NEG_INF = -1e30

kernel_name = 'hybrid_moba_stickbreaking_peer_layer'


def rmsnorm(x, g):
    xf = x.astype(jnp.float32)
    y = xf * lax.rsqrt(jnp.mean(xf * xf, axis=-1, keepdims=True) + RMS_EPS)
    return (y * g.astype(jnp.float32)).astype(x.dtype)


def alibi_slopes(n_heads):
    return jnp.asarray(2.0 ** (-8.0 * np.arange(1, n_heads + 1) / n_heads), dtype=jnp.float32)


def split_heads(t, n_heads):
    b, s, _ = t.shape
    return t.reshape(b, s, n_heads, HEAD_DIM).transpose(0, 2, 1, 3)


def merge_heads(t):
    b, h, s, d = t.shape
    return t.transpose(0, 2, 1, 3).reshape(b, s, h * d)


def moba_attention(q, k, v, slopes):
    B, H, T, Dh = q.shape
    nb = -(-T // MOBA_BLOCK)
    pad = nb * MOBA_BLOCK - T
    k_blocks = jnp.pad(k, ((0, 0), (0, 0), (0, pad), (0, 0))).reshape(B, H, nb, MOBA_BLOCK, Dh)
    v_blocks = jnp.pad(v, ((0, 0), (0, 0), (0, pad), (0, 0))).reshape(B, H, nb, MOBA_BLOCK, Dh)
    k_mean = jnp.mean(k_blocks.astype(jnp.float32), axis=3)
    scale = Dh ** -0.5
    k_eff = min(MOBA_TOPK, nb)
    b_idx = jnp.arange(B)[:, None, None, None]
    h_idx = jnp.arange(H)[None, :, None, None]
    blk_pos = jnp.arange(MOBA_BLOCK)
    n_chunks = T // MOBA_Q_CHUNK

    def chunk(c):
        t0 = c * MOBA_Q_CHUNK
        q_c = lax.dynamic_slice_in_dim(q, t0, MOBA_Q_CHUNK, axis=2)
        t_pos = t0 + jnp.arange(MOBA_Q_CHUNK)
        own = t0 // MOBA_BLOCK
        gate = jnp.einsum('bhqd,bhnd->bhqn', q_c.astype(jnp.float32), k_mean)
        gate = jnp.where(jnp.arange(nb) < own, gate, NEG_INF)
        _, sel = lax.top_k(gate, k_eff)
        sel_valid = jnp.arange(k_eff) < own
        k_sel = k_blocks[b_idx, h_idx, sel]
        v_sel = v_blocks[b_idx, h_idx, sel]
        s_sel = jnp.einsum('bhqd,bhqkjd->bhqkj', q_c, k_sel).astype(jnp.float32) * scale
        dist_sel = (t_pos[:, None, None] - (sel[..., None] * MOBA_BLOCK + blk_pos)).astype(jnp.float32)
        s_sel = s_sel - slopes[:, None, None, None] * dist_sel
        s_sel = jnp.where(sel_valid[:, None], s_sel, NEG_INF)
        s_sel = s_sel.reshape(B, H, MOBA_Q_CHUNK, k_eff * MOBA_BLOCK)
        k_own = lax.dynamic_index_in_dim(k_blocks, own, axis=2, keepdims=False)
        v_own = lax.dynamic_index_in_dim(v_blocks, own, axis=2, keepdims=False)
        s_own = jnp.einsum('bhqd,bhjd->bhqj', q_c, k_own).astype(jnp.float32) * scale
        dist_own = t_pos[:, None] - (own * MOBA_BLOCK + blk_pos)[None, :]
        s_own = s_own - slopes[:, None, None] * dist_own.astype(jnp.float32)
        s_own = jnp.where(dist_own >= 0, s_own, NEG_INF)
        p = jax.nn.softmax(jnp.concatenate([s_sel, s_own], axis=-1), axis=-1)
        p_sel = p[..., : k_eff * MOBA_BLOCK].reshape(B, H, MOBA_Q_CHUNK, k_eff, MOBA_BLOCK)
        p_own = p[..., k_eff * MOBA_BLOCK:]
        return (jnp.einsum('bhqkj,bhqkjd->bhqd', p_sel.astype(v.dtype), v_sel)
                + jnp.einsum('bhqj,bhjd->bhqd', p_own.astype(v.dtype), v_own))

    o = lax.map(chunk, jnp.arange(n_chunks))
    return jnp.moveaxis(o, 0, 2).reshape(B, H, T, Dh)


def stick_breaking_attention(q, k, v):
    B, H, T, Dh = q.shape
    scale = Dh ** -0.5
    key_pos = jnp.arange(T)
    n_blocks = T // SB_Q_BLOCK

    def block(c):
        t0 = c * SB_Q_BLOCK
        q_c = lax.dynamic_slice_in_dim(q, t0, SB_Q_BLOCK, axis=2)
        t_pos = t0 + jnp.arange(SB_Q_BLOCK)
        z = jnp.einsum('bhqd,bhsd->bhqs', q_c, k).astype(jnp.float32) * scale
        strict = key_pos[None, :] < t_pos[:, None]
        log_beta = jax.nn.log_sigmoid(z)
        log_one_minus = jnp.where(strict, jax.nn.log_sigmoid(-z), 0.0)
        log_remain = lax.cumsum(log_one_minus, axis=3, reverse=True) - log_one_minus
        a = jnp.where(strict, jnp.exp(log_beta + log_remain), 0.0)
        return jnp.einsum('bhqs,bhsd->bhqd', a.astype(v.dtype), v)

    o = lax.map(block, jnp.arange(n_blocks))
    return jnp.moveaxis(o, 0, 2).reshape(B, H, T, Dh)


def cross_attention(h, mem_n, w_xq, w_xkv, q_norm_g, k_norm_g, w_xo):
    B, T, _ = h.shape
    M = mem_n.shape[1]
    q = rmsnorm((h @ w_xq).reshape(B, T, N_XATTN_HEADS, HEAD_DIM), q_norm_g)
    kv = mem_n @ w_xkv
    k = rmsnorm(kv[..., :XATTN_WIDTH].reshape(B, M, N_XATTN_HEADS, HEAD_DIM), k_norm_g)
    v = kv[..., XATTN_WIDTH:].reshape(B, M, N_XATTN_HEADS, HEAD_DIM)
    s = jnp.einsum('bthd,bmhd->bhtm', q, k).astype(jnp.float32) * (HEAD_DIM ** -0.5)
    p = jax.nn.softmax(s, axis=-1)
    o = jnp.einsum('bhtm,bmhd->bthd', p.astype(v.dtype), v).reshape(B, T, XATTN_WIDTH)
    return o @ w_xo


def peer_ffn(h, w_q, sub_keys, u, v):
    B, T, D = h.shape
    n_tok = B * T
    hf = h.reshape(n_tok, D)
    q = (hf @ w_q).reshape(n_tok, PEER_HEADS, 2, PEER_HALF)
    s = jnp.einsum('nhpc,hpkc->nhpk', q, sub_keys).astype(jnp.float32)
    val, idx = lax.top_k(s, PEER_TOPK)
    cand = val[:, :, 0, :, None] + val[:, :, 1, None, :]
    cand_id = idx[:, :, 0, :, None] * PEER_N_KEYS + idx[:, :, 1, None, :]
    top_val, top_pos = lax.top_k(cand.reshape(n_tok, PEER_HEADS, PEER_TOPK * PEER_TOPK), PEER_TOPK)
    expert = jnp.take_along_axis(cand_id.reshape(n_tok, PEER_HEADS, PEER_TOPK * PEER_TOPK), top_pos, axis=-1)
    g = jax.nn.softmax(top_val, axis=-1)
    expert = expert.reshape(n_tok, PEER_HEADS * PEER_TOPK)
    act = jax.nn.gelu(hf @ u.T, approximate=False)
    a_sel = jnp.take_along_axis(act, expert, axis=-1)
    w = (g.reshape(n_tok, -1) * a_sel.astype(jnp.float32)).astype(act.dtype)
    gate = jnp.zeros_like(act).at[jnp.arange(n_tok)[:, None], expert].add(w)
    return (gate @ v).reshape(B, T, D)


def setup_inputs(seed: int = 0) -> dict:
    key = jax.random.key(seed)
    ks = jax.random.split(key, 21)
    L, D = DEPTH, D_MODEL

    def normal(k, shape, scale):
        return jax.random.normal(k, shape, jnp.float32) * scale

    def gain(k, shape):
        return 1.0 + 0.02 * jax.random.normal(k, shape, jnp.float32)

    return {
        'x': normal(ks[0], (BATCH, SEQ, D), 1.0),
        'mem': normal(ks[1], (BATCH, N_MEM, D), 1.0),
        'norm_mix_g': gain(ks[2], (L, D)),
        'w_in': normal(ks[3], (L, D, 3 * MIX_WIDTH), D ** -0.5),
        'moba_q_norm_g': gain(ks[4], (L, HEAD_DIM)),
        'moba_k_norm_g': gain(ks[5], (L, HEAD_DIM)),
        'moba_out_norm_g': gain(ks[6], (L, MOBA_WIDTH)),
        'sb_out_norm_g': gain(ks[7], (L, SB_WIDTH)),
        'w_out': normal(ks[8], (L, MIX_WIDTH, D), MIX_WIDTH ** -0.5),
        'norm_xattn_g': gain(ks[9], (L, D)),
        'norm_mem_g': gain(ks[10], (L, D)),
        'w_xq': normal(ks[11], (L, D, XATTN_WIDTH), D ** -0.5),
        'w_xkv': normal(ks[12], (L, D, 2 * XATTN_WIDTH), D ** -0.5),
        'xattn_q_norm_g': gain(ks[13], (L, HEAD_DIM)),
        'xattn_k_norm_g': gain(ks[14], (L, HEAD_DIM)),
        'w_xo': normal(ks[15], (L, XATTN_WIDTH, D), XATTN_WIDTH ** -0.5),
        'norm_ffn_g': gain(ks[16], (L, D)),
        'w_peer_q': normal(ks[17], (L, D, PEER_HEADS * PEER_KEY_DIM), D ** -0.5),
        'peer_sub_keys': normal(ks[18], (L, PEER_HEADS, 2, PEER_N_KEYS, PEER_HALF), PEER_HALF ** -0.5),
        'peer_u': normal(ks[19], (L, PEER_N_EXPERTS, D), D ** -0.5),
        'peer_v': normal(ks[20], (L, PEER_N_EXPERTS, D), PEER_HEADS ** -0.5),
    }


def reference(x, mem, norm_mix_g, w_in, moba_q_norm_g, moba_k_norm_g, moba_out_norm_g,
              sb_out_norm_g, w_out, norm_xattn_g, norm_mem_g, w_xq, w_xkv, xattn_q_norm_g,
              xattn_k_norm_g, w_xo, norm_ffn_g, w_peer_q, peer_sub_keys, peer_u, peer_v):
    slopes = alibi_slopes(N_MOBA_HEADS)
    splits = [MOBA_WIDTH, 2 * MOBA_WIDTH, 3 * MOBA_WIDTH,
              3 * MOBA_WIDTH + SB_WIDTH, 3 * MOBA_WIDTH + 2 * SB_WIDTH]
    for l in range(DEPTH):
        h = rmsnorm(x, norm_mix_g[l])
        proj = h @ w_in[l]
        mq, mk, mv, sq, sk, sv = jnp.split(proj, splits, axis=-1)
        mq = rmsnorm(split_heads(mq, N_MOBA_HEADS), moba_q_norm_g[l])
        mk = rmsnorm(split_heads(mk, N_MOBA_HEADS), moba_k_norm_g[l])
        o_moba = moba_attention(mq, mk, split_heads(mv, N_MOBA_HEADS), slopes)
        o_sb = stick_breaking_attention(split_heads(sq, N_SB_HEADS), split_heads(sk, N_SB_HEADS),
                                        split_heads(sv, N_SB_HEADS))
        o_moba = rmsnorm(merge_heads(o_moba), moba_out_norm_g[l])
        o_sb = rmsnorm(merge_heads(o_sb), sb_out_norm_g[l])
        x = x + jnp.concatenate([o_moba, o_sb], axis=-1) @ w_out[l]
        h = rmsnorm(x, norm_xattn_g[l])
        mem_n = rmsnorm(mem, norm_mem_g[l])
        x = x + cross_attention(h, mem_n, w_xq[l], w_xkv[l], xattn_q_norm_g[l],
                                xattn_k_norm_g[l], w_xo[l])
        h = rmsnorm(x, norm_ffn_g[l])
        x = x + peer_ffn(h, w_peer_q[l], peer_sub_keys[l], peer_u[l], peer_v[l])
    return x
```

```python
import functools

import numpy as np
import jax
import jax.numpy as jnp
from jax import lax
from jax.experimental import pallas as pl
from jax.experimental.pallas import tpu as pltpu

HEAD_DIM = 128
MOBA_BLOCK = 256
MOBA_TOPK = 3
N_XATTN_HEADS = 4
PEER_HEADS = 8
PEER_N_KEYS = 128
PEER_TOPK = 16
RMS_EPS = 1e-6
NEG_INF = -1e30

LANES = 128
VMEM_LIMIT = 56 * 1024 * 1024

F32 = jnp.float32
BF16 = jnp.bfloat16

_NT = (((1,), (1,)), ((), ()))
_TN = (((0,), (0,)), ((), ()))


def _params(*sem):
    return pltpu.CompilerParams(dimension_semantics=sem, vmem_limit_bytes=VMEM_LIMIT)


def _split_bf16(x):
    hi = x.astype(BF16)
    lo = (x - hi.astype(F32)).astype(BF16)
    return hi, lo


def _norm_matmul_kernel(*refs, k_sizes, has_gain, n_norm_tiles, n_col_tiles, has_res,
                        head_major, row_chunk):
    n_in = len(k_sizes)
    pos = 0
    x_refs = refs[pos:pos + n_in]; pos += n_in
    g_refs = []
    for hg in has_gain:
        if hg:
            g_refs.append(refs[pos]); pos += 1
        else:
            g_refs.append(None)
    w_ref = refs[pos]; pos += 1
    cg_ref = None
    if n_norm_tiles > 0:
        cg_ref = refs[pos]; pos += 1
    res_ref = None
    if has_res:
        res_ref = refs[pos]; pos += 1
    o_ref = refs[pos]; pos += 1
    h_ref = refs[pos]

    j = pl.program_id(1)
    tm = h_ref.shape[0]

    @pl.when(j == 0)
    def _prologue():
        off = 0
        for x_ref, g_ref, ksz in zip(x_refs, g_refs, k_sizes):
            for r0 in range(0, tm, row_chunk):
                x = x_ref[r0:r0 + row_chunk, :]
                if g_ref is not None:
                    ms = jnp.mean(x * x, axis=-1, keepdims=True)
                    x = (x * lax.rsqrt(ms + RMS_EPS)) * g_ref[...]
                h_ref[r0:r0 + row_chunk, off:off + ksz] = x.astype(BF16)
            off += ksz

    acc = jnp.dot(h_ref[...], w_ref[...], preferred_element_type=F32)
    tn = acc.shape[1]

    def finish(get_group):
        for hh in range(tn // LANES):
            y = get_group(hh)
            if res_ref is not None:
                y = y + res_ref[:, hh * LANES:(hh + 1) * LANES]
            if head_major:
                o_ref[0, hh] = y
            else:
                o_ref[:, hh * LANES:(hh + 1) * LANES] = y

    def plain_group(hh):
        return acc[:, hh * LANES:(hh + 1) * LANES]

    def normed_group(hh):
        a = acc[:, hh * LANES:(hh + 1) * LANES]
        ms = jnp.mean(a * a, axis=-1, keepdims=True)
        return (a * lax.rsqrt(ms + RMS_EPS)) * cg_ref[:, hh * LANES:(hh + 1) * LANES]

    if n_norm_tiles == 0:
        finish(plain_group)
    elif n_norm_tiles >= n_col_tiles:
        finish(normed_group)
    else:
        @pl.when(j < n_norm_tiles)
        def _():
            finish(normed_group)

        @pl.when(j >= n_norm_tiles)
        def _():
            finish(plain_group)


def norm_matmul(xs, gains, w, *, col_gain=None, n_norm_cols=0, residual=None,
                head_major_bt=None, tm=512, tn=512):
    m = xs[0].shape[0]
    k_sizes = tuple(int(x.shape[1]) for x in xs)
    k_total = sum(k_sizes)
    n_cols = w.shape[1]
    tm = min(tm, m)
    tn = min(tn, n_cols)
    assert m % tm == 0 and n_cols % tn == 0 and w.shape[0] == k_total
    assert n_norm_cols % tn == 0
    n_col_tiles = n_cols // tn
    n_norm_tiles = n_norm_cols // tn
    has_gain = tuple(g is not None for g in gains)

    args, in_specs = [], []
    for x, ksz in zip(xs, k_sizes):
        args.append(x)
        in_specs.append(pl.BlockSpec((tm, ksz), lambda i, j: (i, 0)))
    for g, ksz in zip(gains, k_sizes):
        if g is not None:
            args.append(g.reshape(1, ksz).astype(F32))
            in_specs.append(pl.BlockSpec((1, ksz), lambda i, j: (0, 0)))
    args.append(w)
    in_specs.append(pl.BlockSpec((k_total, tn), lambda i, j: (0, j)))
    if n_norm_tiles > 0:
        args.append(col_gain.reshape(1, n_cols).astype(F32))
        in_specs.append(pl.BlockSpec((1, tn), lambda i, j: (0, j)))
    if residual is not None:
        args.append(residual)
        in_specs.append(pl.BlockSpec((tm, tn), lambda i, j: (i, j)))

    if head_major_bt is not None:
        b, t = head_major_bt
        assert b * t == m and t % tm == 0
        tiles_per_b = t // tm
        out_shape = jax.ShapeDtypeStruct((b, n_cols // LANES, t, LANES), F32)
        out_spec = pl.BlockSpec((1, tn // LANES, tm, LANES),
                                lambda i, j: (i // tiles_per_b, j, i % tiles_per_b, 0))
    else:
        out_shape = jax.ShapeDtypeStruct((m, n_cols), F32)
        out_spec = pl.BlockSpec((tm, tn), lambda i, j: (i, j))

    kern = functools.partial(
        _norm_matmul_kernel, k_sizes=k_sizes, has_gain=has_gain, n_norm_tiles=n_norm_tiles,
        n_col_tiles=n_col_tiles, has_res=residual is not None,
        head_major=head_major_bt is not None, row_chunk=min(128, tm))
    return pl.pallas_call(
        kern,
        out_shape=out_shape,
        grid=(m // tm, n_col_tiles),
        in_specs=in_specs,
        out_specs=out_spec,
        scratch_shapes=[pltpu.VMEM((tm, k_total), BF16)],
        compiler_params=_params("parallel", "arbitrary"),
    )(*args)


def _moba_kernel(slopes_ref, q_ref, k_ref, v_ref, o_ref,
                 kb_ref, vt_ref, kmean_ref, sel_ref, m_ref, l_ref, acc_ref, *, nb, blk, scale):
    h = pl.program_id(1)
    qi = pl.program_id(2)

    @pl.when(qi == 0)
    def _prep():
        for n in range(nb):
            kblk = k_ref[0, 0, n * blk:(n + 1) * blk, :]
            kb_ref[n] = kblk.astype(BF16)
            kmean_ref[n:n + 1, :] = jnp.mean(kblk, axis=0, keepdims=True)
            vt_ref[n] = v_ref[0, 0, n * blk:(n + 1) * blk, :].T.astype(BF16)

    q = q_ref[0, 0]
    qb = q.astype(BF16)
    slope = slopes_ref[h]

    kh, kl = _split_bf16(kmean_ref[...])
    qh, ql = _split_bf16(q)
    gate = (lax.dot_general(kh, qh, _NT, preferred_element_type=F32)
            + lax.dot_general(kh, ql, _NT, preferred_element_type=F32)
            + lax.dot_general(kl, qh, _NT, preferred_element_type=F32))
    blkid = lax.broadcasted_iota(jnp.int32, (nb, blk), 0)
    valid = blkid < qi
    gate = jnp.where(valid, gate, NEG_INF)
    rank = jnp.zeros((nb, blk), F32)
    for mth in range(nb):
        gm = gate[mth:mth + 1, :]
        beats = jnp.where(gm > gate, 1.0, jnp.where(gm == gate, jnp.where(blkid > mth, 1.0, 0.0), 0.0))
        rank = rank + beats
    sel_ref[...] = jnp.where(valid, jnp.where(rank < MOBA_TOPK, 1.0, 0.0), 0.0)

    t_idx = lax.broadcasted_iota(jnp.int32, (blk, blk), 1)
    j_idx = lax.broadcasted_iota(jnp.int32, (blk, blk), 0)
    dist_own = (t_idx - j_idx).astype(F32)

    s = lax.dot_general(kb_ref[qi], qb, _NT, preferred_element_type=F32) * scale
    s = s - slope * dist_own
    s = jnp.where(dist_own >= 0, s, NEG_INF)
    m0 = jnp.max(s, axis=0, keepdims=True)
    p = jnp.exp(s - m0)
    m_ref[...] = m0
    l_ref[...] = jnp.sum(p, axis=0, keepdims=True)
    acc_ref[...] = jnp.dot(vt_ref[qi], p.astype(BF16), preferred_element_type=F32)

    def body(n, carry):
        selrow = sel_ref[pl.ds(n, 1), :]
        dist = dist_own + ((qi - n) * blk).astype(F32)
        s = lax.dot_general(kb_ref[n], qb, _NT, preferred_element_type=F32) * scale
        s = s - slope * dist
        s = jnp.where(selrow > 0, s, NEG_INF)
        m_old = m_ref[...]
        m_new = jnp.maximum(m_old, jnp.max(s, axis=0, keepdims=True))
        alpha = jnp.exp(m_old - m_new)
        p = jnp.exp(s - m_new)
        l_ref[...] = alpha * l_ref[...] + jnp.sum(p, axis=0, keepdims=True)
        acc_ref[...] = alpha * acc_ref[...] + jnp.dot(vt_ref[n], p.astype(BF16),
                                                      preferred_element_type=F32)
        m_ref[...] = m_new
        return carry

    lax.fori_loop(0, qi, body, 0)
    o_ref[0] = (acc_ref[...] / l_ref[...]).T


def moba_attention(qkv, slopes, *, q_head0, k_head0, v_head0, n_heads):
    b, _, t, dh = qkv.shape
    blk = MOBA_BLOCK
    assert t % blk == 0
    nb = t // blk
    kern = functools.partial(_moba_kernel, nb=nb, blk=blk, scale=dh ** -0.5)
    return pl.pallas_call(
        kern,
        out_shape=jax.ShapeDtypeStruct((b, t, n_heads * dh), F32),
        grid=(b, n_heads, nb),
        in_specs=[
            pl.BlockSpec(memory_space=pltpu.SMEM),
            pl.BlockSpec((1, 1, blk, dh), lambda bi, h, qi: (bi, q_head0 + h, qi, 0)),
            pl.BlockSpec((1, 1, t, dh), lambda bi, h, qi: (bi, k_head0 + h, 0, 0)),
            pl.BlockSpec((1, 1, t, dh), lambda bi, h, qi: (bi, v_head0 + h, 0, 0)),
        ],
        out_specs=pl.BlockSpec((1, blk, dh), lambda bi, h, qi: (bi, qi, h)),
        scratch_shapes=[
            pltpu.VMEM((nb, blk, dh), BF16),
            pltpu.VMEM((nb, dh, blk), BF16),
            pltpu.VMEM((nb, dh), F32),
            pltpu.VMEM((nb, blk), F32),
            pltpu.VMEM((1, blk), F32),
            pltpu.VMEM((1, blk), F32),
            pltpu.VMEM((dh, blk), F32),
        ],
        compiler_params=_params("parallel", "parallel", "arbitrary"),
    )(slopes, qkv, qkv, qkv)


def _sb_kernel(q_ref, k_ref, v_ref, o_ref, kb_ref, vb_ref, *, nb, blk, scale):
    qi = pl.program_id(2)

    @pl.when(qi == 0)
    def _prep():
        for n in range(nb):
            kb_ref[n] = k_ref[0, 0, n * blk:(n + 1) * blk, :].astype(BF16)
            vb_ref[n] = v_ref[0, 0, n * blk:(n + 1) * blk, :].astype(BF16)

    qb = q_ref[0, 0].astype(BF16)
    row = lax.broadcasted_iota(jnp.int32, (blk, blk), 0)
    col = lax.broadcasted_iota(jnp.int32, (blk, blk), 1)
    after = jnp.where(row > col, 1.0, 0.0).astype(BF16)
    strict = col < row

    def tile(kn, vn, carry, diag):
        z = lax.dot_general(qb, kn, _NT, preferred_element_type=F32) * scale
        t = jnp.log1p(jnp.exp(-jnp.abs(z)))
        log_beta = jnp.minimum(z, 0.0) - t
        log_om = jnp.minimum(-z, 0.0) - t
        if diag:
            log_om = jnp.where(strict, log_om, 0.0)
        hi, lo = _split_bf16(log_om)
        remain = (jnp.dot(hi, after, preferred_element_type=F32)
                  + jnp.dot(lo, after, preferred_element_type=F32))
        a = jnp.exp(log_beta + remain + carry)
        if diag:
            a = jnp.where(strict, a, 0.0)
        contrib = jnp.dot(a.astype(BF16), vn, preferred_element_type=F32)
        return contrib, carry + jnp.sum(log_om, axis=1, keepdims=True)

    acc0, carry0 = tile(kb_ref[qi], vb_ref[qi], jnp.zeros((blk, 1), F32), True)

    def body(step, state):
        acc, carry = state
        n = qi - 1 - step
        contrib, carry = tile(kb_ref[n], vb_ref[n], carry, False)
        return acc + contrib, carry

    acc, _ = lax.fori_loop(0, qi, body, (acc0, carry0))
    o_ref[0] = acc


def stick_breaking_attention(qkv, *, q_head0, k_head0, v_head0, n_heads, blk=256):
    b, _, t, dh = qkv.shape
    assert t % blk == 0
    nb = t // blk
    kern = functools.partial(_sb_kernel, nb=nb, blk=blk, scale=dh ** -0.5)
    return pl.pallas_call(
        kern,
        out_shape=jax.ShapeDtypeStruct((b, t, n_heads * dh), F32),
        grid=(b, n_heads, nb),
        in_specs=[
            pl.BlockSpec((1, 1, blk, dh), lambda bi, h, qi: (bi, q_head0 + h, qi, 0)),
            pl.BlockSpec((1, 1, t, dh), lambda bi, h, qi: (bi, k_head0 + h, 0, 0)),
            pl.BlockSpec((1, 1, t, dh), lambda bi, h, qi: (bi, v_head0 + h, 0, 0)),
        ],
        out_specs=pl.BlockSpec((1, blk, dh), lambda bi, h, qi: (bi, qi, h)),
        scratch_shapes=[
            pltpu.VMEM((nb, blk, dh), BF16),
            pltpu.VMEM((nb, blk, dh), BF16),
        ],
        compiler_params=_params("parallel", "parallel", "arbitrary"),
    )(qkv, qkv, qkv)


def _xattn_kernel(q_ref, kv_ref, o_ref, *, n_heads, dh, scale):
    width = n_heads * dh
    for hh in range(n_heads):
        qh = q_ref[0, :, hh * dh:(hh + 1) * dh].astype(BF16)
        kh = kv_ref[0, :, hh * dh:(hh + 1) * dh].astype(BF16)
        vh = kv_ref[0, :, width + hh * dh:width + (hh + 1) * dh].astype(BF16)
        s = lax.dot_general(qh, kh, _NT, preferred_element_type=F32) * scale
        e = jnp.exp(s - jnp.max(s, axis=-1, keepdims=True))
        p = e / jnp.sum(e, axis=-1, keepdims=True)
        o_ref[0, :, hh * dh:(hh + 1) * dh] = jnp.dot(p.astype(BF16), vh, preferred_element_type=F32)


def cross_attention_core(q, kv, *, n_heads, tq=512):
    b, t, width = q.shape
    mlen = kv.shape[1]
    dh = width // n_heads
    tq = min(tq, t)
    kern = functools.partial(_xattn_kernel, n_heads=n_heads, dh=dh, scale=dh ** -0.5)
    return pl.pallas_call(
        kern,
        out_shape=jax.ShapeDtypeStruct((b, t, width), F32),
        grid=(b, t // tq),
        in_specs=[
            pl.BlockSpec((1, tq, width), lambda bi, i: (bi, i, 0)),
            pl.BlockSpec((1, mlen, 2 * width), lambda bi, i: (bi, 0, 0)),
        ],
        out_specs=pl.BlockSpec((1, tq, width), lambda bi, i: (bi, i, 0)),
        compiler_params=_params("parallel", "arbitrary"),
    )(q, kv)


def _extract_topk(work, n_rounds):
    rows = lax.broadcasted_iota(jnp.int32, work.shape, 0).astype(F32)
    n_rows = float(work.shape[0])
    taken = jnp.zeros(work.shape, F32)
    vals = []
    for _ in range(n_rounds):
        mx = jnp.max(work, axis=0, keepdims=True)
        first = jnp.min(jnp.where(work == mx, rows, n_rows), axis=0, keepdims=True)
        hit = rows == first
        work = jnp.where(hit, -jnp.inf, work)
        taken = jnp.where(hit, 1.0, taken)
        vals.append(mx)
    return vals, taken


def _peer_select_kernel(q_ref, keys_ref, s0_ref, e0_ref, s1_ref, e1_ref, tau_ref, *, topk):
    scores, vals, masks = [], [], []
    for p in range(2):
        qs = q_ref[:, p * LANES:(p + 1) * LANES].astype(BF16)
        s = lax.dot_general(keys_ref[p], qs, _NT, preferred_element_type=F32)
        v, taken = _extract_topk(s, topk)
        scores.append(s); vals.append(v); masks.append(taken)
    val1 = jnp.concatenate(vals[1], axis=0)
    cand = jnp.concatenate([val1 + vals[0][a] for a in range(topk)], axis=0)
    tops, _ = _extract_topk(cand, topk)
    z = jnp.zeros_like(tops[0])
    for tv in tops:
        z = z + jnp.exp(tv - tops[0])
    s0_ref[0] = scores[0]
    s1_ref[0] = scores[1]
    e0_ref[0] = jnp.where(masks[0] > 0, jnp.exp(scores[0] - vals[0][0]), 0.0) / z
    e1_ref[0] = jnp.where(masks[1] > 0, jnp.exp(scores[1] - vals[1][0]), 0.0)
    tau_ref[0] = tops[topk - 1]


def peer_select(q, keys_bf16, *, tm=256):
    n = q.shape[0]
    n_heads = keys_bf16.shape[0] // 2
    n_keys = keys_bf16.shape[1]
    tm = min(tm, n)
    big = jax.ShapeDtypeStruct((n_heads, n_keys, n), F32)
    big_spec = pl.BlockSpec((1, n_keys, tm), lambda i, h: (h, 0, i))
    kern = functools.partial(_peer_select_kernel, topk=PEER_TOPK)
    s0, e0, s1, e1, tau = pl.pallas_call(
        kern,
        out_shape=(big, big, big, big, jax.ShapeDtypeStruct((n_heads, 1, n), F32)),
        grid=(n // tm, n_heads),
        in_specs=[
            pl.BlockSpec((tm, 2 * LANES), lambda i, h: (i, h)),
            pl.BlockSpec((2, n_keys, keys_bf16.shape[2]), lambda i, h: (h, 0, 0)),
        ],
        out_specs=(big_spec, big_spec, big_spec, big_spec,
                   pl.BlockSpec((1, 1, tm), lambda i, h: (h, 0, i))),
        compiler_params=_params("parallel", "arbitrary"),
    )(q, keys_bf16)
    return s0, e0, s1, e1, tau.reshape(n_heads, n)


def _peer_main_kernel(x_ref, g_ref, u_ref, v_ref, s0_ref, e0_ref, s1_ref, e1_ref, tau_ref, o_ref,
                      h_ref, gate_ref, *, n_heads, n_keys, row_chunk):
    e = pl.program_id(1)
    tm = h_ref.shape[0]
    te = u_ref.shape[0]
    groups = te // n_keys

    @pl.when(e == 0)
    def _prologue():
        for r0 in range(0, tm, row_chunk):
            x = x_ref[r0:r0 + row_chunk, :]
            ms = jnp.mean(x * x, axis=-1, keepdims=True)
            h_ref[r0:r0 + row_chunk, :] = ((x * lax.rsqrt(ms + RMS_EPS)) * g_ref[...]).astype(BF16)
            o_ref[r0:r0 + row_chunk, :] = x

    act = lax.dot_general(u_ref[...], h_ref[...], _NT, preferred_element_type=F32)
    for gi in range(groups):
        i = e * groups + gi
        route = jnp.zeros((n_keys, tm), F32)
        for h in range(n_heads):
            a0 = s0_ref[h, pl.ds(i, 1), :]
            w0 = e0_ref[h, pl.ds(i, 1), :]
            cand = s1_ref[h] + a0
            route = route + jnp.where(cand >= tau_ref[h:h + 1, :], e1_ref[h] * w0, 0.0)
        a = act[gi * n_keys:(gi + 1) * n_keys, :]
        gelu = 0.5 * a * (1.0 + lax.erf(a * np.float32(np.sqrt(0.5))))
        gate_ref[gi * n_keys:(gi + 1) * n_keys, :] = (gelu * route).astype(BF16)
    o_ref[...] += lax.dot_general(gate_ref[...], v_ref[...], _TN, preferred_element_type=F32)


def peer_main(x, g, u_bf16, v_bf16, s0, e0, s1, e1, tau, *, tm=256, te=256):
    n, d = x.shape
    n_exp = u_bf16.shape[0]
    n_heads, n_keys, _ = s0.shape
    tm = min(tm, n)
    assert n % tm == 0 and n_exp % te == 0 and te % n_keys == 0
    big_spec = pl.BlockSpec((n_heads, n_keys, tm), lambda i, e: (0, 0, i))
    kern = functools.partial(_peer_main_kernel, n_heads=n_heads, n_keys=n_keys,
                             row_chunk=min(128, tm))
    return pl.pallas_call(
        kern,
        out_shape=jax.ShapeDtypeStruct((n, d), F32),
        grid=(n // tm, n_exp // te),
        in_specs=[
            pl.BlockSpec((tm, d), lambda i, e: (i, 0)),
            pl.BlockSpec((1, d), lambda i, e: (0, 0)),
            pl.BlockSpec((te, d), lambda i, e: (e, 0)),
            pl.BlockSpec((te, d), lambda i, e: (e, 0)),
            big_spec, big_spec, big_spec, big_spec,
            pl.BlockSpec((n_heads, tm), lambda i, e: (0, i)),
        ],
        out_specs=pl.BlockSpec((tm, d), lambda i, e: (i, 0)),
        scratch_shapes=[
            pltpu.VMEM((tm, d), BF16),
            pltpu.VMEM((te, tm), BF16),
        ],
        compiler_params=_params("parallel", "arbitrary"),
    )(x, g.reshape(1, d).astype(F32), u_bf16, v_bf16, s0, e0, s1, e1, tau)


def _layer(x, mem, norm_mix_g, w_in, moba_q_norm_g, moba_k_norm_g, moba_out_norm_g,
           sb_out_norm_g, w_out, norm_xattn_g, norm_mem_g, w_xq, w_xkv, xattn_q_norm_g,
           xattn_k_norm_g, w_xo, norm_ffn_g, w_peer_q, peer_sub_keys, peer_u, peer_v):
    b, t, d = x.shape
    n = b * t
    xf = x.reshape(n, d)
    mix_width = w_in.shape[1] // 3
    n_heads = mix_width // (2 * HEAD_DIM)
    grp = n_heads * HEAD_DIM
    slopes = jnp.asarray(2.0 ** (-8.0 * np.arange(1, n_heads + 1) / n_heads), dtype=F32)

    col_gain = jnp.concatenate([jnp.tile(moba_q_norm_g, n_heads), jnp.tile(moba_k_norm_g, n_heads),
                                jnp.ones((w_in.shape[1] - 2 * grp,), F32)])
    qkv = norm_matmul([xf], [norm_mix_g], w_in.astype(BF16), col_gain=col_gain,
                      n_norm_cols=2 * grp, head_major_bt=(b, t))
    o_moba = moba_attention(qkv, slopes, q_head0=0, k_head0=n_heads, v_head0=2 * n_heads,
                            n_heads=n_heads)
    o_sb = stick_breaking_attention(qkv, q_head0=3 * n_heads, k_head0=4 * n_heads,
                                    v_head0=5 * n_heads, n_heads=n_heads)
    x1 = norm_matmul([o_moba.reshape(n, grp), o_sb.reshape(n, grp)],
                     [moba_out_norm_g, sb_out_norm_g], w_out.astype(BF16), residual=xf)

    xw = w_xq.shape[1]
    n_mem = mem.shape[1]
    kv_gain = jnp.concatenate([jnp.tile(xattn_k_norm_g, N_XATTN_HEADS), jnp.ones((xw,), F32)])
    kv = norm_matmul([mem.reshape(b * n_mem, d)], [norm_mem_g], w_xkv.astype(BF16),
                     col_gain=kv_gain, n_norm_cols=xw)
    xq = norm_matmul([x1], [norm_xattn_g], w_xq.astype(BF16),
                     col_gain=jnp.tile(xattn_q_norm_g, N_XATTN_HEADS), n_norm_cols=xw)
    o_x = cross_attention_core(xq.reshape(b, t, xw), kv.reshape(b, n_mem, 2 * xw),
                               n_heads=N_XATTN_HEADS)
    x2 = norm_matmul([o_x.reshape(n, xw)], [None], w_xo.astype(BF16), residual=x1)

    pq = norm_matmul([x2], [norm_ffn_g], w_peer_q.astype(BF16))
    keys = peer_sub_keys.reshape(PEER_HEADS * 2, PEER_N_KEYS, -1).astype(BF16)
    s0, e0, s1, e1, tau = peer_select(pq, keys)
    x3 = peer_main(x2, norm_ffn_g, peer_u.astype(BF16), peer_v.astype(BF16), s0, e0, s1, e1, tau)
    return x3.reshape(b, t, d)


def kernel(x, mem, norm_mix_g, w_in, moba_q_norm_g, moba_k_norm_g, moba_out_norm_g, sb_out_norm_g,
           w_out, norm_xattn_g, norm_mem_g, w_xq, w_xkv, xattn_q_norm_g, xattn_k_norm_g, w_xo,
           norm_ffn_g, w_peer_q, peer_sub_keys, peer_u, peer_v):
    depth = w_in.shape[0]
    for l in range(depth):
        x = _layer(x, mem, norm_mix_g[l], w_in[l], moba_q_norm_g[l], moba_k_norm_g[l],
                   moba_out_norm_g[l], sb_out_norm_g[l], w_out[l], norm_xattn_g[l], norm_mem_g[l],
                   w_xq[l], w_xkv[l], xattn_q_norm_g[l], xattn_k_norm_g[l], w_xo[l], norm_ffn_g[l],
                   w_peer_q[l], peer_sub_keys[l], peer_u[l], peer_v[l])
    return x
```

```python
import functools

import numpy as np
import jax
import jax.numpy as jnp
from jax import lax
from jax.experimental import pallas as pl
from jax.experimental.pallas import tpu as pltpu

HEAD_DIM = 128
MOBA_BLOCK = 256
MOBA_TOPK = 3
N_XATTN_HEADS = 4
PEER_HEADS = 8
PEER_N_KEYS = 128
PEER_TOPK = 16
RMS_EPS = 1e-6
NEG_INF = -1e30
EXP_ZERO = -110.0
NORM_MARGIN = 1.02

LANES = 128
VMEM_LIMIT = 56 * 1024 * 1024

F32 = jnp.float32
BF16 = jnp.bfloat16

_NT = (((1,), (1,)), ((), ()))
_TN = (((0,), (0,)), ((), ()))


def _params(*sem):
    return pltpu.CompilerParams(dimension_semantics=sem, vmem_limit_bytes=VMEM_LIMIT)


def _split_bf16(x):
    hi = x.astype(BF16)
    lo = (x - hi.astype(F32)).astype(BF16)
    return hi, lo


def _norm_matmul_kernel(*refs, k_sizes, has_gain, n_norm_tiles, n_col_tiles, has_res,
                        head_major, row_chunk):
    n_in = len(k_sizes)
    pos = 0
    x_refs = refs[pos:pos + n_in]; pos += n_in
    g_refs = []
    for hg in has_gain:
        if hg:
            g_refs.append(refs[pos]); pos += 1
        else:
            g_refs.append(None)
    w_ref = refs[pos]; pos += 1
    cg_ref = None
    if n_norm_tiles > 0:
        cg_ref = refs[pos]; pos += 1
    res_ref = None
    if has_res:
        res_ref = refs[pos]; pos += 1
    o_ref = refs[pos]; pos += 1
    h_ref = refs[pos]

    j = pl.program_id(1)
    tm = h_ref.shape[0]

    @pl.when(j == 0)
    def _prologue():
        off = 0
        for x_ref, g_ref, ksz in zip(x_refs, g_refs, k_sizes):
            for r0 in range(0, tm, row_chunk):
                x = x_ref[r0:r0 + row_chunk, :]
                if g_ref is not None:
                    ms = jnp.mean(x * x, axis=-1, keepdims=True)
                    x = (x * lax.rsqrt(ms + RMS_EPS)) * g_ref[...]
                h_ref[r0:r0 + row_chunk, off:off + ksz] = x.astype(BF16)
            off += ksz

    acc = jnp.dot(h_ref[...], w_ref[...], preferred_element_type=F32)
    tn = acc.shape[1]

    def finish(get_group):
        for hh in range(tn // LANES):
            y = get_group(hh)
            if res_ref is not None:
                y = y + res_ref[:, hh * LANES:(hh + 1) * LANES]
            if head_major:
                o_ref[0, hh] = y
            else:
                o_ref[:, hh * LANES:(hh + 1) * LANES] = y

    def plain_group(hh):
        return acc[:, hh * LANES:(hh + 1) * LANES]

    def normed_group(hh):
        a = acc[:, hh * LANES:(hh + 1) * LANES]
        ms = jnp.mean(a * a, axis=-1, keepdims=True)
        return (a * lax.rsqrt(ms + RMS_EPS)) * cg_ref[:, hh * LANES:(hh + 1) * LANES]

    if n_norm_tiles == 0:
        finish(plain_group)
    elif n_norm_tiles >= n_col_tiles:
        finish(normed_group)
    else:
        @pl.when(j < n_norm_tiles)
        def _():
            finish(normed_group)

        @pl.when(j >= n_norm_tiles)
        def _():
            finish(plain_group)


def norm_matmul(xs, gains, w, *, col_gain=None, n_norm_cols=0, residual=None,
                head_major_bt=None, tm=512, tn=512):
    m = xs[0].shape[0]
    k_sizes = tuple(int(x.shape[1]) for x in xs)
    k_total = sum(k_sizes)
    n_cols = w.shape[1]
    tm = min(tm, m)
    tn = min(tn, n_cols)
    assert m % tm == 0 and n_cols % tn == 0 and w.shape[0] == k_total
    assert n_norm_cols % tn == 0
    n_col_tiles = n_cols // tn
    n_norm_tiles = n_norm_cols // tn
    has_gain = tuple(g is not None for g in gains)

    args, in_specs = [], []
    for x, ksz in zip(xs, k_sizes):
        args.append(x)
        in_specs.append(pl.BlockSpec((tm, ksz), lambda i, j: (i, 0)))
    for g, ksz in zip(gains, k_sizes):
        if g is not None:
            args.append(g.reshape(1, ksz).astype(F32))
            in_specs.append(pl.BlockSpec((1, ksz), lambda i, j: (0, 0)))
    args.append(w)
    in_specs.append(pl.BlockSpec((k_total, tn), lambda i, j: (0, j)))
    if n_norm_tiles > 0:
        args.append(col_gain.reshape(1, n_cols).astype(F32))
        in_specs.append(pl.BlockSpec((1, tn), lambda i, j: (0, j)))
    if residual is not None:
        args.append(residual)
        in_specs.append(pl.BlockSpec((tm, tn), lambda i, j: (i, j)))

    if head_major_bt is not None:
        b, t = head_major_bt
        assert b * t == m and t % tm == 0
        tiles_per_b = t // tm
        out_shape = jax.ShapeDtypeStruct((b, n_cols // LANES, t, LANES), F32)
        out_spec = pl.BlockSpec((1, tn // LANES, tm, LANES),
                                lambda i, j: (i // tiles_per_b, j, i % tiles_per_b, 0))
    else:
        out_shape = jax.ShapeDtypeStruct((m, n_cols), F32)
        out_spec = pl.BlockSpec((tm, tn), lambda i, j: (i, j))

    kern = functools.partial(
        _norm_matmul_kernel, k_sizes=k_sizes, has_gain=has_gain, n_norm_tiles=n_norm_tiles,
        n_col_tiles=n_col_tiles, has_res=residual is not None,
        head_major=head_major_bt is not None, row_chunk=min(128, tm))
    return pl.pallas_call(
        kern,
        out_shape=out_shape,
        grid=(m // tm, n_col_tiles),
        in_specs=in_specs,
        out_specs=out_spec,
        scratch_shapes=[pltpu.VMEM((tm, k_total), BF16)],
        compiler_params=_params("parallel", "arbitrary"),
    )(*args)


def _moba_kernel(slopes_ref, q_ref, k_ref, v_ref, o_ref,
                 kb_ref, vt_ref, kmean_ref, sel_ref, m_ref, l_ref, acc_ref, kn2_ref,
                 *, nb, blk, scale):
    h = pl.program_id(1)
    qi = pl.program_id(2)

    @pl.when(qi == 0)
    def _prep():
        kn2 = jnp.zeros((1, 1), F32)
        for n in range(nb):
            kblk = k_ref[0, 0, n * blk:(n + 1) * blk, :]
            kb_ref[n] = kblk.astype(BF16)
            kmean_ref[n:n + 1, :] = jnp.mean(kblk, axis=0, keepdims=True)
            vt_ref[n] = v_ref[0, 0, n * blk:(n + 1) * blk, :].T.astype(BF16)
            kn2 = jnp.maximum(kn2, jnp.max(jnp.sum(kblk * kblk, axis=1, keepdims=True),
                                           axis=0, keepdims=True))
        kn2_ref[...] = jnp.broadcast_to(kn2, kn2_ref.shape)

    q = q_ref[0, 0]
    qb = q.astype(BF16)
    slope = slopes_ref[h]

    kh, kl = _split_bf16(kmean_ref[...])
    qh, ql = _split_bf16(q)
    gate = (lax.dot_general(kh, qh, _NT, preferred_element_type=F32)
            + lax.dot_general(kh, ql, _NT, preferred_element_type=F32)
            + lax.dot_general(kl, qh, _NT, preferred_element_type=F32))
    blkid = lax.broadcasted_iota(jnp.int32, (nb, blk), 0)
    valid = blkid < qi
    gate = jnp.where(valid, gate, NEG_INF)
    rank = jnp.zeros((nb, blk), F32)
    for mth in range(nb):
        gm = gate[mth:mth + 1, :]
        beats = jnp.where(gm > gate, 1.0, jnp.where(gm == gate, jnp.where(blkid > mth, 1.0, 0.0), 0.0))
        rank = rank + beats
    sel_ref[...] = jnp.where(valid, jnp.where(rank < MOBA_TOPK, 1.0, 0.0), 0.0)

    t_idx = lax.broadcasted_iota(jnp.int32, (blk, blk), 1)
    j_idx = lax.broadcasted_iota(jnp.int32, (blk, blk), 0)
    dist_own = (t_idx - j_idx).astype(F32)

    s = lax.dot_general(kb_ref[qi], qb, _NT, preferred_element_type=F32) * scale
    s = s - slope * dist_own
    s = jnp.where(dist_own >= 0, s, NEG_INF)
    m0 = jnp.max(s, axis=0, keepdims=True)
    p = jnp.exp(s - m0)
    m_ref[...] = m0
    l_ref[...] = jnp.sum(p, axis=0, keepdims=True)
    acc_ref[...] = jnp.dot(vt_ref[qi], p.astype(BF16), preferred_element_type=F32)

    t2 = lax.broadcasted_iota(jnp.int32, (2 * blk, blk), 1)
    j2 = lax.broadcasted_iota(jnp.int32, (2 * blk, blk), 0)
    dist_pair = (t2 - j2).astype(F32)

    def body(pair, carry):
        n0 = 2 * pair
        sel0 = sel_ref[pl.ds(n0, 1), :]
        sel1 = sel_ref[pl.ds(n0 + 1, 1), :]
        kpair = jnp.concatenate([kb_ref[n0], kb_ref[n0 + 1]], axis=0)
        dist = dist_pair + ((qi - n0) * blk).astype(F32)
        s = lax.dot_general(kpair, qb, _NT, preferred_element_type=F32) * scale
        s = s - slope * dist
        s = jnp.concatenate([jnp.where(sel0 > 0, s[:blk], NEG_INF),
                             jnp.where(sel1 > 0, s[blk:], NEG_INF)], axis=0)
        m_old = m_ref[...]
        m_new = jnp.maximum(m_old, jnp.max(s, axis=0, keepdims=True))
        alpha = jnp.exp(m_old - m_new)
        p = jnp.exp(s - m_new)
        pb = p.astype(BF16)
        l_ref[...] = alpha * l_ref[...] + jnp.sum(p, axis=0, keepdims=True)
        acc_ref[...] = (alpha * acc_ref[...]
                        + jnp.dot(vt_ref[n0], pb[:blk], preferred_element_type=F32)
                        + jnp.dot(vt_ref[n0 + 1], pb[blk:], preferred_element_type=F32))
        m_ref[...] = m_new
        return carry

    qn2 = jnp.max(jnp.sum(q * q, axis=1, keepdims=True), axis=0, keepdims=True)
    bound = jnp.sqrt(qn2 * kn2_ref[:, 0:1]) * (scale * NORM_MARGIN)
    room = bound - jnp.min(m0, axis=1, keepdims=True) - EXP_ZERO
    d = lax.broadcasted_iota(jnp.int32, (1, LANES), 1).astype(F32)
    n_live = jnp.sum(jnp.where(d * (slope * blk) + slope <= room, 1, 0))
    first_block = jnp.maximum(qi - n_live, 0)

    lax.fori_loop(first_block // 2, (qi + 1) // 2, body, 0)
    o_ref[0] = (acc_ref[...] / l_ref[...]).T


def moba_attention(qkv, slopes, *, q_head0, k_head0, v_head0, n_heads):
    b, _, t, dh = qkv.shape
    blk = MOBA_BLOCK
    assert t % blk == 0
    nb = t // blk
    kern = functools.partial(_moba_kernel, nb=nb, blk=blk, scale=dh ** -0.5)
    return pl.pallas_call(
        kern,
        out_shape=jax.ShapeDtypeStruct((b, t, n_heads * dh), F32),
        grid=(b, n_heads, nb),
        in_specs=[
            pl.BlockSpec(memory_space=pltpu.SMEM),
            pl.BlockSpec((1, 1, blk, dh), lambda bi, h, qi: (bi, q_head0 + h, qi, 0)),
            pl.BlockSpec((1, 1, t, dh), lambda bi, h, qi: (bi, k_head0 + h, 0, 0)),
            pl.BlockSpec((1, 1, t, dh), lambda bi, h, qi: (bi, v_head0 + h, 0, 0)),
        ],
        out_specs=pl.BlockSpec((1, blk, dh), lambda bi, h, qi: (bi, qi, h)),
        scratch_shapes=[
            pltpu.VMEM((nb, blk, dh), BF16),
            pltpu.VMEM((nb, dh, blk), BF16),
            pltpu.VMEM((nb, dh), F32),
            pltpu.VMEM((nb, blk), F32),
            pltpu.VMEM((1, blk), F32),
            pltpu.VMEM((1, blk), F32),
            pltpu.VMEM((dh, blk), F32),
            pltpu.VMEM((1, LANES), F32),
        ],
        compiler_params=_params("parallel", "parallel", "arbitrary"),
    )(slopes, qkv, qkv, qkv)


def _sb_kernel(q_ref, k_ref, v_ref, o_ref, kb_ref, vb_ref, *, nb, blk, scale):
    qi = pl.program_id(2)

    @pl.when(qi == 0)
    def _prep():
        for n in range(nb):
            kb_ref[n] = k_ref[0, 0, n * blk:(n + 1) * blk, :].astype(BF16)
            vb_ref[n] = v_ref[0, 0, n * blk:(n + 1) * blk, :].astype(BF16)

    qb = q_ref[0, 0].astype(BF16)
    row = lax.broadcasted_iota(jnp.int32, (blk, blk), 0)
    col = lax.broadcasted_iota(jnp.int32, (blk, blk), 1)
    after = jnp.where(row > col, 1.0, 0.0).astype(BF16)
    strict = col < row

    def log_sigmoids(z):
        t = jnp.log(1.0 + jnp.exp(-jnp.abs(z)))
        log_beta = jnp.minimum(z, 0.0) - t
        return log_beta, log_beta - z

    def sum_after(x):
        hi, lo = _split_bf16(x)
        return (jnp.dot(hi, after, preferred_element_type=F32)
                + jnp.dot(lo, after, preferred_element_type=F32))

    def row_sum(x):
        return jnp.sum(x, axis=1, keepdims=True)

    def pair_step(n_late, late_is_own, acc, carry):
        has_early = n_late >= 1
        n_early = jnp.maximum(n_late - 1, 0)
        kpair = jnp.concatenate([kb_ref[n_early], kb_ref[n_late]], axis=0)
        z = lax.dot_general(qb, kpair, _NT, preferred_element_type=F32) * scale
        log_beta, log_om = log_sigmoids(z)
        om_early, om_late = log_om[:, :blk], log_om[:, blk:]
        if late_is_own:
            om_late = jnp.where(strict, om_late, 0.0)
            sum_late = row_sum(om_late)
            a_late = jnp.where(strict, jnp.exp(log_beta[:, blk:] + sum_after(om_late)), 0.0)
            left = sum_late
        else:
            sum_late = row_sum(om_late)
            a_late = jnp.exp(log_beta[:, blk:] + sum_after(om_late) + carry)
            left = carry + sum_late
        a_early = jnp.exp(log_beta[:, :blk] + sum_after(om_early) + left)
        v_early = vb_ref[n_early]
        v_early = jnp.where(has_early, v_early, jnp.zeros_like(v_early))
        contrib = (jnp.dot(a_late.astype(BF16), vb_ref[n_late], preferred_element_type=F32)
                   + jnp.dot(a_early.astype(BF16), v_early, preferred_element_type=F32))
        acc = contrib if acc is None else acc + contrib
        return acc, left + row_sum(om_early)

    acc0, carry0 = pair_step(qi, True, None, None)

    n_pairs = qi // 2

    def cond(state):
        pair, _, _, carry_max = state
        return jnp.logical_and(pair < n_pairs, carry_max > EXP_ZERO)

    def body(state):
        pair, acc, carry, _ = state
        acc, carry = pair_step(qi - 2 - 2 * pair, False, acc, carry)
        return pair + 1, acc, carry, jnp.max(carry)

    _, acc, _, _ = lax.while_loop(cond, body, (jnp.int32(0), acc0, carry0, jnp.max(carry0)))
    o_ref[0] = acc


def stick_breaking_attention(qkv, *, q_head0, k_head0, v_head0, n_heads, blk=256):
    b, _, t, dh = qkv.shape
    assert t % blk == 0
    nb = t // blk
    kern = functools.partial(_sb_kernel, nb=nb, blk=blk, scale=dh ** -0.5)
    return pl.pallas_call(
        kern,
        out_shape=jax.ShapeDtypeStruct((b, t, n_heads * dh), F32),
        grid=(b, n_heads, nb),
        in_specs=[
            pl.BlockSpec((1, 1, blk, dh), lambda bi, h, qi: (bi, q_head0 + h, qi, 0)),
            pl.BlockSpec((1, 1, t, dh), lambda bi, h, qi: (bi, k_head0 + h, 0, 0)),
            pl.BlockSpec((1, 1, t, dh), lambda bi, h, qi: (bi, v_head0 + h, 0, 0)),
        ],
        out_specs=pl.BlockSpec((1, blk, dh), lambda bi, h, qi: (bi, qi, h)),
        scratch_shapes=[
            pltpu.VMEM((nb, blk, dh), BF16),
            pltpu.VMEM((nb, blk, dh), BF16),
        ],
        compiler_params=_params("parallel", "parallel", "arbitrary"),
    )(qkv, qkv, qkv)


def _xattn_kernel(q_ref, kv_ref, o_ref, *, n_heads, dh, scale):
    width = n_heads * dh
    for hh in range(n_heads):
        qh = q_ref[0, :, hh * dh:(hh + 1) * dh].astype(BF16)
        kh = kv_ref[0, :, hh * dh:(hh + 1) * dh].astype(BF16)
        vh = kv_ref[0, :, width + hh * dh:width + (hh + 1) * dh].astype(BF16)
        s = lax.dot_general(qh, kh, _NT, preferred_element_type=F32) * scale
        e = jnp.exp(s - jnp.max(s, axis=-1, keepdims=True))
        p = e / jnp.sum(e, axis=-1, keepdims=True)
        o_ref[0, :, hh * dh:(hh + 1) * dh] = jnp.dot(p.astype(BF16), vh, preferred_element_type=F32)


def cross_attention_core(q, kv, *, n_heads, tq=512):
    b, t, width = q.shape
    mlen = kv.shape[1]
    dh = width // n_heads
    tq = min(tq, t)
    kern = functools.partial(_xattn_kernel, n_heads=n_heads, dh=dh, scale=dh ** -0.5)
    return pl.pallas_call(
        kern,
        out_shape=jax.ShapeDtypeStruct((b, t, width), F32),
        grid=(b, t // tq),
        in_specs=[
            pl.BlockSpec((1, tq, width), lambda bi, i: (bi, i, 0)),
            pl.BlockSpec((1, mlen, 2 * width), lambda bi, i: (bi, 0, 0)),
        ],
        out_specs=pl.BlockSpec((1, tq, width), lambda bi, i: (bi, i, 0)),
        compiler_params=_params("parallel", "arbitrary"),
    )(q, kv)


def _extract_topk(work, n_rounds):
    rows = lax.broadcasted_iota(jnp.int32, work.shape, 0).astype(F32)
    n_rows = float(work.shape[0])
    taken = jnp.zeros(work.shape, F32)
    vals = []
    for _ in range(n_rounds):
        mx = jnp.max(work, axis=0, keepdims=True)
        first = jnp.min(jnp.where(work == mx, rows, n_rows), axis=0, keepdims=True)
        hit = rows == first
        work = jnp.where(hit, -jnp.inf, work)
        taken = jnp.where(hit, 1.0, taken)
        vals.append(mx)
    return vals, taken


def _peer_select_kernel(q_ref, keys_ref, s0_ref, e0_ref, s1_ref, e1_ref, tau_ref, *, topk):
    scores, vals, masks = [], [], []
    for p in range(2):
        qs = q_ref[:, p * LANES:(p + 1) * LANES].astype(BF16)
        s = lax.dot_general(keys_ref[p], qs, _NT, preferred_element_type=F32)
        v, taken = _extract_topk(s, topk)
        scores.append(s); vals.append(v); masks.append(taken)
    val1 = jnp.concatenate(vals[1], axis=0)
    cand = jnp.concatenate([val1 + vals[0][a] for a in range(topk)], axis=0)
    tops, _ = _extract_topk(cand, topk)
    z = jnp.zeros_like(tops[0])
    for tv in tops:
        z = z + jnp.exp(tv - tops[0])
    s0_ref[0] = scores[0]
    s1_ref[0] = scores[1]
    e0_ref[0] = jnp.where(masks[0] > 0, jnp.exp(scores[0] - vals[0][0]), 0.0) / z
    e1_ref[0] = jnp.where(masks[1] > 0, jnp.exp(scores[1] - vals[1][0]), 0.0)
    tau_ref[0] = tops[topk - 1]


def peer_select(q, keys_bf16, *, tm=256):
    n = q.shape[0]
    n_heads = keys_bf16.shape[0] // 2
    n_keys = keys_bf16.shape[1]
    tm = min(tm, n)
    big = jax.ShapeDtypeStruct((n_heads, n_keys, n), F32)
    big_spec = pl.BlockSpec((1, n_keys, tm), lambda i, h: (h, 0, i))
    kern = functools.partial(_peer_select_kernel, topk=PEER_TOPK)
    s0, e0, s1, e1, tau = pl.pallas_call(
        kern,
        out_shape=(big, big, big, big, jax.ShapeDtypeStruct((n_heads, 1, n), F32)),
        grid=(n // tm, n_heads),
        in_specs=[
            pl.BlockSpec((tm, 2 * LANES), lambda i, h: (i, h)),
            pl.BlockSpec((2, n_keys, keys_bf16.shape[2]), lambda i, h: (h, 0, 0)),
        ],
        out_specs=(big_spec, big_spec, big_spec, big_spec,
                   pl.BlockSpec((1, 1, tm), lambda i, h: (h, 0, i))),
        compiler_params=_params("parallel", "arbitrary"),
    )(q, keys_bf16)
    return s0, e0, s1, e1, tau.reshape(n_heads, n)


def _peer_main_kernel(x_ref, g_ref, u_ref, v_ref, s0_ref, e0_ref, s1_ref, e1_ref, tau_ref, o_ref,
                      h_ref, gate_ref, *, n_heads, n_keys, row_chunk):
    e = pl.program_id(1)
    tm = h_ref.shape[0]
    te = u_ref.shape[0]
    groups = te // n_keys

    @pl.when(e == 0)
    def _prologue():
        for r0 in range(0, tm, row_chunk):
            x = x_ref[r0:r0 + row_chunk, :]
            ms = jnp.mean(x * x, axis=-1, keepdims=True)
            h_ref[r0:r0 + row_chunk, :] = ((x * lax.rsqrt(ms + RMS_EPS)) * g_ref[...]).astype(BF16)
            o_ref[r0:r0 + row_chunk, :] = x

    act = lax.dot_general(u_ref[...], h_ref[...], _NT, preferred_element_type=F32)
    for gi in range(groups):
        i = e * groups + gi
        route = jnp.zeros((n_keys, tm), F32)
        for h in range(n_heads):
            a0 = s0_ref[h, pl.ds(i, 1), :]
            w0 = e0_ref[h, pl.ds(i, 1), :]
            cand = s1_ref[h] + a0
            route = route + jnp.where(cand >= tau_ref[h:h + 1, :], e1_ref[h] * w0, 0.0)
        a = act[gi * n_keys:(gi + 1) * n_keys, :]
        gelu = 0.5 * a * (1.0 + lax.erf(a * np.float32(np.sqrt(0.5))))
        gate_ref[gi * n_keys:(gi + 1) * n_keys, :] = (gelu * route).astype(BF16)
    o_ref[...] += lax.dot_general(gate_ref[...], v_ref[...], _TN, preferred_element_type=F32)


def peer_main(x, g, u_bf16, v_bf16, s0, e0, s1, e1, tau, *, tm=512, te=512):
    n, d = x.shape
    n_exp = u_bf16.shape[0]
    n_heads, n_keys, _ = s0.shape
    tm = min(tm, n)
    assert n % tm == 0 and n_exp % te == 0 and te % n_keys == 0
    once = pl.Buffered(1)
    big_spec = pl.BlockSpec((n_heads, n_keys, tm), lambda i, e: (0, 0, i), pipeline_mode=once)
    kern = functools.partial(_peer_main_kernel, n_heads=n_heads, n_keys=n_keys,
                             row_chunk=min(128, tm))
    return pl.pallas_call(
        kern,
        out_shape=jax.ShapeDtypeStruct((n, d), F32),
        grid=(n // tm, n_exp // te),
        in_specs=[
            pl.BlockSpec((tm, d), lambda i, e: (i, 0), pipeline_mode=once),
            pl.BlockSpec((1, d), lambda i, e: (0, 0)),
            pl.BlockSpec((te, d), lambda i, e: (e, 0)),
            pl.BlockSpec((te, d), lambda i, e: (e, 0)),
            big_spec, big_spec, big_spec, big_spec,
            pl.BlockSpec((n_heads, tm), lambda i, e: (0, i)),
        ],
        out_specs=pl.BlockSpec((tm, d), lambda i, e: (i, 0), pipeline_mode=once),
        scratch_shapes=[
            pltpu.VMEM((tm, d), BF16),
            pltpu.VMEM((te, tm), BF16),
        ],
        compiler_params=_params("parallel", "arbitrary"),
    )(x, g.reshape(1, d).astype(F32), u_bf16, v_bf16, s0, e0, s1, e1, tau)


def _layer(x, mem, norm_mix_g, w_in, moba_q_norm_g, moba_k_norm_g, moba_out_norm_g,
           sb_out_norm_g, w_out, norm_xattn_g, norm_mem_g, w_xq, w_xkv, xattn_q_norm_g,
           xattn_k_norm_g, w_xo, norm_ffn_g, w_peer_q, peer_sub_keys, peer_u, peer_v):
    b, t, d = x.shape
    n = b * t
    xf = x.reshape(n, d)
    mix_width = w_in.shape[1] // 3
    n_heads = mix_width // (2 * HEAD_DIM)
    grp = n_heads * HEAD_DIM
    slopes = jnp.asarray(2.0 ** (-8.0 * np.arange(1, n_heads + 1) / n_heads), dtype=F32)

    col_gain = jnp.concatenate([jnp.tile(moba_q_norm_g, n_heads), jnp.tile(moba_k_norm_g, n_heads),
                                jnp.ones((w_in.shape[1] - 2 * grp,), F32)])
    qkv = norm_matmul([xf], [norm_mix_g], w_in.astype(BF16), col_gain=col_gain,
                      n_norm_cols=2 * grp, head_major_bt=(b, t))
    o_moba = moba_attention(qkv, slopes, q_head0=0, k_head0=n_heads, v_head0=2 * n_heads,
                            n_heads=n_heads)
    o_sb = stick_breaking_attention(qkv, q_head0=3 * n_heads, k_head0=4 * n_heads,
                                    v_head0=5 * n_heads, n_heads=n_heads)
    x1 = norm_matmul([o_moba.reshape(n, grp), o_sb.reshape(n, grp)],
                     [moba_out_norm_g, sb_out_norm_g], w_out.astype(BF16), residual=xf)

    xw = w_xq.shape[1]
    n_mem = mem.shape[1]
    kv_gain = jnp.concatenate([jnp.tile(xattn_k_norm_g, N_XATTN_HEADS), jnp.ones((xw,), F32)])
    kv = norm_matmul([mem.reshape(b * n_mem, d)], [norm_mem_g], w_xkv.astype(BF16),
                     col_gain=kv_gain, n_norm_cols=xw)
    xq = norm_matmul([x1], [norm_xattn_g], w_xq.astype(BF16),
                     col_gain=jnp.tile(xattn_q_norm_g, N_XATTN_HEADS), n_norm_cols=xw)
    o_x = cross_attention_core(xq.reshape(b, t, xw), kv.reshape(b, n_mem, 2 * xw),
                               n_heads=N_XATTN_HEADS)
    x2 = norm_matmul([o_x.reshape(n, xw)], [None], w_xo.astype(BF16), residual=x1)

    pq = norm_matmul([x2], [norm_ffn_g], w_peer_q.astype(BF16))
    keys = peer_sub_keys.reshape(PEER_HEADS * 2, PEER_N_KEYS, -1).astype(BF16)
    s0, e0, s1, e1, tau = peer_select(pq, keys)
    x3 = peer_main(x2, norm_ffn_g, peer_u.astype(BF16), peer_v.astype(BF16), s0, e0, s1, e1, tau)
    return x3.reshape(b, t, d)


def kernel(x, mem, norm_mix_g, w_in, moba_q_norm_g, moba_k_norm_g, moba_out_norm_g, sb_out_norm_g,
           w_out, norm_xattn_g, norm_mem_g, w_xq, w_xkv, xattn_q_norm_g, xattn_k_norm_g, w_xo,
           norm_ffn_g, w_peer_q, peer_sub_keys, peer_u, peer_v):
    depth = w_in.shape[0]
    for l in range(depth):
        x = _layer(x, mem, norm_mix_g[l], w_in[l], moba_q_norm_g[l], moba_k_norm_g[l],
                   moba_out_norm_g[l], sb_out_norm_g[l], w_out[l], norm_xattn_g[l], norm_mem_g[l],
                   w_xq[l], w_xkv[l], xattn_q_norm_g[l], xattn_k_norm_g[l], w_xo[l], norm_ffn_g[l],
                   w_peer_q[l], peer_sub_keys[l], peer_u[l], peer_v[l])
    return x
```

```python
import functools

import numpy as np
import jax
import jax.numpy as jnp
from jax import lax
from jax.experimental import pallas as pl
from jax.experimental.pallas import tpu as pltpu

HEAD_DIM = 128
MOBA_BLOCK = 256
MOBA_TOPK = 3
N_XATTN_HEADS = 4
PEER_HEADS = 8
PEER_N_KEYS = 128
PEER_TOPK = 16
RMS_EPS = 1e-6
NEG_INF = -1e30
EXP_ZERO = -110.0
NORM_MARGIN = 1.02

LANES = 128
VMEM_LIMIT = 56 * 1024 * 1024

F32 = jnp.float32
BF16 = jnp.bfloat16

_NT = (((1,), (1,)), ((), ()))
_TN = (((0,), (0,)), ((), ()))


def _params(*sem):
    return pltpu.CompilerParams(dimension_semantics=sem, vmem_limit_bytes=VMEM_LIMIT)


def _split_bf16(x):
    hi = x.astype(BF16)
    lo = (x - hi.astype(F32)).astype(BF16)
    return hi, lo


def _norm_matmul_kernel(*refs, k_sizes, has_gain, n_norm_tiles, n_col_tiles, has_res,
                        head_major, row_chunk):
    n_in = len(k_sizes)
    pos = 0
    x_refs = refs[pos:pos + n_in]; pos += n_in
    g_refs = []
    for hg in has_gain:
        if hg:
            g_refs.append(refs[pos]); pos += 1
        else:
            g_refs.append(None)
    w_ref = refs[pos]; pos += 1
    cg_ref = None
    if n_norm_tiles > 0:
        cg_ref = refs[pos]; pos += 1
    res_ref = None
    if has_res:
        res_ref = refs[pos]; pos += 1
    o_ref = refs[pos]; pos += 1
    h_ref = refs[pos]

    j = pl.program_id(1)
    tm = h_ref.shape[0]

    @pl.when(j == 0)
    def _prologue():
        off = 0
        for x_ref, g_ref, ksz in zip(x_refs, g_refs, k_sizes):
            for r0 in range(0, tm, row_chunk):
                x = x_ref[r0:r0 + row_chunk, :]
                if g_ref is not None:
                    ms = jnp.mean(x * x, axis=-1, keepdims=True)
                    x = (x * lax.rsqrt(ms + RMS_EPS)) * g_ref[...]
                h_ref[r0:r0 + row_chunk, off:off + ksz] = x.astype(BF16)
            off += ksz

    acc = jnp.dot(h_ref[...], w_ref[...], preferred_element_type=F32)
    tn = acc.shape[1]

    def finish(get_group):
        for hh in range(tn // LANES):
            y = get_group(hh)
            if res_ref is not None:
                y = y + res_ref[:, hh * LANES:(hh + 1) * LANES]
            if head_major:
                o_ref[0, hh] = y
            else:
                o_ref[:, hh * LANES:(hh + 1) * LANES] = y

    def plain_group(hh):
        return acc[:, hh * LANES:(hh + 1) * LANES]

    def normed_group(hh):
        a = acc[:, hh * LANES:(hh + 1) * LANES]
        ms = jnp.mean(a * a, axis=-1, keepdims=True)
        return (a * lax.rsqrt(ms + RMS_EPS)) * cg_ref[:, hh * LANES:(hh + 1) * LANES]

    if n_norm_tiles == 0:
        finish(plain_group)
    elif n_norm_tiles >= n_col_tiles:
        finish(normed_group)
    else:
        @pl.when(j < n_norm_tiles)
        def _():
            finish(normed_group)

        @pl.when(j >= n_norm_tiles)
        def _():
            finish(plain_group)


def norm_matmul(xs, gains, w, *, col_gain=None, n_norm_cols=0, residual=None,
                head_major_bt=None, tm=512, tn=512):
    m = xs[0].shape[0]
    k_sizes = tuple(int(x.shape[1]) for x in xs)
    k_total = sum(k_sizes)
    n_cols = w.shape[1]
    tm = min(tm, m)
    tn = min(tn, n_cols)
    assert m % tm == 0 and n_cols % tn == 0 and w.shape[0] == k_total
    assert n_norm_cols % tn == 0
    n_col_tiles = n_cols // tn
    n_norm_tiles = n_norm_cols // tn
    has_gain = tuple(g is not None for g in gains)

    args, in_specs = [], []
    for x, ksz in zip(xs, k_sizes):
        args.append(x)
        in_specs.append(pl.BlockSpec((tm, ksz), lambda i, j: (i, 0)))
    for g, ksz in zip(gains, k_sizes):
        if g is not None:
            args.append(g.reshape(1, ksz).astype(F32))
            in_specs.append(pl.BlockSpec((1, ksz), lambda i, j: (0, 0)))
    args.append(w)
    in_specs.append(pl.BlockSpec((k_total, tn), lambda i, j: (0, j)))
    if n_norm_tiles > 0:
        args.append(col_gain.reshape(1, n_cols).astype(F32))
        in_specs.append(pl.BlockSpec((1, tn), lambda i, j: (0, j)))
    if residual is not None:
        args.append(residual)
        in_specs.append(pl.BlockSpec((tm, tn), lambda i, j: (i, j)))

    if head_major_bt is not None:
        b, t = head_major_bt
        assert b * t == m and t % tm == 0
        tiles_per_b = t // tm
        out_shape = jax.ShapeDtypeStruct((b, n_cols // LANES, t, LANES), F32)
        out_spec = pl.BlockSpec((1, tn // LANES, tm, LANES),
                                lambda i, j: (i // tiles_per_b, j, i % tiles_per_b, 0))
    else:
        out_shape = jax.ShapeDtypeStruct((m, n_cols), F32)
        out_spec = pl.BlockSpec((tm, tn), lambda i, j: (i, j))

    kern = functools.partial(
        _norm_matmul_kernel, k_sizes=k_sizes, has_gain=has_gain, n_norm_tiles=n_norm_tiles,
        n_col_tiles=n_col_tiles, has_res=residual is not None,
        head_major=head_major_bt is not None, row_chunk=min(128, tm))
    return pl.pallas_call(
        kern,
        out_shape=out_shape,
        grid=(m // tm, n_col_tiles),
        in_specs=in_specs,
        out_specs=out_spec,
        scratch_shapes=[pltpu.VMEM((tm, k_total), BF16)],
        compiler_params=_params("parallel", "arbitrary"),
    )(*args)


def _moba_kernel(slopes_ref, q_ref, k_ref, v_ref, o_ref,
                 kb_ref, vt_ref, kmean_ref, sel_ref, m_ref, l_ref, acc_ref, kn2_ref, s_ref,
                 *, nb, blk, scale):
    h = pl.program_id(1)
    qi = pl.program_id(2)

    @pl.when(qi == 0)
    def _prep():
        kn2 = jnp.zeros((1, 1), F32)
        for n in range(nb):
            kblk = k_ref[0, 0, n * blk:(n + 1) * blk, :]
            kb_ref[n] = kblk.astype(BF16)
            kmean_ref[n:n + 1, :] = jnp.mean(kblk, axis=0, keepdims=True)
            vt_ref[n] = v_ref[0, 0, n * blk:(n + 1) * blk, :].T.astype(BF16)
            kn2 = jnp.maximum(kn2, jnp.max(jnp.sum(kblk * kblk, axis=1, keepdims=True),
                                           axis=0, keepdims=True))
        kn2_ref[...] = jnp.broadcast_to(kn2, kn2_ref.shape)

    q = q_ref[0, 0]
    qb = q.astype(BF16)
    slope = slopes_ref[h]

    kh, kl = _split_bf16(kmean_ref[...])
    qh, ql = _split_bf16(q)
    gate = (lax.dot_general(kh, qh, _NT, preferred_element_type=F32)
            + lax.dot_general(kh, ql, _NT, preferred_element_type=F32)
            + lax.dot_general(kl, qh, _NT, preferred_element_type=F32))
    blkid = lax.broadcasted_iota(jnp.int32, (nb, blk), 0)
    valid = blkid < qi
    gate = jnp.where(valid, gate, NEG_INF)
    rank = jnp.zeros((nb, blk), F32)
    for mth in range(nb):
        gm = gate[mth:mth + 1, :]
        beats = jnp.where(gm > gate, 1.0, jnp.where(gm == gate, jnp.where(blkid > mth, 1.0, 0.0), 0.0))
        rank = rank + beats
    sel_ref[...] = jnp.where(valid, jnp.where(rank < MOBA_TOPK, 1.0, 0.0), 0.0)

    t_idx = lax.broadcasted_iota(jnp.int32, (blk, blk), 1)
    j_idx = lax.broadcasted_iota(jnp.int32, (blk, blk), 0)
    dist_own = (t_idx - j_idx).astype(F32)

    s = lax.dot_general(kb_ref[qi], qb, _NT, preferred_element_type=F32) * scale
    s = s - slope * dist_own
    s = jnp.where(dist_own >= 0, s, NEG_INF)
    m0 = jnp.max(s, axis=0, keepdims=True)
    p = jnp.exp(s - m0)
    m_ref[...] = m0
    l_ref[...] = jnp.sum(p, axis=0, keepdims=True)
    acc_ref[...] = jnp.dot(vt_ref[qi], p.astype(BF16), preferred_element_type=F32)

    t2 = lax.broadcasted_iota(jnp.int32, (2 * blk, blk), 1)
    j2 = lax.broadcasted_iota(jnp.int32, (2 * blk, blk), 0)
    dist_pair = (t2 - j2).astype(F32)

    def pair_scores(pair):
        n0 = 2 * pair
        sel0 = sel_ref[pl.ds(n0, 1), :]
        sel1 = sel_ref[pl.ds(n0 + 1, 1), :]
        kpair = jnp.concatenate([kb_ref[n0], kb_ref[n0 + 1]], axis=0)
        dist = dist_pair + ((qi - n0) * blk).astype(F32)
        s = lax.dot_general(kpair, qb, _NT, preferred_element_type=F32) * scale
        s = s - slope * dist
        return jnp.concatenate([jnp.where(sel0 > 0, s[:blk], NEG_INF),
                                jnp.where(sel1 > 0, s[blk:], NEG_INF)], axis=0)

    last_pair = nb // 2 - 1

    def body(pair, carry):
        n0 = 2 * pair
        s = s_ref[pair % 2]
        s_ref[(pair + 1) % 2] = pair_scores(jnp.minimum(pair + 1, last_pair))
        m_old = m_ref[...]
        m_new = jnp.maximum(m_old, jnp.max(s, axis=0, keepdims=True))
        alpha = jnp.exp(m_old - m_new)
        p = jnp.exp(s - m_new)
        pb = p.astype(BF16)
        l_ref[...] = alpha * l_ref[...] + jnp.sum(p, axis=0, keepdims=True)
        acc_ref[...] = (alpha * acc_ref[...]
                        + jnp.dot(vt_ref[n0], pb[:blk], preferred_element_type=F32)
                        + jnp.dot(vt_ref[n0 + 1], pb[blk:], preferred_element_type=F32))
        m_ref[...] = m_new
        return carry

    qn2 = jnp.max(jnp.sum(q * q, axis=1, keepdims=True), axis=0, keepdims=True)
    bound = jnp.sqrt(qn2 * kn2_ref[:, 0:1]) * (scale * NORM_MARGIN)
    room = bound - jnp.min(m0, axis=1, keepdims=True) - EXP_ZERO
    d = lax.broadcasted_iota(jnp.int32, (1, LANES), 1).astype(F32)
    n_live = jnp.sum(jnp.where(d * (slope * blk) + slope <= room, 1, 0))
    first_pair = jnp.maximum(qi - n_live, 0) // 2

    s_ref[first_pair % 2] = pair_scores(jnp.minimum(first_pair, last_pair))
    lax.fori_loop(first_pair, (qi + 1) // 2, body, 0)
    o_ref[0] = (acc_ref[...] / l_ref[...]).T


def moba_attention(qkv, slopes, *, q_head0, k_head0, v_head0, n_heads):
    b, _, t, dh = qkv.shape
    blk = MOBA_BLOCK
    assert t % (2 * blk) == 0
    nb = t // blk
    kern = functools.partial(_moba_kernel, nb=nb, blk=blk, scale=dh ** -0.5)
    return pl.pallas_call(
        kern,
        out_shape=jax.ShapeDtypeStruct((b, t, n_heads * dh), F32),
        grid=(b, n_heads, nb),
        in_specs=[
            pl.BlockSpec(memory_space=pltpu.SMEM),
            pl.BlockSpec((1, 1, blk, dh), lambda bi, h, qi: (bi, q_head0 + h, qi, 0)),
            pl.BlockSpec((1, 1, t, dh), lambda bi, h, qi: (bi, k_head0 + h, 0, 0)),
            pl.BlockSpec((1, 1, t, dh), lambda bi, h, qi: (bi, v_head0 + h, 0, 0)),
        ],
        out_specs=pl.BlockSpec((1, blk, dh), lambda bi, h, qi: (bi, qi, h)),
        scratch_shapes=[
            pltpu.VMEM((nb, blk, dh), BF16),
            pltpu.VMEM((nb, dh, blk), BF16),
            pltpu.VMEM((nb, dh), F32),
            pltpu.VMEM((nb, blk), F32),
            pltpu.VMEM((1, blk), F32),
            pltpu.VMEM((1, blk), F32),
            pltpu.VMEM((dh, blk), F32),
            pltpu.VMEM((1, LANES), F32),
            pltpu.VMEM((2, 2 * blk, blk), F32),
        ],
        compiler_params=_params("parallel", "parallel", "arbitrary"),
    )(slopes, qkv, qkv, qkv)


def _sb_kernel(q_ref, k_ref, v_ref, o_ref, kb_ref, vb_ref, *, nb, blk, scale):
    qi = pl.program_id(2)

    @pl.when(qi == 0)
    def _prep():
        for n in range(nb):
            kb_ref[n] = k_ref[0, 0, n * blk:(n + 1) * blk, :].astype(BF16)
            vb_ref[n] = v_ref[0, 0, n * blk:(n + 1) * blk, :].astype(BF16)

    qb = q_ref[0, 0].astype(BF16)
    row = lax.broadcasted_iota(jnp.int32, (blk, blk), 0)
    col = lax.broadcasted_iota(jnp.int32, (blk, blk), 1)
    after = jnp.where(row > col, 1.0, 0.0).astype(BF16)
    strict = col < row

    def log_sigmoids(z):
        t = jnp.log(1.0 + jnp.exp(-jnp.abs(z)))
        log_beta = jnp.minimum(z, 0.0) - t
        return log_beta, log_beta - z

    def sum_after(x):
        hi, lo = _split_bf16(x)
        return (jnp.dot(hi, after, preferred_element_type=F32)
                + jnp.dot(lo, after, preferred_element_type=F32))

    def row_sum(x):
        return jnp.sum(x, axis=1, keepdims=True)

    def pair_step(n_late, late_is_own, acc, carry):
        has_early = n_late >= 1
        n_early = jnp.maximum(n_late - 1, 0)
        kpair = jnp.concatenate([kb_ref[n_early], kb_ref[n_late]], axis=0)
        z = lax.dot_general(qb, kpair, _NT, preferred_element_type=F32) * scale
        log_beta, log_om = log_sigmoids(z)
        om_early, om_late = log_om[:, :blk], log_om[:, blk:]
        if late_is_own:
            om_late = jnp.where(strict, om_late, 0.0)
            sum_late = row_sum(om_late)
            a_late = jnp.where(strict, jnp.exp(log_beta[:, blk:] + sum_after(om_late)), 0.0)
            left = sum_late
        else:
            sum_late = row_sum(om_late)
            a_late = jnp.exp(log_beta[:, blk:] + sum_after(om_late) + carry)
            left = carry + sum_late
        a_early = jnp.exp(log_beta[:, :blk] + sum_after(om_early) + left)
        v_early = vb_ref[n_early]
        v_early = jnp.where(has_early, v_early, jnp.zeros_like(v_early))
        contrib = (jnp.dot(a_late.astype(BF16), vb_ref[n_late], preferred_element_type=F32)
                   + jnp.dot(a_early.astype(BF16), v_early, preferred_element_type=F32))
        acc = contrib if acc is None else acc + contrib
        return acc, left + row_sum(om_early)

    acc0, carry0 = pair_step(qi, True, None, None)

    n_pairs = qi // 2

    def cond(state):
        pair, _, _, carry_max = state
        return jnp.logical_and(pair < n_pairs, carry_max > EXP_ZERO)

    def body(state):
        pair, acc, carry, _ = state
        acc, carry = pair_step(qi - 2 - 2 * pair, False, acc, carry)
        return pair + 1, acc, carry, jnp.max(carry)

    _, acc, _, _ = lax.while_loop(cond, body, (jnp.int32(0), acc0, carry0, jnp.max(carry0)))
    o_ref[0] = acc


def stick_breaking_attention(qkv, *, q_head0, k_head0, v_head0, n_heads, blk=256):
    b, _, t, dh = qkv.shape
    assert t % blk == 0
    nb = t // blk
    kern = functools.partial(_sb_kernel, nb=nb, blk=blk, scale=dh ** -0.5)
    return pl.pallas_call(
        kern,
        out_shape=jax.ShapeDtypeStruct((b, t, n_heads * dh), F32),
        grid=(b, n_heads, nb),
        in_specs=[
            pl.BlockSpec((1, 1, blk, dh), lambda bi, h, qi: (bi, q_head0 + h, qi, 0)),
            pl.BlockSpec((1, 1, t, dh), lambda bi, h, qi: (bi, k_head0 + h, 0, 0)),
            pl.BlockSpec((1, 1, t, dh), lambda bi, h, qi: (bi, v_head0 + h, 0, 0)),
        ],
        out_specs=pl.BlockSpec((1, blk, dh), lambda bi, h, qi: (bi, qi, h)),
        scratch_shapes=[
            pltpu.VMEM((nb, blk, dh), BF16),
            pltpu.VMEM((nb, blk, dh), BF16),
        ],
        compiler_params=_params("parallel", "parallel", "arbitrary"),
    )(qkv, qkv, qkv)


def _xattn_kernel(q_ref, kv_ref, o_ref, *, n_heads, dh, scale):
    width = n_heads * dh
    for hh in range(n_heads):
        qh = q_ref[0, :, hh * dh:(hh + 1) * dh].astype(BF16)
        kh = kv_ref[0, :, hh * dh:(hh + 1) * dh].astype(BF16)
        vh = kv_ref[0, :, width + hh * dh:width + (hh + 1) * dh].astype(BF16)
        s = lax.dot_general(qh, kh, _NT, preferred_element_type=F32) * scale
        e = jnp.exp(s - jnp.max(s, axis=-1, keepdims=True))
        p = e / jnp.sum(e, axis=-1, keepdims=True)
        o_ref[0, :, hh * dh:(hh + 1) * dh] = jnp.dot(p.astype(BF16), vh, preferred_element_type=F32)


def cross_attention_core(q, kv, *, n_heads, tq=512):
    b, t, width = q.shape
    mlen = kv.shape[1]
    dh = width // n_heads
    tq = min(tq, t)
    kern = functools.partial(_xattn_kernel, n_heads=n_heads, dh=dh, scale=dh ** -0.5)
    return pl.pallas_call(
        kern,
        out_shape=jax.ShapeDtypeStruct((b, t, width), F32),
        grid=(b, t // tq),
        in_specs=[
            pl.BlockSpec((1, tq, width), lambda bi, i: (bi, i, 0)),
            pl.BlockSpec((1, mlen, 2 * width), lambda bi, i: (bi, 0, 0)),
        ],
        out_specs=pl.BlockSpec((1, tq, width), lambda bi, i: (bi, i, 0)),
        compiler_params=_params("parallel", "arbitrary"),
    )(q, kv)


def _extract_topk(work, n_rounds):
    rows = lax.broadcasted_iota(jnp.int32, work.shape, 0).astype(F32)
    n_rows = float(work.shape[0])
    vals = []
    for _ in range(n_rounds):
        mx = jnp.max(work, axis=0, keepdims=True)
        first = jnp.min(jnp.where(work == mx, rows, n_rows), axis=0, keepdims=True)
        work = jnp.where(rows == first, -jnp.inf, work)
        vals.append(mx)
    return vals, work


def _staircase_candidates(vals0, vals1, topk):
    sub = 8
    val1_all = jnp.concatenate(vals1, axis=0)
    val1_head = val1_all[:sub]
    row = lax.broadcasted_iota(jnp.int32, val1_head.shape, 0)
    slabs = [val1_all + vals0[0]]
    for a in range(1, sub):
        b_max = topk // (a + 1) - 1
        slab = val1_head + vals0[a]
        slabs.append(slab if b_max >= sub - 1 else jnp.where(row <= b_max, slab, -jnp.inf))
    slabs.append(jnp.concatenate(vals0[sub:], axis=0) + vals1[0])
    return jnp.concatenate(slabs, axis=0)


def _peer_select_kernel(q_ref, keys_ref, s0_ref, e0_ref, s1_ref, e1_ref, tau_ref, *, topk):
    scores, vals, left = [], [], []
    for p in range(2):
        qs = q_ref[:, p * LANES:(p + 1) * LANES].astype(BF16)
        s = lax.dot_general(keys_ref[p], qs, _NT, preferred_element_type=F32)
        v, rest = _extract_topk(s, topk)
        scores.append(s); vals.append(v); left.append(rest)
    tops, _ = _extract_topk(_staircase_candidates(vals[0], vals[1], topk), topk)
    z = jnp.zeros_like(tops[0])
    for tv in tops:
        z = z + jnp.exp(tv - tops[0])
    s0_ref[0] = scores[0]
    s1_ref[0] = scores[1]
    e0_ref[0] = jnp.where(left[0] == -jnp.inf, jnp.exp(scores[0] - vals[0][0]), 0.0) / z
    e1_ref[0] = jnp.where(left[1] == -jnp.inf, jnp.exp(scores[1] - vals[1][0]), 0.0)
    tau_ref[0] = tops[topk - 1]


def peer_select(q, keys_bf16, *, tm=256):
    n = q.shape[0]
    n_heads = keys_bf16.shape[0] // 2
    n_keys = keys_bf16.shape[1]
    tm = min(tm, n)
    big = jax.ShapeDtypeStruct((n_heads, n_keys, n), F32)
    big_spec = pl.BlockSpec((1, n_keys, tm), lambda i, h: (h, 0, i))
    kern = functools.partial(_peer_select_kernel, topk=PEER_TOPK)
    s0, e0, s1, e1, tau = pl.pallas_call(
        kern,
        out_shape=(big, big, big, big, jax.ShapeDtypeStruct((n_heads, 1, n), F32)),
        grid=(n // tm, n_heads),
        in_specs=[
            pl.BlockSpec((tm, 2 * LANES), lambda i, h: (i, h)),
            pl.BlockSpec((2, n_keys, keys_bf16.shape[2]), lambda i, h: (h, 0, 0)),
        ],
        out_specs=(big_spec, big_spec, big_spec, big_spec,
                   pl.BlockSpec((1, 1, tm), lambda i, h: (h, 0, i))),
        compiler_params=_params("parallel", "arbitrary"),
    )(q, keys_bf16)
    return s0, e0, s1, e1, tau.reshape(n_heads, n)


def _peer_main_kernel(x_ref, g_ref, u_ref, v_ref, s0_ref, e0_ref, s1_ref, e1_ref, tau_ref, o_ref,
                      h_ref, gate_ref, *, n_heads, n_keys, row_chunk):
    e = pl.program_id(1)
    tm = h_ref.shape[0]
    te = u_ref.shape[0]
    groups = te // n_keys

    @pl.when(e == 0)
    def _prologue():
        for r0 in range(0, tm, row_chunk):
            x = x_ref[r0:r0 + row_chunk, :]
            ms = jnp.mean(x * x, axis=-1, keepdims=True)
            h_ref[r0:r0 + row_chunk, :] = ((x * lax.rsqrt(ms + RMS_EPS)) * g_ref[...]).astype(BF16)
            o_ref[r0:r0 + row_chunk, :] = x

    act = lax.dot_general(u_ref[...], h_ref[...], _NT, preferred_element_type=F32)
    for gi in range(groups):
        i = e * groups + gi
        route = jnp.zeros((n_keys, tm), F32)
        for h in range(n_heads):
            a0 = s0_ref[h, pl.ds(i, 1), :]
            w0 = e0_ref[h, pl.ds(i, 1), :]
            cand = s1_ref[h] + a0
            route = route + jnp.where(cand >= tau_ref[h:h + 1, :], e1_ref[h] * w0, 0.0)
        a = act[gi * n_keys:(gi + 1) * n_keys, :]
        gelu = 0.5 * a * (1.0 + lax.erf(a * np.float32(np.sqrt(0.5))))
        gate_ref[gi * n_keys:(gi + 1) * n_keys, :] = (gelu * route).astype(BF16)
    o_ref[...] += lax.dot_general(gate_ref[...], v_ref[...], _TN, preferred_element_type=F32)


def peer_main(x, g, u_bf16, v_bf16, s0, e0, s1, e1, tau, *, tm=512, te=512):
    n, d = x.shape
    n_exp = u_bf16.shape[0]
    n_heads, n_keys, _ = s0.shape
    tm = min(tm, n)
    assert n % tm == 0 and n_exp % te == 0 and te % n_keys == 0 and tm % LANES == 0
    once = pl.Buffered(1)
    big_spec = pl.BlockSpec((n_heads, n_keys, tm), lambda i, e: (0, 0, i), pipeline_mode=once)
    kern = functools.partial(_peer_main_kernel, n_heads=n_heads, n_keys=n_keys,
                             row_chunk=min(128, tm))
    return pl.pallas_call(
        kern,
        out_shape=jax.ShapeDtypeStruct((n, d), F32),
        grid=(n // tm, n_exp // te),
        in_specs=[
            pl.BlockSpec((tm, d), lambda i, e: (i, 0), pipeline_mode=once),
            pl.BlockSpec((1, d), lambda i, e: (0, 0)),
            pl.BlockSpec((te, d), lambda i, e: (e, 0)),
            pl.BlockSpec((te, d), lambda i, e: (e, 0)),
            big_spec, big_spec, big_spec, big_spec,
            pl.BlockSpec((n_heads, tm), lambda i, e: (0, i)),
        ],
        out_specs=pl.BlockSpec((tm, d), lambda i, e: (i, 0), pipeline_mode=once),
        scratch_shapes=[
            pltpu.VMEM((tm, d), BF16),
            pltpu.VMEM((te, tm), BF16),
        ],
        compiler_params=_params("parallel", "arbitrary"),
    )(x, g.reshape(1, d).astype(F32), u_bf16, v_bf16, s0, e0, s1, e1, tau)


def _layer(x, mem, norm_mix_g, w_in, moba_q_norm_g, moba_k_norm_g, moba_out_norm_g,
           sb_out_norm_g, w_out, norm_xattn_g, norm_mem_g, w_xq, w_xkv, xattn_q_norm_g,
           xattn_k_norm_g, w_xo, norm_ffn_g, w_peer_q, peer_sub_keys, peer_u, peer_v):
    b, t, d = x.shape
    n = b * t
    xf = x.reshape(n, d)
    mix_width = w_in.shape[1] // 3
    n_heads = mix_width // (2 * HEAD_DIM)
    grp = n_heads * HEAD_DIM
    slopes = jnp.asarray(2.0 ** (-8.0 * np.arange(1, n_heads + 1) / n_heads), dtype=F32)

    col_gain = jnp.concatenate([jnp.tile(moba_q_norm_g, n_heads), jnp.tile(moba_k_norm_g, n_heads),
                                jnp.ones((w_in.shape[1] - 2 * grp,), F32)])
    qkv = norm_matmul([xf], [norm_mix_g], w_in.astype(BF16), col_gain=col_gain,
                      n_norm_cols=2 * grp, head_major_bt=(b, t), tn=1024)
    o_moba = moba_attention(qkv, slopes, q_head0=0, k_head0=n_heads, v_head0=2 * n_heads,
                            n_heads=n_heads)
    o_sb = stick_breaking_attention(qkv, q_head0=3 * n_heads, k_head0=4 * n_heads,
                                    v_head0=5 * n_heads, n_heads=n_heads)
    x1 = norm_matmul([o_moba.reshape(n, grp), o_sb.reshape(n, grp)],
                     [moba_out_norm_g, sb_out_norm_g], w_out.astype(BF16), residual=xf)

    xw = w_xq.shape[1]
    n_mem = mem.shape[1]
    kv_gain = jnp.concatenate([jnp.tile(xattn_k_norm_g, N_XATTN_HEADS), jnp.ones((xw,), F32)])
    kv = norm_matmul([mem.reshape(b * n_mem, d)], [norm_mem_g], w_xkv.astype(BF16),
                     col_gain=kv_gain, n_norm_cols=xw)
    xq = norm_matmul([x1], [norm_xattn_g], w_xq.astype(BF16),
                     col_gain=jnp.tile(xattn_q_norm_g, N_XATTN_HEADS), n_norm_cols=xw)
    o_x = cross_attention_core(xq.reshape(b, t, xw), kv.reshape(b, n_mem, 2 * xw),
                               n_heads=N_XATTN_HEADS)
    x2 = norm_matmul([o_x.reshape(n, xw)], [None], w_xo.astype(BF16), residual=x1)

    pq = norm_matmul([x2], [norm_ffn_g], w_peer_q.astype(BF16))
    keys = peer_sub_keys.reshape(PEER_HEADS * 2, PEER_N_KEYS, -1).astype(BF16)
    s0, e0, s1, e1, tau = peer_select(pq, keys)
    x3 = peer_main(x2, norm_ffn_g, peer_u.astype(BF16), peer_v.astype(BF16), s0, e0, s1, e1, tau)
    return x3.reshape(b, t, d)


def kernel(x, mem, norm_mix_g, w_in, moba_q_norm_g, moba_k_norm_g, moba_out_norm_g, sb_out_norm_g,
           w_out, norm_xattn_g, norm_mem_g, w_xq, w_xkv, xattn_q_norm_g, xattn_k_norm_g, w_xo,
           norm_ffn_g, w_peer_q, peer_sub_keys, peer_u, peer_v):
    depth = w_in.shape[0]
    for l in range(depth):
        x = _layer(x, mem, norm_mix_g[l], w_in[l], moba_q_norm_g[l], moba_k_norm_g[l],
                   moba_out_norm_g[l], sb_out_norm_g[l], w_out[l], norm_xattn_g[l], norm_mem_g[l],
                   w_xq[l], w_xkv[l], xattn_q_norm_g[l], xattn_k_norm_g[l], w_xo[l], norm_ffn_g[l],
                   w_peer_q[l], peer_sub_keys[l], peer_u[l], peer_v[l])
    return x
```

```python
import functools

import numpy as np
import jax
import jax.numpy as jnp
from jax import lax
from jax.experimental import pallas as pl
from jax.experimental.pallas import tpu as pltpu

HEAD_DIM = 128
MOBA_BLOCK = 256
MOBA_TOPK = 3
N_XATTN_HEADS = 4
PEER_HEADS = 8
PEER_N_KEYS = 128
PEER_TOPK = 16
RMS_EPS = 1e-6
NEG_INF = -1e30
EXP_ZERO = -110.0
EXP2_ZERO = -160.0
LOG2E = 1.4426950408889634
NORM_MARGIN = 1.02

LANES = 128
VMEM_LIMIT = 56 * 1024 * 1024

F32 = jnp.float32
BF16 = jnp.bfloat16

_NT = (((1,), (1,)), ((), ()))
_TN = (((0,), (0,)), ((), ()))


def _params(*sem):
    return pltpu.CompilerParams(dimension_semantics=sem, vmem_limit_bytes=VMEM_LIMIT)


def _split_bf16(x):
    hi = x.astype(BF16)
    lo = (x - hi.astype(F32)).astype(BF16)
    return hi, lo


def _norm_matmul_kernel(*refs, k_sizes, has_gain, n_norm_tiles, n_col_tiles, has_res,
                        head_major, row_chunk):
    n_in = len(k_sizes)
    pos = 0
    x_refs = refs[pos:pos + n_in]; pos += n_in
    g_refs = []
    for hg in has_gain:
        if hg:
            g_refs.append(refs[pos]); pos += 1
        else:
            g_refs.append(None)
    w_ref = refs[pos]; pos += 1
    cg_ref = None
    if n_norm_tiles > 0:
        cg_ref = refs[pos]; pos += 1
    res_ref = None
    if has_res:
        res_ref = refs[pos]; pos += 1
    o_ref = refs[pos]; pos += 1
    h_ref = refs[pos]

    j = pl.program_id(1)
    tm = h_ref.shape[0]

    @pl.when(j == 0)
    def _prologue():
        off = 0
        for x_ref, g_ref, ksz in zip(x_refs, g_refs, k_sizes):
            for r0 in range(0, tm, row_chunk):
                x = x_ref[r0:r0 + row_chunk, :]
                if g_ref is not None:
                    ms = jnp.mean(x * x, axis=-1, keepdims=True)
                    x = (x * lax.rsqrt(ms + RMS_EPS)) * g_ref[...]
                h_ref[r0:r0 + row_chunk, off:off + ksz] = x.astype(BF16)
            off += ksz

    acc = jnp.dot(h_ref[...], w_ref[...], preferred_element_type=F32)
    tn = acc.shape[1]

    def finish(get_group):
        for hh in range(tn // LANES):
            y = get_group(hh)
            if res_ref is not None:
                y = y + res_ref[:, hh * LANES:(hh + 1) * LANES]
            if head_major:
                o_ref[0, hh] = y
            else:
                o_ref[:, hh * LANES:(hh + 1) * LANES] = y

    def plain_group(hh):
        return acc[:, hh * LANES:(hh + 1) * LANES]

    def normed_group(hh):
        a = acc[:, hh * LANES:(hh + 1) * LANES]
        ms = jnp.mean(a * a, axis=-1, keepdims=True)
        return (a * lax.rsqrt(ms + RMS_EPS)) * cg_ref[:, hh * LANES:(hh + 1) * LANES]

    if n_norm_tiles == 0:
        finish(plain_group)
    elif n_norm_tiles >= n_col_tiles:
        finish(normed_group)
    else:
        @pl.when(j < n_norm_tiles)
        def _():
            finish(normed_group)

        @pl.when(j >= n_norm_tiles)
        def _():
            finish(plain_group)


def norm_matmul(xs, gains, w, *, col_gain=None, n_norm_cols=0, residual=None,
                head_major_bt=None, tm=512, tn=512):
    m = xs[0].shape[0]
    k_sizes = tuple(int(x.shape[1]) for x in xs)
    k_total = sum(k_sizes)
    n_cols = w.shape[1]
    tm = min(tm, m)
    tn = min(tn, n_cols)
    assert m % tm == 0 and n_cols % tn == 0 and w.shape[0] == k_total
    assert n_norm_cols % tn == 0
    n_col_tiles = n_cols // tn
    n_norm_tiles = n_norm_cols // tn
    has_gain = tuple(g is not None for g in gains)

    args, in_specs = [], []
    for x, ksz in zip(xs, k_sizes):
        args.append(x)
        in_specs.append(pl.BlockSpec((tm, ksz), lambda i, j: (i, 0)))
    for g, ksz in zip(gains, k_sizes):
        if g is not None:
            args.append(g.reshape(1, ksz).astype(F32))
            in_specs.append(pl.BlockSpec((1, ksz), lambda i, j: (0, 0)))
    args.append(w)
    in_specs.append(pl.BlockSpec((k_total, tn), lambda i, j: (0, j)))
    if n_norm_tiles > 0:
        args.append(col_gain.reshape(1, n_cols).astype(F32))
        in_specs.append(pl.BlockSpec((1, tn), lambda i, j: (0, j)))
    if residual is not None:
        args.append(residual)
        in_specs.append(pl.BlockSpec((tm, tn), lambda i, j: (i, j)))

    if head_major_bt is not None:
        b, t = head_major_bt
        assert b * t == m and t % tm == 0
        tiles_per_b = t // tm
        out_shape = jax.ShapeDtypeStruct((b, n_cols // LANES, t, LANES), F32)
        out_spec = pl.BlockSpec((1, tn // LANES, tm, LANES),
                                lambda i, j: (i // tiles_per_b, j, i % tiles_per_b, 0))
    else:
        out_shape = jax.ShapeDtypeStruct((m, n_cols), F32)
        out_spec = pl.BlockSpec((tm, tn), lambda i, j: (i, j))

    kern = functools.partial(
        _norm_matmul_kernel, k_sizes=k_sizes, has_gain=has_gain, n_norm_tiles=n_norm_tiles,
        n_col_tiles=n_col_tiles, has_res=residual is not None,
        head_major=head_major_bt is not None, row_chunk=min(128, tm))
    return pl.pallas_call(
        kern,
        out_shape=out_shape,
        grid=(m // tm, n_col_tiles),
        in_specs=in_specs,
        out_specs=out_spec,
        scratch_shapes=[pltpu.VMEM((tm, k_total), BF16)],
        compiler_params=_params("parallel", "arbitrary"),
    )(*args)


def _moba_kernel(slopes_ref, q_ref, k_ref, v_ref, o_ref,
                 kb_ref, vt_ref, kmean_ref, sel_ref, m_ref, l_ref, acc_ref, kn2_ref, s_ref,
                 *, nb, blk, scale):
    h = pl.program_id(1)
    qi = pl.program_id(2)

    @pl.when(qi == 0)
    def _prep():
        kn2 = jnp.zeros((1, 1), F32)
        for n in range(nb):
            kblk = k_ref[0, 0, n * blk:(n + 1) * blk, :]
            kb_ref[n] = kblk.astype(BF16)
            kmean_ref[n:n + 1, :] = jnp.mean(kblk, axis=0, keepdims=True)
            vt_ref[n] = v_ref[0, 0, n * blk:(n + 1) * blk, :].T.astype(BF16)
            kn2 = jnp.maximum(kn2, jnp.max(jnp.sum(kblk * kblk, axis=1, keepdims=True),
                                           axis=0, keepdims=True))
        kn2_ref[...] = jnp.broadcast_to(kn2, kn2_ref.shape)

    q = q_ref[0, 0]
    qb = q.astype(BF16)
    slope = slopes_ref[h]

    kh, kl = _split_bf16(kmean_ref[...])
    qh, ql = _split_bf16(q)
    gate = (lax.dot_general(kh, qh, _NT, preferred_element_type=F32)
            + lax.dot_general(kh, ql, _NT, preferred_element_type=F32)
            + lax.dot_general(kl, qh, _NT, preferred_element_type=F32))
    blkid = lax.broadcasted_iota(jnp.int32, (nb, blk), 0)
    valid = blkid < qi
    gate = jnp.where(valid, gate, NEG_INF)
    terms = []
    for mth in range(nb):
        gm = gate[mth:mth + 1, :]
        terms.append(jnp.where(gm > gate, 1.0,
                               jnp.where(gm == gate, jnp.where(blkid > mth, 1.0, 0.0), 0.0)))
    while len(terms) > 1:
        terms = [terms[k] + terms[k + 1] for k in range(0, len(terms) - 1, 2)] + terms[len(terms) & ~1:]
    sel_ref[...] = jnp.where(valid, jnp.where(terms[0] < MOBA_TOPK, 1.0, 0.0), 0.0)

    t_idx = lax.broadcasted_iota(jnp.int32, (blk, blk), 1)
    j_idx = lax.broadcasted_iota(jnp.int32, (blk, blk), 0)
    dist_own = (t_idx - j_idx).astype(F32)

    scale2 = scale * LOG2E
    slope2 = slope * LOG2E

    s = lax.dot_general(kb_ref[qi], qb, _NT, preferred_element_type=F32) * scale2
    s = s - slope2 * dist_own
    s = jnp.where(dist_own >= 0, s, NEG_INF)
    m0 = jnp.max(s, axis=0, keepdims=True)
    p = jnp.exp2(s - m0)
    m_ref[...] = m0
    l_ref[...] = jnp.sum(p, axis=0, keepdims=True)
    acc_ref[...] = jnp.dot(vt_ref[qi], p.astype(BF16), preferred_element_type=F32)

    t2 = lax.broadcasted_iota(jnp.int32, (2 * blk, blk), 1)
    j2 = lax.broadcasted_iota(jnp.int32, (2 * blk, blk), 0)
    bias_pair = (t2 - j2).astype(F32) * (-slope2)

    def pair_scores(pair):
        n0 = 2 * pair
        sel0 = sel_ref[pl.ds(n0, 1), :]
        sel1 = sel_ref[pl.ds(n0 + 1, 1), :]
        kpair = jnp.concatenate([kb_ref[n0], kb_ref[n0 + 1]], axis=0)
        s = lax.dot_general(kpair, qb, _NT, preferred_element_type=F32) * scale2 + bias_pair
        return jnp.concatenate([jnp.where(sel0 > 0, s[:blk], NEG_INF),
                                jnp.where(sel1 > 0, s[blk:], NEG_INF)], axis=0)

    def pair_offset(pair):
        return slope2 * ((qi - 2 * pair) * blk).astype(F32)

    last_pair = nb // 2 - 1

    def body(pair, carry):
        n0 = 2 * pair
        s = s_ref[pair % 2]
        s_ref[(pair + 1) % 2] = pair_scores(jnp.minimum(pair + 1, last_pair))
        off = pair_offset(pair)
        m_old = m_ref[...]
        m_new = jnp.maximum(m_old, jnp.max(s, axis=0, keepdims=True) - off)
        alpha = jnp.exp2(m_old - m_new)
        p = jnp.exp2(s - (m_new + off))
        pb = p.astype(BF16)
        l_ref[...] = alpha * l_ref[...] + jnp.sum(p, axis=0, keepdims=True)
        acc_ref[...] = (alpha * acc_ref[...]
                        + jnp.dot(vt_ref[n0], pb[:blk], preferred_element_type=F32)
                        + jnp.dot(vt_ref[n0 + 1], pb[blk:], preferred_element_type=F32))
        m_ref[...] = m_new
        return carry

    qn2 = jnp.max(jnp.sum(q * q, axis=1, keepdims=True), axis=0, keepdims=True)
    bound = jnp.sqrt(qn2 * kn2_ref[:, 0:1]) * (scale2 * NORM_MARGIN)
    room = bound - jnp.min(m0, axis=1, keepdims=True) - EXP2_ZERO
    d = lax.broadcasted_iota(jnp.int32, (1, LANES), 1).astype(F32)
    n_live = jnp.sum(jnp.where(d * (slope2 * blk) + slope2 <= room, 1, 0))
    first_pair = jnp.maximum(qi - n_live, 0) // 2

    s_ref[first_pair % 2] = pair_scores(jnp.minimum(first_pair, last_pair))
    lax.fori_loop(first_pair, (qi + 1) // 2, body, 0)
    o_ref[0] = (acc_ref[...] / l_ref[...]).T


def moba_attention(qkv, slopes, *, q_head0, k_head0, v_head0, n_heads):
    b, _, t, dh = qkv.shape
    blk = MOBA_BLOCK
    assert t % (2 * blk) == 0
    nb = t // blk
    kern = functools.partial(_moba_kernel, nb=nb, blk=blk, scale=dh ** -0.5)
    return pl.pallas_call(
        kern,
        out_shape=jax.ShapeDtypeStruct((b, t, n_heads * dh), F32),
        grid=(b, n_heads, nb),
        in_specs=[
            pl.BlockSpec(memory_space=pltpu.SMEM),
            pl.BlockSpec((1, 1, blk, dh), lambda bi, h, qi: (bi, q_head0 + h, qi, 0)),
            pl.BlockSpec((1, 1, t, dh), lambda bi, h, qi: (bi, k_head0 + h, 0, 0)),
            pl.BlockSpec((1, 1, t, dh), lambda bi, h, qi: (bi, v_head0 + h, 0, 0)),
        ],
        out_specs=pl.BlockSpec((1, blk, dh), lambda bi, h, qi: (bi, qi, h)),
        scratch_shapes=[
            pltpu.VMEM((nb, blk, dh), BF16),
            pltpu.VMEM((nb, dh, blk), BF16),
            pltpu.VMEM((nb, dh), F32),
            pltpu.VMEM((nb, blk), F32),
            pltpu.VMEM((1, blk), F32),
            pltpu.VMEM((1, blk), F32),
            pltpu.VMEM((dh, blk), F32),
            pltpu.VMEM((1, LANES), F32),
            pltpu.VMEM((2, 2 * blk, blk), F32),
        ],
        compiler_params=_params("parallel", "parallel", "arbitrary"),
    )(slopes, qkv, qkv, qkv)


def _sb_kernel(q_ref, k_ref, v_ref, o_ref, kb_ref, vb_ref, *, nb, blk, q_tiles, scale):
    step = pl.program_id(2)

    @pl.when(step == 0)
    def _prep():
        for n in range(nb):
            kb_ref[n] = k_ref[0, 0, n * blk:(n + 1) * blk, :].astype(BF16)
            vb_ref[n] = v_ref[0, 0, n * blk:(n + 1) * blk, :].astype(BF16)

    row = lax.broadcasted_iota(jnp.int32, (blk, blk), 0)
    col = lax.broadcasted_iota(jnp.int32, (blk, blk), 1)
    after = jnp.where(row > col, 1.0, 0.0).astype(BF16)
    strict = col < row

    def log_sigmoids(z):
        t = jnp.log(1.0 + jnp.exp(-jnp.abs(z)))
        log_beta = jnp.minimum(z, 0.0) - t
        return log_beta, log_beta - z

    def sum_after(x):
        hi, lo = _split_bf16(x)
        return (jnp.dot(hi, after, preferred_element_type=F32)
                + jnp.dot(lo, after, preferred_element_type=F32))

    def row_sum(x):
        return jnp.sum(x, axis=1, keepdims=True)

    def pair_step(qb, n_late, late_is_own, acc, carry):
        has_early = n_late >= 1
        n_early = jnp.maximum(n_late - 1, 0)
        kpair = jnp.concatenate([kb_ref[n_early], kb_ref[n_late]], axis=0)
        z = lax.dot_general(qb, kpair, _NT, preferred_element_type=F32) * scale
        log_beta, log_om = log_sigmoids(z)
        om_early, om_late = log_om[:, :blk], log_om[:, blk:]
        if late_is_own:
            om_late = jnp.where(strict, om_late, 0.0)
            sum_late = row_sum(om_late)
            a_late = jnp.where(strict, jnp.exp(log_beta[:, blk:] + sum_after(om_late)), 0.0)
            left = sum_late
        else:
            sum_late = row_sum(om_late)
            a_late = jnp.exp(log_beta[:, blk:] + sum_after(om_late) + carry)
            left = carry + sum_late
        a_early = jnp.exp(log_beta[:, :blk] + sum_after(om_early) + left)
        v_early = vb_ref[n_early]
        v_early = jnp.where(has_early, v_early, jnp.zeros_like(v_early))
        contrib = (jnp.dot(a_late.astype(BF16), vb_ref[n_late], preferred_element_type=F32)
                   + jnp.dot(a_early.astype(BF16), v_early, preferred_element_type=F32))
        acc = contrib if acc is None else acc + contrib
        return acc, left + row_sum(om_early)

    tiles = [q_tiles * step + k for k in range(q_tiles)]
    qbs = [q_ref[0, 0, k * blk:(k + 1) * blk, :].astype(BF16) for k in range(q_tiles)]
    firsts = [pair_step(qb, qi, True, None, None) for qb, qi in zip(qbs, tiles)]

    for k, (qb, qi, (acc0, carry0)) in enumerate(zip(qbs, tiles, firsts)):
        n_pairs = qi // 2

        def cond(state, n_pairs=n_pairs):
            pair, _, _, carry_max = state
            return jnp.logical_and(pair < n_pairs, carry_max > EXP_ZERO)

        def body(state, qb=qb, qi=qi):
            pair, acc, carry, _ = state
            acc, carry = pair_step(qb, qi - 2 - 2 * pair, False, acc, carry)
            return pair + 1, acc, carry, jnp.max(carry)

        _, acc, _, _ = lax.while_loop(cond, body, (jnp.int32(0), acc0, carry0, jnp.max(carry0)))
        o_ref[0, k * blk:(k + 1) * blk, :] = acc


def stick_breaking_attention(qkv, *, q_head0, k_head0, v_head0, n_heads, blk=256, q_tiles=2):
    b, _, t, dh = qkv.shape
    assert t % (blk * q_tiles) == 0
    nb = t // blk
    tq = blk * q_tiles
    kern = functools.partial(_sb_kernel, nb=nb, blk=blk, q_tiles=q_tiles, scale=dh ** -0.5)
    return pl.pallas_call(
        kern,
        out_shape=jax.ShapeDtypeStruct((b, t, n_heads * dh), F32),
        grid=(b, n_heads, t // tq),
        in_specs=[
            pl.BlockSpec((1, 1, tq, dh), lambda bi, h, qi: (bi, q_head0 + h, qi, 0)),
            pl.BlockSpec((1, 1, t, dh), lambda bi, h, qi: (bi, k_head0 + h, 0, 0)),
            pl.BlockSpec((1, 1, t, dh), lambda bi, h, qi: (bi, v_head0 + h, 0, 0)),
        ],
        out_specs=pl.BlockSpec((1, tq, dh), lambda bi, h, qi: (bi, qi, h)),
        scratch_shapes=[
            pltpu.VMEM((nb, blk, dh), BF16),
            pltpu.VMEM((nb, blk, dh), BF16),
        ],
        compiler_params=_params("parallel", "parallel", "arbitrary"),
    )(qkv, qkv, qkv)


def _xattn_kernel(q_ref, kv_ref, o_ref, *, n_heads, dh, scale):
    width = n_heads * dh
    for hh in range(n_heads):
        qh = q_ref[0, :, hh * dh:(hh + 1) * dh].astype(BF16)
        kh = kv_ref[0, :, hh * dh:(hh + 1) * dh].astype(BF16)
        vh = kv_ref[0, :, width + hh * dh:width + (hh + 1) * dh].astype(BF16)
        s = lax.dot_general(qh, kh, _NT, preferred_element_type=F32) * scale
        e = jnp.exp(s - jnp.max(s, axis=-1, keepdims=True))
        p = e / jnp.sum(e, axis=-1, keepdims=True)
        o_ref[0, :, hh * dh:(hh + 1) * dh] = jnp.dot(p.astype(BF16), vh, preferred_element_type=F32)


def cross_attention_core(q, kv, *, n_heads, tq=512):
    b, t, width = q.shape
    mlen = kv.shape[1]
    dh = width // n_heads
    tq = min(tq, t)
    kern = functools.partial(_xattn_kernel, n_heads=n_heads, dh=dh, scale=dh ** -0.5)
    return pl.pallas_call(
        kern,
        out_shape=jax.ShapeDtypeStruct((b, t, width), F32),
        grid=(b, t // tq),
        in_specs=[
            pl.BlockSpec((1, tq, width), lambda bi, i: (bi, i, 0)),
            pl.BlockSpec((1, mlen, 2 * width), lambda bi, i: (bi, 0, 0)),
        ],
        out_specs=pl.BlockSpec((1, tq, width), lambda bi, i: (bi, i, 0)),
        compiler_params=_params("parallel", "arbitrary"),
    )(q, kv)


def _extract_topk(work, n_rounds):
    rows = lax.broadcasted_iota(jnp.int32, work.shape, 0).astype(F32)
    n_rows = float(work.shape[0])
    vals = []
    for _ in range(n_rounds):
        mx = jnp.max(work, axis=0, keepdims=True)
        first = jnp.min(jnp.where(work == mx, rows, n_rows), axis=0, keepdims=True)
        work = jnp.where(rows == first, -jnp.inf, work)
        vals.append(mx)
    return vals, work


def _staircase_candidates(vals0, vals1, topk):
    sub = 8
    val1_all = jnp.concatenate(vals1, axis=0)
    val1_head = val1_all[:sub]
    row = lax.broadcasted_iota(jnp.int32, val1_head.shape, 0)
    slabs = [val1_all + vals0[0]]
    for a in range(1, sub):
        b_max = topk // (a + 1) - 1
        slab = val1_head + vals0[a]
        slabs.append(slab if b_max >= sub - 1 else jnp.where(row <= b_max, slab, -jnp.inf))
    slabs.append(jnp.concatenate(vals0[sub:], axis=0) + vals1[0])
    return jnp.concatenate(slabs, axis=0)


def _peer_select_kernel(q_ref, keys_ref, s0_ref, e0_ref, s1_ref, e1_ref, tau_ref, *, topk):
    scores, vals, left = [], [], []
    for p in range(2):
        qs = q_ref[:, p * LANES:(p + 1) * LANES].astype(BF16)
        s = lax.dot_general(keys_ref[p], qs, _NT, preferred_element_type=F32)
        v, rest = _extract_topk(s, topk)
        scores.append(s); vals.append(v); left.append(rest)
    tops, _ = _extract_topk(_staircase_candidates(vals[0], vals[1], topk), topk)
    z = jnp.zeros_like(tops[0])
    for tv in tops:
        z = z + jnp.exp(tv - tops[0])
    s0_ref[0] = scores[0]
    s1_ref[0] = scores[1]
    e0_ref[0] = jnp.where(left[0] == -jnp.inf, jnp.exp(scores[0] - vals[0][0]), 0.0) / z
    e1_ref[0] = jnp.where(left[1] == -jnp.inf, jnp.exp(scores[1] - vals[1][0]), 0.0)
    tau_ref[0] = tops[topk - 1]


def peer_select(q, keys_bf16, *, tm=256):
    n = q.shape[0]
    n_heads = keys_bf16.shape[0] // 2
    n_keys = keys_bf16.shape[1]
    tm = min(tm, n)
    big = jax.ShapeDtypeStruct((n_heads, n_keys, n), F32)
    big_spec = pl.BlockSpec((1, n_keys, tm), lambda i, h: (h, 0, i))
    kern = functools.partial(_peer_select_kernel, topk=PEER_TOPK)
    s0, e0, s1, e1, tau = pl.pallas_call(
        kern,
        out_shape=(big, big, big, big, jax.ShapeDtypeStruct((n_heads, 1, n), F32)),
        grid=(n // tm, n_heads),
        in_specs=[
            pl.BlockSpec((tm, 2 * LANES), lambda i, h: (i, h)),
            pl.BlockSpec((2, n_keys, keys_bf16.shape[2]), lambda i, h: (h, 0, 0)),
        ],
        out_specs=(big_spec, big_spec, big_spec, big_spec,
                   pl.BlockSpec((1, 1, tm), lambda i, h: (h, 0, i))),
        compiler_params=_params("parallel", "arbitrary"),
    )(q, keys_bf16)
    return s0, e0, s1, e1, tau.reshape(n_heads, n)


def _peer_main_kernel(x_ref, g_ref, u_ref, v_ref, s0_ref, e0_ref, s1_ref, e1_ref, tau_ref, o_ref,
                      h_ref, gate_ref, *, n_heads, n_keys, row_chunk):
    e = pl.program_id(1)
    tm = h_ref.shape[0]
    te = u_ref.shape[0]
    groups = te // n_keys

    @pl.when(e == 0)
    def _prologue():
        for r0 in range(0, tm, row_chunk):
            x = x_ref[r0:r0 + row_chunk, :]
            ms = jnp.mean(x * x, axis=-1, keepdims=True)
            h_ref[r0:r0 + row_chunk, :] = ((x * lax.rsqrt(ms + RMS_EPS)) * g_ref[...]).astype(BF16)
            o_ref[r0:r0 + row_chunk, :] = x

    act = lax.dot_general(u_ref[...], h_ref[...], _NT, preferred_element_type=F32)
    for gi in range(groups):
        i = e * groups + gi
        route = jnp.zeros((n_keys, tm), F32)
        for h in range(n_heads):
            a0 = s0_ref[h, pl.ds(i, 1), :]
            w0 = e0_ref[h, pl.ds(i, 1), :]
            cand = s1_ref[h] + a0
            route = route + jnp.where(cand >= tau_ref[h:h + 1, :], e1_ref[h] * w0, 0.0)
        a = act[gi * n_keys:(gi + 1) * n_keys, :]
        gelu = 0.5 * a * (1.0 + lax.erf(a * np.float32(np.sqrt(0.5))))
        gate_ref[gi * n_keys:(gi + 1) * n_keys, :] = (gelu * route).astype(BF16)
    o_ref[...] += lax.dot_general(gate_ref[...], v_ref[...], _TN, preferred_element_type=F32)


def peer_main(x, g, u_bf16, v_bf16, s0, e0, s1, e1, tau, *, tm=512, te=512):
    n, d = x.shape
    n_exp = u_bf16.shape[0]
    n_heads, n_keys, _ = s0.shape
    tm = min(tm, n)
    assert n % tm == 0 and n_exp % te == 0 and te % n_keys == 0 and tm % LANES == 0
    once = pl.Buffered(1)
    big_spec = pl.BlockSpec((n_heads, n_keys, tm), lambda i, e: (0, 0, i), pipeline_mode=once)
    kern = functools.partial(_peer_main_kernel, n_heads=n_heads, n_keys=n_keys,
                             row_chunk=min(128, tm))
    return pl.pallas_call(
        kern,
        out_shape=jax.ShapeDtypeStruct((n, d), F32),
        grid=(n // tm, n_exp // te),
        in_specs=[
            pl.BlockSpec((tm, d), lambda i, e: (i, 0), pipeline_mode=once),
            pl.BlockSpec((1, d), lambda i, e: (0, 0)),
            pl.BlockSpec((te, d), lambda i, e: (e, 0)),
            pl.BlockSpec((te, d), lambda i, e: (e, 0)),
            big_spec, big_spec, big_spec, big_spec,
            pl.BlockSpec((n_heads, tm), lambda i, e: (0, i)),
        ],
        out_specs=pl.BlockSpec((tm, d), lambda i, e: (i, 0), pipeline_mode=once),
        scratch_shapes=[
            pltpu.VMEM((tm, d), BF16),
            pltpu.VMEM((te, tm), BF16),
        ],
        compiler_params=_params("parallel", "arbitrary"),
    )(x, g.reshape(1, d).astype(F32), u_bf16, v_bf16, s0, e0, s1, e1, tau)


def _layer(x, mem, norm_mix_g, w_in, moba_q_norm_g, moba_k_norm_g, moba_out_norm_g,
           sb_out_norm_g, w_out, norm_xattn_g, norm_mem_g, w_xq, w_xkv, xattn_q_norm_g,
           xattn_k_norm_g, w_xo, norm_ffn_g, w_peer_q, peer_sub_keys, peer_u, peer_v):
    b, t, d = x.shape
    n = b * t
    xf = x.reshape(n, d)
    mix_width = w_in.shape[1] // 3
    n_heads = mix_width // (2 * HEAD_DIM)
    grp = n_heads * HEAD_DIM
    slopes = jnp.asarray(2.0 ** (-8.0 * np.arange(1, n_heads + 1) / n_heads), dtype=F32)

    col_gain = jnp.concatenate([jnp.tile(moba_q_norm_g, n_heads), jnp.tile(moba_k_norm_g, n_heads),
                                jnp.ones((w_in.shape[1] - 2 * grp,), F32)])
    qkv = norm_matmul([xf], [norm_mix_g], w_in.astype(BF16), col_gain=col_gain,
                      n_norm_cols=2 * grp, head_major_bt=(b, t), tn=1024)
    o_moba = moba_attention(qkv, slopes, q_head0=0, k_head0=n_heads, v_head0=2 * n_heads,
                            n_heads=n_heads)
    o_sb = stick_breaking_attention(qkv, q_head0=3 * n_heads, k_head0=4 * n_heads,
                                    v_head0=5 * n_heads, n_heads=n_heads)
    x1 = norm_matmul([o_moba.reshape(n, grp), o_sb.reshape(n, grp)],
                     [moba_out_norm_g, sb_out_norm_g], w_out.astype(BF16), residual=xf, tn=1024)

    xw = w_xq.shape[1]
    n_mem = mem.shape[1]
    kv_gain = jnp.concatenate([jnp.tile(xattn_k_norm_g, N_XATTN_HEADS), jnp.ones((xw,), F32)])
    kv = norm_matmul([mem.reshape(b * n_mem, d)], [norm_mem_g], w_xkv.astype(BF16),
                     col_gain=kv_gain, n_norm_cols=xw)
    xq = norm_matmul([x1], [norm_xattn_g], w_xq.astype(BF16),
                     col_gain=jnp.tile(xattn_q_norm_g, N_XATTN_HEADS), n_norm_cols=xw)
    o_x = cross_attention_core(xq.reshape(b, t, xw), kv.reshape(b, n_mem, 2 * xw),
                               n_heads=N_XATTN_HEADS)
    x2 = norm_matmul([o_x.reshape(n, xw)], [None], w_xo.astype(BF16), residual=x1, tn=2048)

    pq = norm_matmul([x2], [norm_ffn_g], w_peer_q.astype(BF16), tn=1024)
    keys = peer_sub_keys.reshape(PEER_HEADS * 2, PEER_N_KEYS, -1).astype(BF16)
    s0, e0, s1, e1, tau = peer_select(pq, keys)
    x3 = peer_main(x2, norm_ffn_g, peer_u.astype(BF16), peer_v.astype(BF16), s0, e0, s1, e1, tau)
    return x3.reshape(b, t, d)


def kernel(x, mem, norm_mix_g, w_in, moba_q_norm_g, moba_k_norm_g, moba_out_norm_g, sb_out_norm_g,
           w_out, norm_xattn_g, norm_mem_g, w_xq, w_xkv, xattn_q_norm_g, xattn_k_norm_g, w_xo,
           norm_ffn_g, w_peer_q, peer_sub_keys, peer_u, peer_v):
    depth = w_in.shape[0]
    for l in range(depth):
        x = _layer(x, mem, norm_mix_g[l], w_in[l], moba_q_norm_g[l], moba_k_norm_g[l],
                   moba_out_norm_g[l], sb_out_norm_g[l], w_out[l], norm_xattn_g[l], norm_mem_g[l],
                   w_xq[l], w_xkv[l], xattn_q_norm_g[l], xattn_k_norm_g[l], w_xo[l], norm_ffn_g[l],
                   w_peer_q[l], peer_sub_keys[l], peer_u[l], peer_v[l])
    return x
```

```python
import functools

import numpy as np
import jax
import jax.numpy as jnp
from jax import lax
from jax.experimental import pallas as pl
from jax.experimental.pallas import tpu as pltpu

HEAD_DIM = 128
MOBA_BLOCK = 256
MOBA_TOPK = 3
N_XATTN_HEADS = 4
PEER_HEADS = 8
PEER_N_KEYS = 128
PEER_TOPK = 16
RMS_EPS = 1e-6
NEG_INF = -1e30
EXP_ZERO = -110.0
EXP2_ZERO = -160.0
LOG2E = 1.4426950408889634
NORM_MARGIN = 1.02

LANES = 128
VMEM_LIMIT = 56 * 1024 * 1024

F32 = jnp.float32
BF16 = jnp.bfloat16

_NT = (((1,), (1,)), ((), ()))
_TN = (((0,), (0,)), ((), ()))


def _params(*sem):
    return pltpu.CompilerParams(dimension_semantics=sem, vmem_limit_bytes=VMEM_LIMIT)


def _split_bf16(x):
    hi = x.astype(BF16)
    lo = (x - hi.astype(F32)).astype(BF16)
    return hi, lo


def _norm_matmul_kernel(*refs, k_sizes, has_gain, n_norm_tiles, n_col_tiles, has_res,
                        head_major, row_chunk):
    n_in = len(k_sizes)
    pos = 0
    x_refs = refs[pos:pos + n_in]; pos += n_in
    g_refs = []
    for hg in has_gain:
        if hg:
            g_refs.append(refs[pos]); pos += 1
        else:
            g_refs.append(None)
    w_ref = refs[pos]; pos += 1
    cg_ref = None
    if n_norm_tiles > 0:
        cg_ref = refs[pos]; pos += 1
    res_ref = None
    if has_res:
        res_ref = refs[pos]; pos += 1
    o_ref = refs[pos]; pos += 1
    h_ref = refs[pos]

    j = pl.program_id(1)
    tm = h_ref.shape[0]

    @pl.when(j == 0)
    def _prologue():
        off = 0
        for x_ref, g_ref, ksz in zip(x_refs, g_refs, k_sizes):
            for r0 in range(0, tm, row_chunk):
                x = x_ref[r0:r0 + row_chunk, :]
                if g_ref is not None:
                    ms = jnp.mean(x * x, axis=-1, keepdims=True)
                    x = (x * lax.rsqrt(ms + RMS_EPS)) * g_ref[...]
                h_ref[r0:r0 + row_chunk, off:off + ksz] = x.astype(BF16)
            off += ksz

    acc = jnp.dot(h_ref[...], w_ref[...], preferred_element_type=F32)
    tn = acc.shape[1]

    def finish(get_group):
        for hh in range(tn // LANES):
            y = get_group(hh)
            if res_ref is not None:
                y = y + res_ref[:, hh * LANES:(hh + 1) * LANES]
            if head_major:
                o_ref[0, hh] = y
            else:
                o_ref[:, hh * LANES:(hh + 1) * LANES] = y

    def plain_group(hh):
        return acc[:, hh * LANES:(hh + 1) * LANES]

    def normed_group(hh):
        a = acc[:, hh * LANES:(hh + 1) * LANES]
        ms = jnp.mean(a * a, axis=-1, keepdims=True)
        return (a * lax.rsqrt(ms + RMS_EPS)) * cg_ref[:, hh * LANES:(hh + 1) * LANES]

    if n_norm_tiles == 0:
        finish(plain_group)
    elif n_norm_tiles >= n_col_tiles:
        finish(normed_group)
    else:
        @pl.when(j < n_norm_tiles)
        def _():
            finish(normed_group)

        @pl.when(j >= n_norm_tiles)
        def _():
            finish(plain_group)


def norm_matmul(xs, gains, w, *, col_gain=None, n_norm_cols=0, residual=None,
                head_major_bt=None, tm=512, tn=512):
    m = xs[0].shape[0]
    k_sizes = tuple(int(x.shape[1]) for x in xs)
    k_total = sum(k_sizes)
    n_cols = w.shape[1]
    tm = min(tm, m)
    tn = min(tn, n_cols)
    assert m % tm == 0 and n_cols % tn == 0 and w.shape[0] == k_total
    assert n_norm_cols % tn == 0
    n_col_tiles = n_cols // tn
    n_norm_tiles = n_norm_cols // tn
    has_gain = tuple(g is not None for g in gains)

    args, in_specs = [], []
    for x, ksz in zip(xs, k_sizes):
        args.append(x)
        in_specs.append(pl.BlockSpec((tm, ksz), lambda i, j: (i, 0)))
    for g, ksz in zip(gains, k_sizes):
        if g is not None:
            args.append(g.reshape(1, ksz).astype(F32))
            in_specs.append(pl.BlockSpec((1, ksz), lambda i, j: (0, 0)))
    args.append(w)
    in_specs.append(pl.BlockSpec((k_total, tn), lambda i, j: (0, j)))
    if n_norm_tiles > 0:
        args.append(col_gain.reshape(1, n_cols).astype(F32))
        in_specs.append(pl.BlockSpec((1, tn), lambda i, j: (0, j)))
    if residual is not None:
        args.append(residual)
        in_specs.append(pl.BlockSpec((tm, tn), lambda i, j: (i, j)))

    if head_major_bt is not None:
        b, t = head_major_bt
        assert b * t == m and t % tm == 0
        tiles_per_b = t // tm
        out_shape = jax.ShapeDtypeStruct((b, n_cols // LANES, t, LANES), F32)
        out_spec = pl.BlockSpec((1, tn // LANES, tm, LANES),
                                lambda i, j: (i // tiles_per_b, j, i % tiles_per_b, 0))
    else:
        out_shape = jax.ShapeDtypeStruct((m, n_cols), F32)
        out_spec = pl.BlockSpec((tm, tn), lambda i, j: (i, j))

    kern = functools.partial(
        _norm_matmul_kernel, k_sizes=k_sizes, has_gain=has_gain, n_norm_tiles=n_norm_tiles,
        n_col_tiles=n_col_tiles, has_res=residual is not None,
        head_major=head_major_bt is not None, row_chunk=min(128, tm))
    return pl.pallas_call(
        kern,
        out_shape=out_shape,
        grid=(m // tm, n_col_tiles),
        in_specs=in_specs,
        out_specs=out_spec,
        scratch_shapes=[pltpu.VMEM((tm, k_total), BF16)],
        compiler_params=_params("parallel", "arbitrary"),
    )(*args)


def _select_blocks(kmean, q_tile, tile, nb):
    blk = q_tile.shape[0]
    if tile == 0:
        return jnp.zeros((nb, blk), F32)
    kh, kl = _split_bf16(kmean)
    qh, ql = _split_bf16(q_tile)
    gate = (lax.dot_general(kh, qh, _NT, preferred_element_type=F32)
            + lax.dot_general(kh, ql, _NT, preferred_element_type=F32)
            + lax.dot_general(kl, qh, _NT, preferred_element_type=F32))
    blkid = lax.broadcasted_iota(jnp.int32, (nb, blk), 0)
    valid = blkid < tile
    gate = jnp.where(valid, gate, NEG_INF)
    terms = []
    for mth in range(tile):
        gm = gate[mth:mth + 1, :]
        terms.append(jnp.where(gm > gate, 1.0,
                               jnp.where(gm == gate, jnp.where(blkid > mth, 1.0, 0.0), 0.0)))
    while len(terms) > 1:
        terms = [terms[k] + terms[k + 1] for k in range(0, len(terms) - 1, 2)] + terms[len(terms) & ~1:]
    return jnp.where(valid, jnp.where(terms[0] < MOBA_TOPK, 1.0, 0.0), 0.0)


def _moba_kernel(slopes_ref, first_ref, q_ref, qall_ref, k_ref, v_ref, o_ref,
                 kb_ref, vt_ref, kmean_ref, sel_all_ref, m_ref, l_ref, acc_ref, s_ref,
                 *, nb, blk, scale):
    h = pl.program_id(1)
    qi = pl.program_id(2)

    scale2 = scale * LOG2E
    slope2 = slopes_ref[h] * LOG2E

    @pl.when(qi == 0)
    def _prep():
        for n in range(nb):
            kblk = k_ref[0, 0, n * blk:(n + 1) * blk, :]
            kb_ref[n] = kblk.astype(BF16)
            kmean_ref[n:n + 1, :] = jnp.mean(kblk, axis=0, keepdims=True)
            vt_ref[n] = v_ref[0, 0, n * blk:(n + 1) * blk, :].T.astype(BF16)
        kmean = kmean_ref[...]
        for tile in range(nb):
            sel_all_ref[tile] = _select_blocks(
                kmean, qall_ref[0, 0, tile * blk:(tile + 1) * blk, :], tile, nb)

    q = q_ref[0, 0]
    qb = q.astype(BF16)
    sel_ref = sel_all_ref.at[qi]

    t2 = lax.broadcasted_iota(jnp.int32, (2 * blk, blk), 1)
    j2 = lax.broadcasted_iota(jnp.int32, (2 * blk, blk), 0)
    bias_pair = (t2 - j2).astype(F32) * (-slope2)

    def pair_scores(pair):
        n0 = 2 * pair
        sel0 = sel_ref[pl.ds(n0, 1), :]
        sel1 = sel_ref[pl.ds(n0 + 1, 1), :]
        kpair = jnp.concatenate([kb_ref[n0], kb_ref[n0 + 1]], axis=0)
        s = lax.dot_general(kpair, qb, _NT, preferred_element_type=F32) * scale2 + bias_pair
        return jnp.concatenate([jnp.where(sel0 > 0, s[:blk], NEG_INF),
                                jnp.where(sel1 > 0, s[blk:], NEG_INF)], axis=0)

    def pair_offset(pair):
        return slope2 * ((qi - 2 * pair) * blk).astype(F32)

    last_pair = nb // 2 - 1

    def body(pair, carry):
        n0 = 2 * pair
        s = s_ref[pair % 2]
        s_ref[(pair + 1) % 2] = pair_scores(jnp.minimum(pair + 1, last_pair))
        off = pair_offset(pair)
        m_old = m_ref[...]
        m_new = jnp.maximum(m_old, jnp.max(s, axis=0, keepdims=True) - off)
        alpha = jnp.exp2(m_old - m_new)
        p = jnp.exp2(s - (m_new + off))
        pb = p.astype(BF16)
        l_ref[...] = alpha * l_ref[...] + jnp.sum(p, axis=0, keepdims=True)
        acc_ref[...] = (alpha * acc_ref[...]
                        + jnp.dot(vt_ref[n0], pb[:blk], preferred_element_type=F32)
                        + jnp.dot(vt_ref[n0 + 1], pb[blk:], preferred_element_type=F32))
        m_ref[...] = m_new
        return carry

    first_pair = first_ref[h, qi]
    s_ref[first_pair % 2] = pair_scores(jnp.minimum(first_pair, last_pair))

    t_idx = lax.broadcasted_iota(jnp.int32, (blk, blk), 1)
    j_idx = lax.broadcasted_iota(jnp.int32, (blk, blk), 0)
    dist_own = (t_idx - j_idx).astype(F32)
    s = lax.dot_general(kb_ref[qi], qb, _NT, preferred_element_type=F32) * scale2
    s = s - slope2 * dist_own
    s = jnp.where(dist_own >= 0, s, NEG_INF)
    m0 = jnp.max(s, axis=0, keepdims=True)
    p = jnp.exp2(s - m0)
    m_ref[...] = m0
    l_ref[...] = jnp.sum(p, axis=0, keepdims=True)
    acc_ref[...] = jnp.dot(vt_ref[qi], p.astype(BF16), preferred_element_type=F32)

    lax.fori_loop(first_pair, (qi + 1) // 2, body, 0)
    o_ref[0] = (acc_ref[...] / l_ref[...]).T


def _moba_first_pairs(slopes, qk_norm_bound, scale, nb, blk):
    slope2 = slopes * LOG2E
    room = 2.0 * (qk_norm_bound * NORM_MARGIN * scale * LOG2E) - EXP2_ZERO
    d = jnp.arange(nb, dtype=F32)
    n_live = jnp.sum(d[None, :] * (slope2[:, None] * blk) + slope2[:, None] <= room, axis=1)
    tiles = jnp.arange(nb, dtype=jnp.int32)
    return jnp.maximum(tiles[None, :] - n_live[:, None].astype(jnp.int32), 0) // 2


def moba_attention(qkv, slopes, qk_norm_bound, *, q_head0, k_head0, v_head0, n_heads):
    b, _, t, dh = qkv.shape
    blk = MOBA_BLOCK
    assert t % (2 * blk) == 0
    nb = t // blk
    scale = dh ** -0.5
    first_pairs = _moba_first_pairs(slopes, qk_norm_bound, scale, nb, blk)
    kern = functools.partial(_moba_kernel, nb=nb, blk=blk, scale=scale)
    return pl.pallas_call(
        kern,
        out_shape=jax.ShapeDtypeStruct((b, t, n_heads * dh), F32),
        grid=(b, n_heads, nb),
        in_specs=[
            pl.BlockSpec(memory_space=pltpu.SMEM),
            pl.BlockSpec(memory_space=pltpu.SMEM),
            pl.BlockSpec((1, 1, blk, dh), lambda bi, h, qi: (bi, q_head0 + h, qi, 0)),
            pl.BlockSpec((1, 1, t, dh), lambda bi, h, qi: (bi, q_head0 + h, 0, 0)),
            pl.BlockSpec((1, 1, t, dh), lambda bi, h, qi: (bi, k_head0 + h, 0, 0)),
            pl.BlockSpec((1, 1, t, dh), lambda bi, h, qi: (bi, v_head0 + h, 0, 0)),
        ],
        out_specs=pl.BlockSpec((1, blk, dh), lambda bi, h, qi: (bi, qi, h)),
        scratch_shapes=[
            pltpu.VMEM((nb, blk, dh), BF16),
            pltpu.VMEM((nb, dh, blk), BF16),
            pltpu.VMEM((nb, dh), F32),
            pltpu.VMEM((nb, nb, blk), F32),
            pltpu.VMEM((1, blk), F32),
            pltpu.VMEM((1, blk), F32),
            pltpu.VMEM((dh, blk), F32),
            pltpu.VMEM((2, 2 * blk, blk), F32),
        ],
        compiler_params=_params("parallel", "parallel", "arbitrary"),
    )(slopes, first_pairs, qkv, qkv, qkv, qkv)


def _sb_kernel(q_ref, k_ref, v_ref, o_ref, kb_ref, vb_ref, *, nb, blk, q_tiles, scale):
    step = pl.program_id(2)

    @pl.when(step == 0)
    def _prep():
        for n in range(nb):
            kb_ref[n] = k_ref[0, 0, n * blk:(n + 1) * blk, :].astype(BF16)
            vb_ref[n] = v_ref[0, 0, n * blk:(n + 1) * blk, :].astype(BF16)

    row = lax.broadcasted_iota(jnp.int32, (blk, blk), 0)
    col = lax.broadcasted_iota(jnp.int32, (blk, blk), 1)
    after = jnp.where(row > col, 1.0, 0.0).astype(BF16)
    strict = col < row

    def log_sigmoids(z):
        t = jnp.log(1.0 + jnp.exp(-jnp.abs(z)))
        log_beta = jnp.minimum(z, 0.0) - t
        return log_beta, log_beta - z

    def sum_after(x):
        hi, lo = _split_bf16(x)
        return (jnp.dot(hi, after, preferred_element_type=F32)
                + jnp.dot(lo, after, preferred_element_type=F32))

    def row_sum(x):
        return jnp.sum(x, axis=1, keepdims=True)

    def pair_step(qb, n_late, late_is_own, acc, carry):
        has_early = n_late >= 1
        n_early = jnp.maximum(n_late - 1, 0)
        kpair = jnp.concatenate([kb_ref[n_early], kb_ref[n_late]], axis=0)
        z = lax.dot_general(qb, kpair, _NT, preferred_element_type=F32) * scale
        log_beta, log_om = log_sigmoids(z)
        om_early, om_late = log_om[:, :blk], log_om[:, blk:]
        if late_is_own:
            om_late = jnp.where(strict, om_late, 0.0)
            sum_late = row_sum(om_late)
            a_late = jnp.where(strict, jnp.exp(log_beta[:, blk:] + sum_after(om_late)), 0.0)
            left = sum_late
        else:
            sum_late = row_sum(om_late)
            a_late = jnp.exp(log_beta[:, blk:] + sum_after(om_late) + carry)
            left = carry + sum_late
        a_early = jnp.exp(log_beta[:, :blk] + sum_after(om_early) + left)
        v_early = vb_ref[n_early]
        v_early = jnp.where(has_early, v_early, jnp.zeros_like(v_early))
        contrib = (jnp.dot(a_late.astype(BF16), vb_ref[n_late], preferred_element_type=F32)
                   + jnp.dot(a_early.astype(BF16), v_early, preferred_element_type=F32))
        acc = contrib if acc is None else acc + contrib
        return acc, left + row_sum(om_early)

    tiles = [q_tiles * step + k for k in range(q_tiles)]
    qbs = [q_ref[0, 0, k * blk:(k + 1) * blk, :].astype(BF16) for k in range(q_tiles)]
    firsts = [pair_step(qb, qi, True, None, None) for qb, qi in zip(qbs, tiles)]

    for k, (qb, qi, (acc0, carry0)) in enumerate(zip(qbs, tiles, firsts)):
        n_pairs = qi // 2

        def cond(state, n_pairs=n_pairs):
            pair, _, _, carry_max = state
            return jnp.logical_and(pair < n_pairs, carry_max > EXP_ZERO)

        def body(state, qb=qb, qi=qi):
            pair, acc, carry, _ = state
            acc, carry = pair_step(qb, qi - 2 - 2 * pair, False, acc, carry)
            return pair + 1, acc, carry, jnp.max(carry)

        _, acc, _, _ = lax.while_loop(cond, body, (jnp.int32(0), acc0, carry0, jnp.max(carry0)))
        o_ref[0, k * blk:(k + 1) * blk, :] = acc


def stick_breaking_attention(qkv, *, q_head0, k_head0, v_head0, n_heads, blk=256, q_tiles=2):
    b, _, t, dh = qkv.shape
    assert t % (blk * q_tiles) == 0
    nb = t // blk
    tq = blk * q_tiles
    kern = functools.partial(_sb_kernel, nb=nb, blk=blk, q_tiles=q_tiles, scale=dh ** -0.5)
    return pl.pallas_call(
        kern,
        out_shape=jax.ShapeDtypeStruct((b, t, n_heads * dh), F32),
        grid=(b, n_heads, t // tq),
        in_specs=[
            pl.BlockSpec((1, 1, tq, dh), lambda bi, h, qi: (bi, q_head0 + h, qi, 0)),
            pl.BlockSpec((1, 1, t, dh), lambda bi, h, qi: (bi, k_head0 + h, 0, 0)),
            pl.BlockSpec((1, 1, t, dh), lambda bi, h, qi: (bi, v_head0 + h, 0, 0)),
        ],
        out_specs=pl.BlockSpec((1, tq, dh), lambda bi, h, qi: (bi, qi, h)),
        scratch_shapes=[
            pltpu.VMEM((nb, blk, dh), BF16),
            pltpu.VMEM((nb, blk, dh), BF16),
        ],
        compiler_params=_params("parallel", "parallel", "arbitrary"),
    )(qkv, qkv, qkv)


def _xattn_kernel(q_ref, kv_ref, o_ref, *, n_heads, dh, scale):
    width = n_heads * dh
    for hh in range(n_heads):
        qh = q_ref[0, :, hh * dh:(hh + 1) * dh].astype(BF16)
        kh = kv_ref[0, :, hh * dh:(hh + 1) * dh].astype(BF16)
        vh = kv_ref[0, :, width + hh * dh:width + (hh + 1) * dh].astype(BF16)
        s = lax.dot_general(qh, kh, _NT, preferred_element_type=F32) * scale
        e = jnp.exp(s - jnp.max(s, axis=-1, keepdims=True))
        p = e / jnp.sum(e, axis=-1, keepdims=True)
        o_ref[0, :, hh * dh:(hh + 1) * dh] = jnp.dot(p.astype(BF16), vh, preferred_element_type=F32)


def cross_attention_core(q, kv, *, n_heads, tq=512):
    b, t, width = q.shape
    mlen = kv.shape[1]
    dh = width // n_heads
    tq = min(tq, t)
    kern = functools.partial(_xattn_kernel, n_heads=n_heads, dh=dh, scale=dh ** -0.5)
    return pl.pallas_call(
        kern,
        out_shape=jax.ShapeDtypeStruct((b, t, width), F32),
        grid=(b, t // tq),
        in_specs=[
            pl.BlockSpec((1, tq, width), lambda bi, i: (bi, i, 0)),
            pl.BlockSpec((1, mlen, 2 * width), lambda bi, i: (bi, 0, 0)),
        ],
        out_specs=pl.BlockSpec((1, tq, width), lambda bi, i: (bi, i, 0)),
        compiler_params=_params("parallel", "arbitrary"),
    )(q, kv)


def _extract_topk(work, n_rounds):
    rows = lax.broadcasted_iota(jnp.int32, work.shape, 0).astype(F32)
    n_rows = float(work.shape[0])
    vals = []
    for _ in range(n_rounds):
        mx = jnp.max(work, axis=0, keepdims=True)
        first = jnp.min(jnp.where(work == mx, rows, n_rows), axis=0, keepdims=True)
        work = jnp.where(rows == first, -jnp.inf, work)
        vals.append(mx)
    return vals, work


def _staircase_candidates(vals0, vals1, topk):
    sub = 8
    val1_all = jnp.concatenate(vals1, axis=0)
    val1_head = val1_all[:sub]
    row = lax.broadcasted_iota(jnp.int32, val1_head.shape, 0)
    slabs = [val1_all + vals0[0]]
    for a in range(1, sub):
        b_max = topk // (a + 1) - 1
        slab = val1_head + vals0[a]
        slabs.append(slab if b_max >= sub - 1 else jnp.where(row <= b_max, slab, -jnp.inf))
    slabs.append(jnp.concatenate(vals0[sub:], axis=0) + vals1[0])
    return jnp.concatenate(slabs, axis=0)


def _peer_select_kernel(q_ref, keys_ref, s0_ref, e0_ref, s1_ref, e1_ref, tau_ref, *, topk):
    scores, vals, left = [], [], []
    for p in range(2):
        qs = q_ref[:, p * LANES:(p + 1) * LANES].astype(BF16)
        s = lax.dot_general(keys_ref[p], qs, _NT, preferred_element_type=F32)
        v, rest = _extract_topk(s, topk)
        scores.append(s); vals.append(v); left.append(rest)
    tops, _ = _extract_topk(_staircase_candidates(vals[0], vals[1], topk), topk)
    z = jnp.zeros_like(tops[0])
    for tv in tops:
        z = z + jnp.exp(tv - tops[0])
    s0_ref[0] = scores[0]
    s1_ref[0] = scores[1]
    e0_ref[0] = jnp.where(left[0] == -jnp.inf, jnp.exp(scores[0] - vals[0][0]), 0.0) / z
    e1_ref[0] = jnp.where(left[1] == -jnp.inf, jnp.exp(scores[1] - vals[1][0]), 0.0)
    tau_ref[0] = tops[topk - 1]


def peer_select(q, keys_bf16, *, tm=256):
    n = q.shape[0]
    n_heads = keys_bf16.shape[0] // 2
    n_keys = keys_bf16.shape[1]
    tm = min(tm, n)
    big = jax.ShapeDtypeStruct((n_heads, n_keys, n), F32)
    big_spec = pl.BlockSpec((1, n_keys, tm), lambda i, h: (h, 0, i))
    kern = functools.partial(_peer_select_kernel, topk=PEER_TOPK)
    s0, e0, s1, e1, tau = pl.pallas_call(
        kern,
        out_shape=(big, big, big, big, jax.ShapeDtypeStruct((n_heads, 1, n), F32)),
        grid=(n // tm, n_heads),
        in_specs=[
            pl.BlockSpec((tm, 2 * LANES), lambda i, h: (i, h)),
            pl.BlockSpec((2, n_keys, keys_bf16.shape[2]), lambda i, h: (h, 0, 0)),
        ],
        out_specs=(big_spec, big_spec, big_spec, big_spec,
                   pl.BlockSpec((1, 1, tm), lambda i, h: (h, 0, i))),
        compiler_params=_params("parallel", "arbitrary"),
    )(q, keys_bf16)
    return s0, e0, s1, e1, tau.reshape(n_heads, n)


def _peer_main_kernel(x_ref, g_ref, u_ref, v_ref, s0_ref, e0_ref, s1_ref, e1_ref, tau_ref, o_ref,
                      h_ref, gate_ref, *, n_heads, n_keys, row_chunk):
    e = pl.program_id(1)
    tm = h_ref.shape[0]
    te = u_ref.shape[0]
    groups = te // n_keys

    @pl.when(e == 0)
    def _prologue():
        for r0 in range(0, tm, row_chunk):
            x = x_ref[r0:r0 + row_chunk, :]
            ms = jnp.mean(x * x, axis=-1, keepdims=True)
            h_ref[r0:r0 + row_chunk, :] = ((x * lax.rsqrt(ms + RMS_EPS)) * g_ref[...]).astype(BF16)
            o_ref[r0:r0 + row_chunk, :] = x

    act = lax.dot_general(u_ref[...], h_ref[...], _NT, preferred_element_type=F32)
    for gi in range(groups):
        i = e * groups + gi
        route = jnp.zeros((n_keys, tm), F32)
        for h in range(n_heads):
            a0 = s0_ref[h, pl.ds(i, 1), :]
            w0 = e0_ref[h, pl.ds(i, 1), :]
            cand = s1_ref[h] + a0
            route = route + jnp.where(cand >= tau_ref[h:h + 1, :], e1_ref[h] * w0, 0.0)
        a = act[gi * n_keys:(gi + 1) * n_keys, :]
        gelu = 0.5 * a * (1.0 + lax.erf(a * np.float32(np.sqrt(0.5))))
        gate_ref[gi * n_keys:(gi + 1) * n_keys, :] = (gelu * route).astype(BF16)
    o_ref[...] += lax.dot_general(gate_ref[...], v_ref[...], _TN, preferred_element_type=F32)


def peer_main(x, g, u_bf16, v_bf16, s0, e0, s1, e1, tau, *, tm=512, te=512):
    n, d = x.shape
    n_exp = u_bf16.shape[0]
    n_heads, n_keys, _ = s0.shape
    tm = min(tm, n)
    assert n % tm == 0 and n_exp % te == 0 and te % n_keys == 0 and tm % LANES == 0
    once = pl.Buffered(1)
    big_spec = pl.BlockSpec((n_heads, n_keys, tm), lambda i, e: (0, 0, i), pipeline_mode=once)
    kern = functools.partial(_peer_main_kernel, n_heads=n_heads, n_keys=n_keys,
                             row_chunk=min(128, tm))
    return pl.pallas_call(
        kern,
        out_shape=jax.ShapeDtypeStruct((n, d), F32),
        grid=(n // tm, n_exp // te),
        in_specs=[
            pl.BlockSpec((tm, d), lambda i, e: (i, 0), pipeline_mode=once),
            pl.BlockSpec((1, d), lambda i, e: (0, 0)),
            pl.BlockSpec((te, d), lambda i, e: (e, 0)),
            pl.BlockSpec((te, d), lambda i, e: (e, 0)),
            big_spec, big_spec, big_spec, big_spec,
            pl.BlockSpec((n_heads, tm), lambda i, e: (0, i)),
        ],
        out_specs=pl.BlockSpec((tm, d), lambda i, e: (i, 0), pipeline_mode=once),
        scratch_shapes=[
            pltpu.VMEM((tm, d), BF16),
            pltpu.VMEM((te, tm), BF16),
        ],
        compiler_params=_params("parallel", "arbitrary"),
    )(x, g.reshape(1, d).astype(F32), u_bf16, v_bf16, s0, e0, s1, e1, tau)


def _layer(x, mem, norm_mix_g, w_in, moba_q_norm_g, moba_k_norm_g, moba_out_norm_g,
           sb_out_norm_g, w_out, norm_xattn_g, norm_mem_g, w_xq, w_xkv, xattn_q_norm_g,
           xattn_k_norm_g, w_xo, norm_ffn_g, w_peer_q, peer_sub_keys, peer_u, peer_v):
    b, t, d = x.shape
    n = b * t
    xf = x.reshape(n, d)
    mix_width = w_in.shape[1] // 3
    n_heads = mix_width // (2 * HEAD_DIM)
    grp = n_heads * HEAD_DIM
    slopes = jnp.asarray(2.0 ** (-8.0 * np.arange(1, n_heads + 1) / n_heads), dtype=F32)

    col_gain = jnp.concatenate([jnp.tile(moba_q_norm_g, n_heads), jnp.tile(moba_k_norm_g, n_heads),
                                jnp.ones((w_in.shape[1] - 2 * grp,), F32)])
    qkv = norm_matmul([xf], [norm_mix_g], w_in.astype(BF16), col_gain=col_gain,
                      n_norm_cols=2 * grp, head_major_bt=(b, t), tn=1024)
    qk_norm_bound = HEAD_DIM * jnp.max(jnp.abs(moba_q_norm_g)) * jnp.max(jnp.abs(moba_k_norm_g))
    o_moba = moba_attention(qkv, slopes, qk_norm_bound, q_head0=0, k_head0=n_heads,
                            v_head0=2 * n_heads, n_heads=n_heads)
    o_sb = stick_breaking_attention(qkv, q_head0=3 * n_heads, k_head0=4 * n_heads,
                                    v_head0=5 * n_heads, n_heads=n_heads)
    x1 = norm_matmul([o_moba.reshape(n, grp), o_sb.reshape(n, grp)],
                     [moba_out_norm_g, sb_out_norm_g], w_out.astype(BF16), residual=xf, tn=1024)

    xw = w_xq.shape[1]
    n_mem = mem.shape[1]
    kv_gain = jnp.concatenate([jnp.tile(xattn_k_norm_g, N_XATTN_HEADS), jnp.ones((xw,), F32)])
    kv = norm_matmul([mem.reshape(b * n_mem, d)], [norm_mem_g], w_xkv.astype(BF16),
                     col_gain=kv_gain, n_norm_cols=xw)
    xq = norm_matmul([x1], [norm_xattn_g], w_xq.astype(BF16),
                     col_gain=jnp.tile(xattn_q_norm_g, N_XATTN_HEADS), n_norm_cols=xw)
    o_x = cross_attention_core(xq.reshape(b, t, xw), kv.reshape(b, n_mem, 2 * xw),
                               n_heads=N_XATTN_HEADS)
    x2 = norm_matmul([o_x.reshape(n, xw)], [None], w_xo.astype(BF16), residual=x1, tn=2048)

    pq = norm_matmul([x2], [norm_ffn_g], w_peer_q.astype(BF16), tn=1024)
    keys = peer_sub_keys.reshape(PEER_HEADS * 2, PEER_N_KEYS, -1).astype(BF16)
    s0, e0, s1, e1, tau = peer_select(pq, keys)
    x3 = peer_main(x2, norm_ffn_g, peer_u.astype(BF16), peer_v.astype(BF16), s0, e0, s1, e1, tau)
    return x3.reshape(b, t, d)


def kernel(x, mem, norm_mix_g, w_in, moba_q_norm_g, moba_k_norm_g, moba_out_norm_g, sb_out_norm_g,
           w_out, norm_xattn_g, norm_mem_g, w_xq, w_xkv, xattn_q_norm_g, xattn_k_norm_g, w_xo,
           norm_ffn_g, w_peer_q, peer_sub_keys, peer_u, peer_v):
    depth = w_in.shape[0]
    for l in range(depth):
        x = _layer(x, mem, norm_mix_g[l], w_in[l], moba_q_norm_g[l], moba_k_norm_g[l],
                   moba_out_norm_g[l], sb_out_norm_g[l], w_out[l], norm_xattn_g[l], norm_mem_g[l],
                   w_xq[l], w_xkv[l], xattn_q_norm_g[l], xattn_k_norm_g[l], w_xo[l], norm_ffn_g[l],
                   w_peer_q[l], peer_sub_keys[l], peer_u[l], peer_v[l])
    return x
```

```python
import functools

import numpy as np
import jax
import jax.numpy as jnp
from jax import lax
from jax.experimental import pallas as pl
from jax.experimental.pallas import tpu as pltpu

HEAD_DIM = 128
MOBA_BLOCK = 256
MOBA_TOPK = 3
N_XATTN_HEADS = 4
PEER_HEADS = 8
PEER_N_KEYS = 128
PEER_TOPK = 16
RMS_EPS = 1e-6
NEG_INF = -1e30
EXP_ZERO = -110.0
EXP2_ZERO = -160.0
LOG2E = 1.4426950408889634
NORM_MARGIN = 1.02

LANES = 128
VMEM_LIMIT = 56 * 1024 * 1024

F32 = jnp.float32
BF16 = jnp.bfloat16

_NT = (((1,), (1,)), ((), ()))
_TN = (((0,), (0,)), ((), ()))


def _params(*sem):
    return pltpu.CompilerParams(dimension_semantics=sem, vmem_limit_bytes=VMEM_LIMIT)


def _split_bf16(x):
    hi = x.astype(BF16)
    lo = (x - hi.astype(F32)).astype(BF16)
    return hi, lo


def _norm_matmul_kernel(*refs, k_sizes, has_gain, n_norm_tiles, n_col_tiles, has_res,
                        head_major, row_chunk):
    n_in = len(k_sizes)
    pos = 0
    x_refs = refs[pos:pos + n_in]; pos += n_in
    g_refs = []
    for hg in has_gain:
        if hg:
            g_refs.append(refs[pos]); pos += 1
        else:
            g_refs.append(None)
    w_ref = refs[pos]; pos += 1
    cg_ref = None
    if n_norm_tiles > 0:
        cg_ref = refs[pos]; pos += 1
    res_ref = None
    if has_res:
        res_ref = refs[pos]; pos += 1
    o_ref = refs[pos]; pos += 1
    h_ref = refs[pos]

    j = pl.program_id(1)
    tm = h_ref.shape[0]

    @pl.when(j == 0)
    def _prologue():
        off = 0
        for x_ref, g_ref, ksz in zip(x_refs, g_refs, k_sizes):
            for r0 in range(0, tm, row_chunk):
                x = x_ref[r0:r0 + row_chunk, :]
                if g_ref is not None:
                    ms = jnp.mean(x * x, axis=-1, keepdims=True)
                    x = (x * lax.rsqrt(ms + RMS_EPS)) * g_ref[...]
                h_ref[r0:r0 + row_chunk, off:off + ksz] = x.astype(BF16)
            off += ksz

    acc = jnp.dot(h_ref[...], w_ref[...], preferred_element_type=F32)
    tn = acc.shape[1]

    def finish(get_group):
        for hh in range(tn // LANES):
            y = get_group(hh)
            if res_ref is not None:
                y = y + res_ref[:, hh * LANES:(hh + 1) * LANES]
            if head_major:
                o_ref[0, hh] = y
            else:
                o_ref[:, hh * LANES:(hh + 1) * LANES] = y

    def plain_group(hh):
        return acc[:, hh * LANES:(hh + 1) * LANES]

    def normed_group(hh):
        a = acc[:, hh * LANES:(hh + 1) * LANES]
        ms = jnp.mean(a * a, axis=-1, keepdims=True)
        return (a * lax.rsqrt(ms + RMS_EPS)) * cg_ref[:, hh * LANES:(hh + 1) * LANES]

    if n_norm_tiles == 0:
        finish(plain_group)
    elif n_norm_tiles >= n_col_tiles:
        finish(normed_group)
    else:
        @pl.when(j < n_norm_tiles)
        def _():
            finish(normed_group)

        @pl.when(j >= n_norm_tiles)
        def _():
            finish(plain_group)


def norm_matmul(xs, gains, w, *, col_gain=None, n_norm_cols=0, residual=None,
                head_major_bt=None, tm=512, tn=512):
    m = xs[0].shape[0]
    k_sizes = tuple(int(x.shape[1]) for x in xs)
    k_total = sum(k_sizes)
    n_cols = w.shape[1]
    tm = min(tm, m)
    tn = min(tn, n_cols)
    assert m % tm == 0 and n_cols % tn == 0 and w.shape[0] == k_total
    assert n_norm_cols % tn == 0
    n_col_tiles = n_cols // tn
    n_norm_tiles = n_norm_cols // tn
    has_gain = tuple(g is not None for g in gains)

    args, in_specs = [], []
    for x, ksz in zip(xs, k_sizes):
        args.append(x)
        in_specs.append(pl.BlockSpec((tm, ksz), lambda i, j: (i, 0)))
    for g, ksz in zip(gains, k_sizes):
        if g is not None:
            args.append(g.reshape(1, ksz).astype(F32))
            in_specs.append(pl.BlockSpec((1, ksz), lambda i, j: (0, 0)))
    args.append(w)
    in_specs.append(pl.BlockSpec((k_total, tn), lambda i, j: (0, j)))
    if n_norm_tiles > 0:
        args.append(col_gain.reshape(1, n_cols).astype(F32))
        in_specs.append(pl.BlockSpec((1, tn), lambda i, j: (0, j)))
    if residual is not None:
        args.append(residual)
        in_specs.append(pl.BlockSpec((tm, tn), lambda i, j: (i, j)))

    if head_major_bt is not None:
        b, t = head_major_bt
        assert b * t == m and t % tm == 0
        tiles_per_b = t // tm
        out_shape = jax.ShapeDtypeStruct((b, n_cols // LANES, t, LANES), F32)
        out_spec = pl.BlockSpec((1, tn // LANES, tm, LANES),
                                lambda i, j: (i // tiles_per_b, j, i % tiles_per_b, 0))
    else:
        out_shape = jax.ShapeDtypeStruct((m, n_cols), F32)
        out_spec = pl.BlockSpec((tm, tn), lambda i, j: (i, j))

    kern = functools.partial(
        _norm_matmul_kernel, k_sizes=k_sizes, has_gain=has_gain, n_norm_tiles=n_norm_tiles,
        n_col_tiles=n_col_tiles, has_res=residual is not None,
        head_major=head_major_bt is not None, row_chunk=min(128, tm))
    return pl.pallas_call(
        kern,
        out_shape=out_shape,
        grid=(m // tm, n_col_tiles),
        in_specs=in_specs,
        out_specs=out_spec,
        scratch_shapes=[pltpu.VMEM((tm, k_total), BF16)],
        compiler_params=_params("parallel", "arbitrary"),
    )(*args)


def _select_blocks(kmean, q_tile, tile, nb):
    blk = q_tile.shape[0]
    if tile == 0:
        return jnp.zeros((nb, blk), F32)
    kh, kl = _split_bf16(kmean)
    qh, ql = _split_bf16(q_tile)
    gate = (lax.dot_general(kh, qh, _NT, preferred_element_type=F32)
            + lax.dot_general(kh, ql, _NT, preferred_element_type=F32)
            + lax.dot_general(kl, qh, _NT, preferred_element_type=F32))
    blkid = lax.broadcasted_iota(jnp.int32, (nb, blk), 0)
    valid = blkid < tile
    gate = jnp.where(valid, gate, NEG_INF)
    terms = []
    for mth in range(tile):
        gm = gate[mth:mth + 1, :]
        terms.append(jnp.where(gm > gate, 1.0,
                               jnp.where(gm == gate, jnp.where(blkid > mth, 1.0, 0.0), 0.0)))
    while len(terms) > 1:
        terms = [terms[k] + terms[k + 1] for k in range(0, len(terms) - 1, 2)] + terms[len(terms) & ~1:]
    return jnp.where(valid, jnp.where(terms[0] < MOBA_TOPK, 1.0, 0.0), 0.0)


def _moba_kernel(slopes_ref, first_ref, q_ref, qall_ref, k_ref, v_ref, o_ref,
                 kb_ref, vt_ref, kmean_ref, sel_all_ref, m_ref, l_ref, acc_ref, s_ref,
                 *, nb, blk, scale):
    h = pl.program_id(1)
    qi = pl.program_id(2)

    scale2 = scale * LOG2E
    slope2 = slopes_ref[h] * LOG2E

    @pl.when(qi == 0)
    def _prep():
        for n in range(nb):
            kblk = k_ref[0, 0, n * blk:(n + 1) * blk, :]
            kb_ref[n] = kblk.astype(BF16)
            kmean_ref[n:n + 1, :] = jnp.mean(kblk, axis=0, keepdims=True)
            vt_ref[n] = v_ref[0, 0, n * blk:(n + 1) * blk, :].T.astype(BF16)
        kmean = kmean_ref[...]
        for tile in range(nb):
            sel_all_ref[tile] = _select_blocks(
                kmean, qall_ref[0, 0, tile * blk:(tile + 1) * blk, :], tile, nb)

    q = q_ref[0, 0]
    qb = q.astype(BF16)
    sel_ref = sel_all_ref.at[qi]

    t2 = lax.broadcasted_iota(jnp.int32, (2 * blk, blk), 1)
    j2 = lax.broadcasted_iota(jnp.int32, (2 * blk, blk), 0)
    bias_pair = (t2 - j2).astype(F32) * (-slope2)

    def pair_scores(pair):
        n0 = 2 * pair
        sel0 = sel_ref[pl.ds(n0, 1), :]
        sel1 = sel_ref[pl.ds(n0 + 1, 1), :]
        kpair = jnp.concatenate([kb_ref[n0], kb_ref[n0 + 1]], axis=0)
        s = lax.dot_general(kpair, qb, _NT, preferred_element_type=F32) * scale2 + bias_pair
        return jnp.concatenate([jnp.where(sel0 > 0, s[:blk], NEG_INF),
                                jnp.where(sel1 > 0, s[blk:], NEG_INF)], axis=0)

    def pair_offset(pair):
        return slope2 * ((qi - 2 * pair) * blk).astype(F32)

    last_pair = nb // 2 - 1

    def body(pair, carry):
        n0 = 2 * pair
        s = s_ref[pair % 2]
        s_ref[(pair + 1) % 2] = pair_scores(jnp.minimum(pair + 1, last_pair))
        off = pair_offset(pair)
        m_old = m_ref[...]
        m_new = jnp.maximum(m_old, jnp.max(s, axis=0, keepdims=True) - off)
        alpha = jnp.exp2(m_old - m_new)
        p = jnp.exp2(s - (m_new + off))
        pb = p.astype(BF16)
        l_ref[...] = alpha * l_ref[...] + jnp.sum(p, axis=0, keepdims=True)
        acc_ref[...] = (alpha * acc_ref[...]
                        + jnp.dot(vt_ref[n0], pb[:blk], preferred_element_type=F32)
                        + jnp.dot(vt_ref[n0 + 1], pb[blk:], preferred_element_type=F32))
        m_ref[...] = m_new
        return carry

    first_pair = first_ref[h, qi]
    s_ref[first_pair % 2] = pair_scores(jnp.minimum(first_pair, last_pair))

    t_idx = lax.broadcasted_iota(jnp.int32, (blk, blk), 1)
    j_idx = lax.broadcasted_iota(jnp.int32, (blk, blk), 0)
    dist_own = (t_idx - j_idx).astype(F32)
    s = lax.dot_general(kb_ref[qi], qb, _NT, preferred_element_type=F32) * scale2
    s = s - slope2 * dist_own
    s = jnp.where(dist_own >= 0, s, NEG_INF)
    m0 = jnp.max(s, axis=0, keepdims=True)
    p = jnp.exp2(s - m0)
    m_ref[...] = m0
    l_ref[...] = jnp.sum(p, axis=0, keepdims=True)
    acc_ref[...] = jnp.dot(vt_ref[qi], p.astype(BF16), preferred_element_type=F32)

    lax.fori_loop(first_pair, (qi + 1) // 2, body, 0)
    o_ref[0] = (acc_ref[...] / l_ref[...]).T


def _moba_first_pairs(slopes, qk_norm_bound, scale, nb, blk):
    slope2 = slopes * LOG2E
    room = 2.0 * (qk_norm_bound * NORM_MARGIN * scale * LOG2E) - EXP2_ZERO
    d = jnp.arange(nb, dtype=F32)
    n_live = jnp.sum(d[None, :] * (slope2[:, None] * blk) + slope2[:, None] <= room, axis=1)
    tiles = jnp.arange(nb, dtype=jnp.int32)
    return jnp.maximum(tiles[None, :] - n_live[:, None].astype(jnp.int32), 0) // 2


def moba_attention(qkv, slopes, qk_norm_bound, *, q_head0, k_head0, v_head0, n_heads):
    b, _, t, dh = qkv.shape
    blk = MOBA_BLOCK
    assert t % (2 * blk) == 0
    nb = t // blk
    scale = dh ** -0.5
    first_pairs = _moba_first_pairs(slopes, qk_norm_bound, scale, nb, blk)
    kern = functools.partial(_moba_kernel, nb=nb, blk=blk, scale=scale)
    return pl.pallas_call(
        kern,
        out_shape=jax.ShapeDtypeStruct((b, t, n_heads * dh), F32),
        grid=(b, n_heads, nb),
        in_specs=[
            pl.BlockSpec(memory_space=pltpu.SMEM),
            pl.BlockSpec(memory_space=pltpu.SMEM),
            pl.BlockSpec((1, 1, blk, dh), lambda bi, h, qi: (bi, q_head0 + h, qi, 0)),
            pl.BlockSpec((1, 1, t, dh), lambda bi, h, qi: (bi, q_head0 + h, 0, 0)),
            pl.BlockSpec((1, 1, t, dh), lambda bi, h, qi: (bi, k_head0 + h, 0, 0)),
            pl.BlockSpec((1, 1, t, dh), lambda bi, h, qi: (bi, v_head0 + h, 0, 0)),
        ],
        out_specs=pl.BlockSpec((1, blk, dh), lambda bi, h, qi: (bi, qi, h)),
        scratch_shapes=[
            pltpu.VMEM((nb, blk, dh), BF16),
            pltpu.VMEM((nb, dh, blk), BF16),
            pltpu.VMEM((nb, dh), F32),
            pltpu.VMEM((nb, nb, blk), F32),
            pltpu.VMEM((1, blk), F32),
            pltpu.VMEM((1, blk), F32),
            pltpu.VMEM((dh, blk), F32),
            pltpu.VMEM((2, 2 * blk, blk), F32),
        ],
        compiler_params=_params("parallel", "parallel", "arbitrary"),
    )(slopes, first_pairs, qkv, qkv, qkv, qkv)


def _sb_kernel(q_ref, k_ref, v_ref, o_ref, kb_ref, vb_ref, *, nb, blk, q_tiles, scale):
    step = pl.program_id(2)

    @pl.when(step == 0)
    def _prep():
        for n in range(nb):
            kb_ref[n] = k_ref[0, 0, n * blk:(n + 1) * blk, :].astype(BF16)
            vb_ref[n] = v_ref[0, 0, n * blk:(n + 1) * blk, :].astype(BF16)

    row = lax.broadcasted_iota(jnp.int32, (blk, blk), 0)
    col = lax.broadcasted_iota(jnp.int32, (blk, blk), 1)
    after = jnp.where(row > col, 1.0, 0.0).astype(BF16)
    strict = col < row

    def log_sigmoids(z):
        t = jnp.log(1.0 + jnp.exp(-jnp.abs(z)))
        log_beta = jnp.minimum(z, 0.0) - t
        return log_beta, log_beta - z

    def sum_after(x):
        hi, lo = _split_bf16(x)
        return (jnp.dot(hi, after, preferred_element_type=F32)
                + jnp.dot(lo, after, preferred_element_type=F32))

    def row_sum(x):
        return jnp.sum(x, axis=1, keepdims=True)

    def pair_step(qb, n_late, late_is_own, acc, carry):
        has_early = n_late >= 1
        n_early = jnp.maximum(n_late - 1, 0)
        kpair = jnp.concatenate([kb_ref[n_early], kb_ref[n_late]], axis=0)
        z = lax.dot_general(qb, kpair, _NT, preferred_element_type=F32) * scale
        log_beta, log_om = log_sigmoids(z)
        om_early, om_late = log_om[:, :blk], log_om[:, blk:]
        if late_is_own:
            om_late = jnp.where(strict, om_late, 0.0)
            sum_late = row_sum(om_late)
            a_late = jnp.where(strict, jnp.exp(log_beta[:, blk:] + sum_after(om_late)), 0.0)
            left = sum_late
        else:
            sum_late = row_sum(om_late)
            a_late = jnp.exp(log_beta[:, blk:] + sum_after(om_late) + carry)
            left = carry + sum_late
        a_early = jnp.exp(log_beta[:, :blk] + sum_after(om_early) + left)
        v_early = vb_ref[n_early]
        v_early = jnp.where(has_early, v_early, jnp.zeros_like(v_early))
        contrib = (jnp.dot(a_late.astype(BF16), vb_ref[n_late], preferred_element_type=F32)
                   + jnp.dot(a_early.astype(BF16), v_early, preferred_element_type=F32))
        acc = contrib if acc is None else acc + contrib
        return acc, left + row_sum(om_early)

    tiles = [q_tiles * step + k for k in range(q_tiles)]
    qbs = [q_ref[0, 0, k * blk:(k + 1) * blk, :].astype(BF16) for k in range(q_tiles)]
    firsts = [pair_step(qb, qi, True, None, None) for qb, qi in zip(qbs, tiles)]

    for k, (qb, qi, (acc0, carry0)) in enumerate(zip(qbs, tiles, firsts)):
        n_pairs = qi // 2

        def cond(state, n_pairs=n_pairs):
            pair, _, _, carry_max = state
            return jnp.logical_and(pair < n_pairs, carry_max > EXP_ZERO)

        def body(state, qb=qb, qi=qi):
            pair, acc, carry, _ = state
            acc, carry = pair_step(qb, qi - 2 - 2 * pair, False, acc, carry)
            return pair + 1, acc, carry, jnp.max(carry)

        _, acc, _, _ = lax.while_loop(cond, body, (jnp.int32(0), acc0, carry0, jnp.max(carry0)))
        o_ref[0, k * blk:(k + 1) * blk, :] = acc


def stick_breaking_attention(qkv, *, q_head0, k_head0, v_head0, n_heads, blk=256, q_tiles=2):
    b, _, t, dh = qkv.shape
    assert t % (blk * q_tiles) == 0
    nb = t // blk
    tq = blk * q_tiles
    kern = functools.partial(_sb_kernel, nb=nb, blk=blk, q_tiles=q_tiles, scale=dh ** -0.5)
    return pl.pallas_call(
        kern,
        out_shape=jax.ShapeDtypeStruct((b, t, n_heads * dh), F32),
        grid=(b, n_heads, t // tq),
        in_specs=[
            pl.BlockSpec((1, 1, tq, dh), lambda bi, h, qi: (bi, q_head0 + h, qi, 0)),
            pl.BlockSpec((1, 1, t, dh), lambda bi, h, qi: (bi, k_head0 + h, 0, 0)),
            pl.BlockSpec((1, 1, t, dh), lambda bi, h, qi: (bi, v_head0 + h, 0, 0)),
        ],
        out_specs=pl.BlockSpec((1, tq, dh), lambda bi, h, qi: (bi, qi, h)),
        scratch_shapes=[
            pltpu.VMEM((nb, blk, dh), BF16),
            pltpu.VMEM((nb, blk, dh), BF16),
        ],
        compiler_params=_params("parallel", "parallel", "arbitrary"),
    )(qkv, qkv, qkv)


def _xattn_kernel(q_ref, kv_ref, o_ref, *, n_heads, dh, scale):
    width = n_heads * dh
    for hh in range(n_heads):
        qh = q_ref[0, :, hh * dh:(hh + 1) * dh].astype(BF16)
        kh = kv_ref[0, :, hh * dh:(hh + 1) * dh].astype(BF16)
        vh = kv_ref[0, :, width + hh * dh:width + (hh + 1) * dh].astype(BF16)
        s = lax.dot_general(qh, kh, _NT, preferred_element_type=F32) * scale
        e = jnp.exp(s - jnp.max(s, axis=-1, keepdims=True))
        p = e / jnp.sum(e, axis=-1, keepdims=True)
        o_ref[0, :, hh * dh:(hh + 1) * dh] = jnp.dot(p.astype(BF16), vh, preferred_element_type=F32)


def cross_attention_core(q, kv, *, n_heads, tq=512):
    b, t, width = q.shape
    mlen = kv.shape[1]
    dh = width // n_heads
    tq = min(tq, t)
    kern = functools.partial(_xattn_kernel, n_heads=n_heads, dh=dh, scale=dh ** -0.5)
    return pl.pallas_call(
        kern,
        out_shape=jax.ShapeDtypeStruct((b, t, width), F32),
        grid=(b, t // tq),
        in_specs=[
            pl.BlockSpec((1, tq, width), lambda bi, i: (bi, i, 0)),
            pl.BlockSpec((1, mlen, 2 * width), lambda bi, i: (bi, 0, 0)),
        ],
        out_specs=pl.BlockSpec((1, tq, width), lambda bi, i: (bi, i, 0)),
        compiler_params=_params("parallel", "arbitrary"),
    )(q, kv)


def _extract_topk(work, n_rounds):
    rows = lax.broadcasted_iota(jnp.int32, work.shape, 0).astype(F32)
    n_rows = float(work.shape[0])
    vals = []
    for _ in range(n_rounds):
        mx = jnp.max(work, axis=0, keepdims=True)
        first = jnp.min(jnp.where(work == mx, rows, n_rows), axis=0, keepdims=True)
        work = jnp.where(rows == first, -jnp.inf, work)
        vals.append(mx)
    return vals, work


def _staircase_candidates(vals0, vals1, topk):
    sub = 8
    val1_all = jnp.concatenate(vals1, axis=0)
    val1_head = val1_all[:sub]
    row = lax.broadcasted_iota(jnp.int32, val1_head.shape, 0)
    slabs = [val1_all + vals0[0]]
    for a in range(1, sub):
        b_max = topk // (a + 1) - 1
        slab = val1_head + vals0[a]
        slabs.append(slab if b_max >= sub - 1 else jnp.where(row <= b_max, slab, -jnp.inf))
    slabs.append(jnp.concatenate(vals0[sub:], axis=0) + vals1[0])
    return jnp.concatenate(slabs, axis=0)


def _peer_select_kernel(q_ref, keys_ref, s0_ref, e0_ref, s1_ref, e1_ref, tau_ref, *, topk, heads):
    for hh in range(heads):
        scores, vals, left = [], [], []
        for p in range(2):
            c0 = (2 * hh + p) * LANES
            qs = q_ref[:, c0:c0 + LANES].astype(BF16)
            s = lax.dot_general(keys_ref[2 * hh + p], qs, _NT, preferred_element_type=F32)
            v, rest = _extract_topk(s, topk)
            scores.append(s); vals.append(v); left.append(rest)
        tops, _ = _extract_topk(_staircase_candidates(vals[0], vals[1], topk), topk)
        z = jnp.zeros_like(tops[0])
        for tv in tops:
            z = z + jnp.exp(tv - tops[0])
        s0_ref[hh] = scores[0]
        s1_ref[hh] = scores[1]
        e0_ref[hh] = jnp.where(left[0] == -jnp.inf, jnp.exp(scores[0] - vals[0][0]), 0.0) / z
        e1_ref[hh] = jnp.where(left[1] == -jnp.inf, jnp.exp(scores[1] - vals[1][0]), 0.0)
        tau_ref[hh] = tops[topk - 1]


def peer_select(q, keys_bf16, *, tm=256, heads_per_step=4):
    n = q.shape[0]
    n_heads = keys_bf16.shape[0] // 2
    n_keys = keys_bf16.shape[1]
    tm = min(tm, n)
    hs = heads_per_step
    assert n_heads % hs == 0
    big = jax.ShapeDtypeStruct((n_heads, n_keys, n), F32)
    big_spec = pl.BlockSpec((hs, n_keys, tm), lambda i, h: (h, 0, i))
    kern = functools.partial(_peer_select_kernel, topk=PEER_TOPK, heads=hs)
    s0, e0, s1, e1, tau = pl.pallas_call(
        kern,
        out_shape=(big, big, big, big, jax.ShapeDtypeStruct((n_heads, 1, n), F32)),
        grid=(n // tm, n_heads // hs),
        in_specs=[
            pl.BlockSpec((tm, hs * 2 * LANES), lambda i, h: (i, h)),
            pl.BlockSpec((hs * 2, n_keys, keys_bf16.shape[2]), lambda i, h: (h, 0, 0)),
        ],
        out_specs=(big_spec, big_spec, big_spec, big_spec,
                   pl.BlockSpec((hs, 1, tm), lambda i, h: (h, 0, i))),
        compiler_params=_params("parallel", "arbitrary"),
    )(q, keys_bf16)
    return s0, e0, s1, e1, tau.reshape(n_heads, n)


def _peer_main_kernel(x_ref, g_ref, u_ref, v_ref, s0_ref, e0_ref, s1_ref, e1_ref, tau_ref, o_ref,
                      h_ref, gate_ref, *, n_heads, n_keys, row_chunk):
    e = pl.program_id(1)
    tm = h_ref.shape[0]
    te = u_ref.shape[0]
    groups = te // n_keys

    @pl.when(e == 0)
    def _prologue():
        for r0 in range(0, tm, row_chunk):
            x = x_ref[r0:r0 + row_chunk, :]
            ms = jnp.mean(x * x, axis=-1, keepdims=True)
            h_ref[r0:r0 + row_chunk, :] = ((x * lax.rsqrt(ms + RMS_EPS)) * g_ref[...]).astype(BF16)
            o_ref[r0:r0 + row_chunk, :] = x

    act = lax.dot_general(u_ref[...], h_ref[...], _NT, preferred_element_type=F32)
    for gi in range(groups):
        i = e * groups + gi
        route = jnp.zeros((n_keys, tm), F32)
        for h in range(n_heads):
            a0 = s0_ref[h, pl.ds(i, 1), :]
            w0 = e0_ref[h, pl.ds(i, 1), :]
            cand = s1_ref[h] + a0
            route = route + jnp.where(cand >= tau_ref[h:h + 1, :], e1_ref[h] * w0, 0.0)
        a = act[gi * n_keys:(gi + 1) * n_keys, :]
        gelu = 0.5 * a * (1.0 + lax.erf(a * np.float32(np.sqrt(0.5))))
        gate_ref[gi * n_keys:(gi + 1) * n_keys, :] = (gelu * route).astype(BF16)
    o_ref[...] += lax.dot_general(gate_ref[...], v_ref[...], _TN, preferred_element_type=F32)


def peer_main(x, g, u_bf16, v_bf16, s0, e0, s1, e1, tau, *, tm=512, te=512):
    n, d = x.shape
    n_exp = u_bf16.shape[0]
    n_heads, n_keys, _ = s0.shape
    tm = min(tm, n)
    assert n % tm == 0 and n_exp % te == 0 and te % n_keys == 0 and tm % LANES == 0
    once = pl.Buffered(1)
    big_spec = pl.BlockSpec((n_heads, n_keys, tm), lambda i, e: (0, 0, i), pipeline_mode=once)
    kern = functools.partial(_peer_main_kernel, n_heads=n_heads, n_keys=n_keys,
                             row_chunk=min(128, tm))
    return pl.pallas_call(
        kern,
        out_shape=jax.ShapeDtypeStruct((n, d), F32),
        grid=(n // tm, n_exp // te),
        in_specs=[
            pl.BlockSpec((tm, d), lambda i, e: (i, 0), pipeline_mode=once),
            pl.BlockSpec((1, d), lambda i, e: (0, 0)),
            pl.BlockSpec((te, d), lambda i, e: (e, 0)),
            pl.BlockSpec((te, d), lambda i, e: (e, 0)),
            big_spec, big_spec, big_spec, big_spec,
            pl.BlockSpec((n_heads, tm), lambda i, e: (0, i)),
        ],
        out_specs=pl.BlockSpec((tm, d), lambda i, e: (i, 0), pipeline_mode=once),
        scratch_shapes=[
            pltpu.VMEM((tm, d), BF16),
            pltpu.VMEM((te, tm), BF16),
        ],
        compiler_params=_params("parallel", "arbitrary"),
    )(x, g.reshape(1, d).astype(F32), u_bf16, v_bf16, s0, e0, s1, e1, tau)


def _layer(x, mem, norm_mix_g, w_in, moba_q_norm_g, moba_k_norm_g, moba_out_norm_g,
           sb_out_norm_g, w_out, norm_xattn_g, norm_mem_g, w_xq, w_xkv, xattn_q_norm_g,
           xattn_k_norm_g, w_xo, norm_ffn_g, w_peer_q, peer_sub_keys, peer_u, peer_v):
    b, t, d = x.shape
    n = b * t
    xf = x.reshape(n, d)
    mix_width = w_in.shape[1] // 3
    n_heads = mix_width // (2 * HEAD_DIM)
    grp = n_heads * HEAD_DIM
    slopes = jnp.asarray(2.0 ** (-8.0 * np.arange(1, n_heads + 1) / n_heads), dtype=F32)

    col_gain = jnp.concatenate([jnp.tile(moba_q_norm_g, n_heads), jnp.tile(moba_k_norm_g, n_heads),
                                jnp.ones((w_in.shape[1] - 2 * grp,), F32)])
    qkv = norm_matmul([xf], [norm_mix_g], w_in.astype(BF16), col_gain=col_gain,
                      n_norm_cols=2 * grp, head_major_bt=(b, t), tn=1024)
    qk_norm_bound = HEAD_DIM * jnp.max(jnp.abs(moba_q_norm_g)) * jnp.max(jnp.abs(moba_k_norm_g))
    o_moba = moba_attention(qkv, slopes, qk_norm_bound, q_head0=0, k_head0=n_heads,
                            v_head0=2 * n_heads, n_heads=n_heads)
    o_sb = stick_breaking_attention(qkv, q_head0=3 * n_heads, k_head0=4 * n_heads,
                                    v_head0=5 * n_heads, n_heads=n_heads)
    x1 = norm_matmul([o_moba.reshape(n, grp), o_sb.reshape(n, grp)],
                     [moba_out_norm_g, sb_out_norm_g], w_out.astype(BF16), residual=xf, tn=1024)

    xw = w_xq.shape[1]
    n_mem = mem.shape[1]
    kv_gain = jnp.concatenate([jnp.tile(xattn_k_norm_g, N_XATTN_HEADS), jnp.ones((xw,), F32)])
    kv = norm_matmul([mem.reshape(b * n_mem, d)], [norm_mem_g], w_xkv.astype(BF16),
                     col_gain=kv_gain, n_norm_cols=xw)
    xq = norm_matmul([x1], [norm_xattn_g], w_xq.astype(BF16),
                     col_gain=jnp.tile(xattn_q_norm_g, N_XATTN_HEADS), n_norm_cols=xw)
    o_x = cross_attention_core(xq.reshape(b, t, xw), kv.reshape(b, n_mem, 2 * xw),
                               n_heads=N_XATTN_HEADS)
    x2 = norm_matmul([o_x.reshape(n, xw)], [None], w_xo.astype(BF16), residual=x1, tn=2048)

    pq = norm_matmul([x2], [norm_ffn_g], w_peer_q.astype(BF16), tn=1024)
    keys = peer_sub_keys.reshape(PEER_HEADS * 2, PEER_N_KEYS, -1).astype(BF16)
    s0, e0, s1, e1, tau = peer_select(pq, keys)
    x3 = peer_main(x2, norm_ffn_g, peer_u.astype(BF16), peer_v.astype(BF16), s0, e0, s1, e1, tau)
    return x3.reshape(b, t, d)


def kernel(x, mem, norm_mix_g, w_in, moba_q_norm_g, moba_k_norm_g, moba_out_norm_g, sb_out_norm_g,
           w_out, norm_xattn_g, norm_mem_g, w_xq, w_xkv, xattn_q_norm_g, xattn_k_norm_g, w_xo,
           norm_ffn_g, w_peer_q, peer_sub_keys, peer_u, peer_v):
    depth = w_in.shape[0]
    for l in range(depth):
        x = _layer(x, mem, norm_mix_g[l], w_in[l], moba_q_norm_g[l], moba_k_norm_g[l],
                   moba_out_norm_g[l], sb_out_norm_g[l], w_out[l], norm_xattn_g[l], norm_mem_g[l],
                   w_xq[l], w_xkv[l], xattn_q_norm_g[l], xattn_k_norm_g[l], w_xo[l], norm_ffn_g[l],
                   w_peer_q[l], peer_sub_keys[l], peer_u[l], peer_v[l])
    return x
```

```python
import functools

import numpy as np
import jax
import jax.numpy as jnp
from jax import lax
from jax.experimental import pallas as pl
from jax.experimental.pallas import tpu as pltpu

HEAD_DIM = 128
MOBA_BLOCK = 256
MOBA_TOPK = 3
N_XATTN_HEADS = 4
PEER_HEADS = 8
PEER_N_KEYS = 128
PEER_TOPK = 16
RMS_EPS = 1e-6
NEG_INF = -1e30
EXP_ZERO = -110.0
EXP2_ZERO = -160.0
LOG2E = 1.4426950408889634
NORM_MARGIN = 1.02

LANES = 128
VMEM_LIMIT = 56 * 1024 * 1024

F32 = jnp.float32
BF16 = jnp.bfloat16

_NT = (((1,), (1,)), ((), ()))
_TN = (((0,), (0,)), ((), ()))


def _params(*sem):
    return pltpu.CompilerParams(dimension_semantics=sem, vmem_limit_bytes=VMEM_LIMIT)


def _split_bf16(x):
    hi = x.astype(BF16)
    lo = (x - hi.astype(F32)).astype(BF16)
    return hi, lo


def _norm_matmul_kernel(*refs, k_sizes, has_gain, n_norm_tiles, n_col_tiles, has_res,
                        head_major, row_chunk):
    n_in = len(k_sizes)
    pos = 0
    x_refs = refs[pos:pos + n_in]; pos += n_in
    g_refs = []
    for hg in has_gain:
        if hg:
            g_refs.append(refs[pos]); pos += 1
        else:
            g_refs.append(None)
    w_ref = refs[pos]; pos += 1
    cg_ref = None
    if n_norm_tiles > 0:
        cg_ref = refs[pos]; pos += 1
    res_ref = None
    if has_res:
        res_ref = refs[pos]; pos += 1
    o_ref = refs[pos]; pos += 1
    h_ref = refs[pos]

    j = pl.program_id(1)
    tm = h_ref.shape[0]

    @pl.when(j == 0)
    def _prologue():
        off = 0
        for x_ref, g_ref, ksz in zip(x_refs, g_refs, k_sizes):
            for r0 in range(0, tm, row_chunk):
                x = x_ref[r0:r0 + row_chunk, :]
                if g_ref is not None:
                    ms = jnp.mean(x * x, axis=-1, keepdims=True)
                    x = (x * lax.rsqrt(ms + RMS_EPS)) * g_ref[...]
                h_ref[r0:r0 + row_chunk, off:off + ksz] = x.astype(BF16)
            off += ksz

    acc = jnp.dot(h_ref[...], w_ref[...], preferred_element_type=F32)
    tn = acc.shape[1]

    def finish(get_group):
        for hh in range(tn // LANES):
            y = get_group(hh)
            if res_ref is not None:
                y = y + res_ref[:, hh * LANES:(hh + 1) * LANES]
            if head_major:
                o_ref[0, hh] = y
            else:
                o_ref[:, hh * LANES:(hh + 1) * LANES] = y

    def plain_group(hh):
        return acc[:, hh * LANES:(hh + 1) * LANES]

    def normed_group(hh):
        a = acc[:, hh * LANES:(hh + 1) * LANES]
        ms = jnp.mean(a * a, axis=-1, keepdims=True)
        return (a * lax.rsqrt(ms + RMS_EPS)) * cg_ref[:, hh * LANES:(hh + 1) * LANES]

    if n_norm_tiles == 0:
        finish(plain_group)
    elif n_norm_tiles >= n_col_tiles:
        finish(normed_group)
    else:
        @pl.when(j < n_norm_tiles)
        def _():
            finish(normed_group)

        @pl.when(j >= n_norm_tiles)
        def _():
            finish(plain_group)


def norm_matmul(xs, gains, w, *, col_gain=None, n_norm_cols=0, residual=None,
                head_major_bt=None, tm=512, tn=512):
    m = xs[0].shape[0]
    k_sizes = tuple(int(x.shape[1]) for x in xs)
    k_total = sum(k_sizes)
    n_cols = w.shape[1]
    tm = min(tm, m)
    tn = min(tn, n_cols)
    assert m % tm == 0 and n_cols % tn == 0 and w.shape[0] == k_total
    assert n_norm_cols % tn == 0
    n_col_tiles = n_cols // tn
    n_norm_tiles = n_norm_cols // tn
    has_gain = tuple(g is not None for g in gains)

    args, in_specs = [], []
    for x, ksz in zip(xs, k_sizes):
        args.append(x)
        in_specs.append(pl.BlockSpec((tm, ksz), lambda i, j: (i, 0)))
    for g, ksz in zip(gains, k_sizes):
        if g is not None:
            args.append(g.reshape(1, ksz).astype(F32))
            in_specs.append(pl.BlockSpec((1, ksz), lambda i, j: (0, 0)))
    args.append(w)
    in_specs.append(pl.BlockSpec((k_total, tn), lambda i, j: (0, j)))
    if n_norm_tiles > 0:
        args.append(col_gain.reshape(1, n_cols).astype(F32))
        in_specs.append(pl.BlockSpec((1, tn), lambda i, j: (0, j)))
    if residual is not None:
        args.append(residual)
        in_specs.append(pl.BlockSpec((tm, tn), lambda i, j: (i, j)))

    if head_major_bt is not None:
        b, t = head_major_bt
        assert b * t == m and t % tm == 0
        tiles_per_b = t // tm
        out_shape = jax.ShapeDtypeStruct((b, n_cols // LANES, t, LANES), F32)
        out_spec = pl.BlockSpec((1, tn // LANES, tm, LANES),
                                lambda i, j: (i // tiles_per_b, j, i % tiles_per_b, 0))
    else:
        out_shape = jax.ShapeDtypeStruct((m, n_cols), F32)
        out_spec = pl.BlockSpec((tm, tn), lambda i, j: (i, j))

    kern = functools.partial(
        _norm_matmul_kernel, k_sizes=k_sizes, has_gain=has_gain, n_norm_tiles=n_norm_tiles,
        n_col_tiles=n_col_tiles, has_res=residual is not None,
        head_major=head_major_bt is not None, row_chunk=min(128, tm))
    return pl.pallas_call(
        kern,
        out_shape=out_shape,
        grid=(m // tm, n_col_tiles),
        in_specs=in_specs,
        out_specs=out_spec,
        scratch_shapes=[pltpu.VMEM((tm, k_total), BF16)],
        compiler_params=_params("parallel", "arbitrary"),
    )(*args)


def _select_blocks(kmean, q_tile, tile, nb):
    blk = q_tile.shape[0]
    if tile == 0:
        return jnp.zeros((nb, blk), F32)
    kh, kl = _split_bf16(kmean)
    qh, ql = _split_bf16(q_tile)
    gate = (lax.dot_general(kh, qh, _NT, preferred_element_type=F32)
            + lax.dot_general(kh, ql, _NT, preferred_element_type=F32)
            + lax.dot_general(kl, qh, _NT, preferred_element_type=F32))
    blkid = lax.broadcasted_iota(jnp.int32, (nb, blk), 0)
    valid = blkid < tile
    gate = jnp.where(valid, gate, NEG_INF)
    terms = []
    for mth in range(tile):
        gm = gate[mth:mth + 1, :]
        terms.append(jnp.where(gm > gate, 1.0,
                               jnp.where(gm == gate, jnp.where(blkid > mth, 1.0, 0.0), 0.0)))
    while len(terms) > 1:
        terms = [terms[k] + terms[k + 1] for k in range(0, len(terms) - 1, 2)] + terms[len(terms) & ~1:]
    return jnp.where(valid, jnp.where(terms[0] < MOBA_TOPK, 1.0, 0.0), 0.0)


def _moba_kernel(slopes_ref, first_ref, q_ref, qall_ref, k_ref, v_ref, o_ref,
                 kb_ref, vt_ref, kmean_ref, sel_all_ref, m_ref, l_ref, acc_ref, s_ref,
                 *, nb, blk, scale):
    h = pl.program_id(1)
    qi = pl.program_id(2)

    scale2 = scale * LOG2E
    slope2 = slopes_ref[h] * LOG2E

    @pl.when(qi == 0)
    def _prep():
        for n in range(nb):
            kblk = k_ref[0, 0, n * blk:(n + 1) * blk, :]
            kb_ref[n] = kblk.astype(BF16)
            kmean_ref[n:n + 1, :] = jnp.mean(kblk, axis=0, keepdims=True)
            vt_ref[n] = v_ref[0, 0, n * blk:(n + 1) * blk, :].T.astype(BF16)
        kmean = kmean_ref[...]
        for tile in range(nb):
            sel_all_ref[tile] = _select_blocks(
                kmean, qall_ref[0, 0, tile * blk:(tile + 1) * blk, :], tile, nb)

    q = q_ref[0, 0]
    qb = q.astype(BF16)
    sel_ref = sel_all_ref.at[qi]

    t2 = lax.broadcasted_iota(jnp.int32, (2 * blk, blk), 1)
    j2 = lax.broadcasted_iota(jnp.int32, (2 * blk, blk), 0)
    bias_pair = (t2 - j2).astype(F32) * (-slope2)

    def pair_scores(pair):
        n0 = 2 * pair
        sel0 = sel_ref[pl.ds(n0, 1), :]
        sel1 = sel_ref[pl.ds(n0 + 1, 1), :]
        kpair = jnp.concatenate([kb_ref[n0], kb_ref[n0 + 1]], axis=0)
        s = lax.dot_general(kpair, qb, _NT, preferred_element_type=F32) * scale2 + bias_pair
        return jnp.concatenate([jnp.where(sel0 > 0, s[:blk], NEG_INF),
                                jnp.where(sel1 > 0, s[blk:], NEG_INF)], axis=0)

    def pair_offset(pair):
        return slope2 * ((qi - 2 * pair) * blk).astype(F32)

    last_pair = nb // 2 - 1

    def body(pair, carry):
        n0 = 2 * pair
        s = s_ref[pair % 2]
        s_ref[(pair + 1) % 2] = pair_scores(jnp.minimum(pair + 1, last_pair))
        off = pair_offset(pair)
        m_old = m_ref[...]
        m_new = jnp.maximum(m_old, jnp.max(s, axis=0, keepdims=True) - off)
        alpha = jnp.exp2(m_old - m_new)
        p = jnp.exp2(s - (m_new + off))
        pb = p.astype(BF16)
        l_ref[...] = alpha * l_ref[...] + jnp.sum(p, axis=0, keepdims=True)
        acc_ref[...] = (alpha * acc_ref[...]
                        + jnp.dot(vt_ref[n0], pb[:blk], preferred_element_type=F32)
                        + jnp.dot(vt_ref[n0 + 1], pb[blk:], preferred_element_type=F32))
        m_ref[...] = m_new
        return carry

    first_pair = first_ref[h, qi]
    s_ref[first_pair % 2] = pair_scores(jnp.minimum(first_pair, last_pair))

    t_idx = lax.broadcasted_iota(jnp.int32, (blk, blk), 1)
    j_idx = lax.broadcasted_iota(jnp.int32, (blk, blk), 0)
    dist_own = (t_idx - j_idx).astype(F32)
    s = lax.dot_general(kb_ref[qi], qb, _NT, preferred_element_type=F32) * scale2
    s = s - slope2 * dist_own
    s = jnp.where(dist_own >= 0, s, NEG_INF)
    m0 = jnp.max(s, axis=0, keepdims=True)
    p = jnp.exp2(s - m0)
    m_ref[...] = m0
    l_ref[...] = jnp.sum(p, axis=0, keepdims=True)
    acc_ref[...] = jnp.dot(vt_ref[qi], p.astype(BF16), preferred_element_type=F32)

    lax.fori_loop(first_pair, (qi + 1) // 2, body, 0)
    o_ref[0] = (acc_ref[...] / l_ref[...]).T


def _moba_first_pairs(slopes, qk_norm_bound, scale, nb, blk):
    slope2 = slopes * LOG2E
    room = 2.0 * (qk_norm_bound * NORM_MARGIN * scale * LOG2E) - EXP2_ZERO
    d = jnp.arange(nb, dtype=F32)
    n_live = jnp.sum(d[None, :] * (slope2[:, None] * blk) + slope2[:, None] <= room, axis=1)
    tiles = jnp.arange(nb, dtype=jnp.int32)
    return jnp.maximum(tiles[None, :] - n_live[:, None].astype(jnp.int32), 0) // 2


def moba_attention(qkv, slopes, qk_norm_bound, *, q_head0, k_head0, v_head0, n_heads):
    b, _, t, dh = qkv.shape
    blk = MOBA_BLOCK
    assert t % (2 * blk) == 0
    nb = t // blk
    scale = dh ** -0.5
    first_pairs = _moba_first_pairs(slopes, qk_norm_bound, scale, nb, blk)
    kern = functools.partial(_moba_kernel, nb=nb, blk=blk, scale=scale)
    return pl.pallas_call(
        kern,
        out_shape=jax.ShapeDtypeStruct((b, t, n_heads * dh), F32),
        grid=(b, n_heads, nb),
        in_specs=[
            pl.BlockSpec(memory_space=pltpu.SMEM),
            pl.BlockSpec(memory_space=pltpu.SMEM),
            pl.BlockSpec((1, 1, blk, dh), lambda bi, h, qi: (bi, q_head0 + h, qi, 0)),
            pl.BlockSpec((1, 1, t, dh), lambda bi, h, qi: (bi, q_head0 + h, 0, 0)),
            pl.BlockSpec((1, 1, t, dh), lambda bi, h, qi: (bi, k_head0 + h, 0, 0)),
            pl.BlockSpec((1, 1, t, dh), lambda bi, h, qi: (bi, v_head0 + h, 0, 0)),
        ],
        out_specs=pl.BlockSpec((1, blk, dh), lambda bi, h, qi: (bi, qi, h)),
        scratch_shapes=[
            pltpu.VMEM((nb, blk, dh), BF16),
            pltpu.VMEM((nb, dh, blk), BF16),
            pltpu.VMEM((nb, dh), F32),
            pltpu.VMEM((nb, nb, blk), F32),
            pltpu.VMEM((1, blk), F32),
            pltpu.VMEM((1, blk), F32),
            pltpu.VMEM((dh, blk), F32),
            pltpu.VMEM((2, 2 * blk, blk), F32),
        ],
        compiler_params=_params("parallel", "parallel", "arbitrary"),
    )(slopes, first_pairs, qkv, qkv, qkv, qkv)


def _sb_kernel(q_ref, k_ref, v_ref, o_ref, kb_ref, vb_ref, *, nb, blk, q_tiles, scale):
    step = pl.program_id(2)

    @pl.when(step == 0)
    def _prep():
        for n in range(nb):
            kb_ref[n] = k_ref[0, 0, n * blk:(n + 1) * blk, :].astype(BF16)
            vb_ref[n] = v_ref[0, 0, n * blk:(n + 1) * blk, :].astype(BF16)

    row = lax.broadcasted_iota(jnp.int32, (blk, blk), 0)
    col = lax.broadcasted_iota(jnp.int32, (blk, blk), 1)
    after = jnp.where(row > col, 1.0, 0.0).astype(BF16)
    strict = col < row

    def log_sigmoids(z):
        t = jnp.log(1.0 + jnp.exp(-jnp.abs(z)))
        log_beta = jnp.minimum(z, 0.0) - t
        return log_beta, log_beta - z

    def sum_after(x):
        hi, lo = _split_bf16(x)
        return (jnp.dot(hi, after, preferred_element_type=F32)
                + jnp.dot(lo, after, preferred_element_type=F32))

    def row_sum(x):
        return jnp.sum(x, axis=1, keepdims=True)

    def pair_step(qb, n_late, late_is_own, acc, carry):
        has_early = n_late >= 1
        n_early = jnp.maximum(n_late - 1, 0)
        kpair = jnp.concatenate([kb_ref[n_early], kb_ref[n_late]], axis=0)
        z = lax.dot_general(qb, kpair, _NT, preferred_element_type=F32) * scale
        log_beta, log_om = log_sigmoids(z)
        om_early, om_late = log_om[:, :blk], log_om[:, blk:]
        if late_is_own:
            om_late = jnp.where(strict, om_late, 0.0)
            sum_late = row_sum(om_late)
            a_late = jnp.where(strict, jnp.exp(log_beta[:, blk:] + sum_after(om_late)), 0.0)
            left = sum_late
        else:
            sum_late = row_sum(om_late)
            a_late = jnp.exp(log_beta[:, blk:] + sum_after(om_late) + carry)
            left = carry + sum_late
        a_early = jnp.exp(log_beta[:, :blk] + sum_after(om_early) + left)
        v_early = vb_ref[n_early]
        v_early = jnp.where(has_early, v_early, jnp.zeros_like(v_early))
        contrib = (jnp.dot(a_late.astype(BF16), vb_ref[n_late], preferred_element_type=F32)
                   + jnp.dot(a_early.astype(BF16), v_early, preferred_element_type=F32))
        acc = contrib if acc is None else acc + contrib
        return acc, left + row_sum(om_early)

    tiles = [q_tiles * step + k for k in range(q_tiles)]
    qbs = [q_ref[0, 0, k * blk:(k + 1) * blk, :].astype(BF16) for k in range(q_tiles)]
    firsts = [pair_step(qb, qi, True, None, None) for qb, qi in zip(qbs, tiles)]

    for k, (qb, qi, (acc0, carry0)) in enumerate(zip(qbs, tiles, firsts)):
        n_pairs = qi // 2

        def cond(state, n_pairs=n_pairs):
            pair, _, _, carry_max = state
            return jnp.logical_and(pair < n_pairs, carry_max > EXP_ZERO)

        def body(state, qb=qb, qi=qi):
            pair, acc, carry, _ = state
            acc, carry = pair_step(qb, qi - 2 - 2 * pair, False, acc, carry)
            return pair + 1, acc, carry, jnp.max(carry)

        _, acc, _, _ = lax.while_loop(cond, body, (jnp.int32(0), acc0, carry0, jnp.max(carry0)))
        o_ref[0, k * blk:(k + 1) * blk, :] = acc


def stick_breaking_attention(qkv, *, q_head0, k_head0, v_head0, n_heads, blk=256, q_tiles=2):
    b, _, t, dh = qkv.shape
    assert t % (blk * q_tiles) == 0
    nb = t // blk
    tq = blk * q_tiles
    kern = functools.partial(_sb_kernel, nb=nb, blk=blk, q_tiles=q_tiles, scale=dh ** -0.5)
    return pl.pallas_call(
        kern,
        out_shape=jax.ShapeDtypeStruct((b, t, n_heads * dh), F32),
        grid=(b, n_heads, t // tq),
        in_specs=[
            pl.BlockSpec((1, 1, tq, dh), lambda bi, h, qi: (bi, q_head0 + h, qi, 0)),
            pl.BlockSpec((1, 1, t, dh), lambda bi, h, qi: (bi, k_head0 + h, 0, 0)),
            pl.BlockSpec((1, 1, t, dh), lambda bi, h, qi: (bi, v_head0 + h, 0, 0)),
        ],
        out_specs=pl.BlockSpec((1, tq, dh), lambda bi, h, qi: (bi, qi, h)),
        scratch_shapes=[
            pltpu.VMEM((nb, blk, dh), BF16),
            pltpu.VMEM((nb, blk, dh), BF16),
        ],
        compiler_params=_params("parallel", "parallel", "arbitrary"),
    )(qkv, qkv, qkv)


def _xattn_kernel(q_ref, kv_ref, o_ref, *, n_heads, dh, scale):
    width = n_heads * dh
    for hh in range(n_heads):
        qh = q_ref[0, :, hh * dh:(hh + 1) * dh].astype(BF16)
        kh = kv_ref[0, :, hh * dh:(hh + 1) * dh].astype(BF16)
        vh = kv_ref[0, :, width + hh * dh:width + (hh + 1) * dh].astype(BF16)
        s = lax.dot_general(qh, kh, _NT, preferred_element_type=F32) * scale
        e = jnp.exp(s - jnp.max(s, axis=-1, keepdims=True))
        p = e / jnp.sum(e, axis=-1, keepdims=True)
        o_ref[0, :, hh * dh:(hh + 1) * dh] = jnp.dot(p.astype(BF16), vh, preferred_element_type=F32)


def cross_attention_core(q, kv, *, n_heads, tq=512):
    b, t, width = q.shape
    mlen = kv.shape[1]
    dh = width // n_heads
    tq = min(tq, t)
    kern = functools.partial(_xattn_kernel, n_heads=n_heads, dh=dh, scale=dh ** -0.5)
    return pl.pallas_call(
        kern,
        out_shape=jax.ShapeDtypeStruct((b, t, width), F32),
        grid=(b, t // tq),
        in_specs=[
            pl.BlockSpec((1, tq, width), lambda bi, i: (bi, i, 0)),
            pl.BlockSpec((1, mlen, 2 * width), lambda bi, i: (bi, 0, 0)),
        ],
        out_specs=pl.BlockSpec((1, tq, width), lambda bi, i: (bi, i, 0)),
        compiler_params=_params("parallel", "arbitrary"),
    )(q, kv)


def _extract_topk(work, n_rounds, want_rank=False):
    rows = lax.broadcasted_iota(jnp.int32, work.shape, 0).astype(F32)
    n_rows = float(work.shape[0])
    rank = jnp.full(work.shape, float(n_rounds), F32) if want_rank else None
    vals, firsts = [], []
    for r in range(n_rounds):
        mx = jnp.max(work, axis=0, keepdims=True)
        first = jnp.min(jnp.where(work == mx, rows, n_rows), axis=0, keepdims=True)
        hit = rows == first
        work = jnp.where(hit, -jnp.inf, work)
        if want_rank:
            rank = jnp.where(hit, float(r), rank)
        vals.append(mx)
        firsts.append(first)
    return vals, firsts, work, rank


def _staircase_candidates(vals0, vals1, topk):
    sub = 8
    val1_all = jnp.concatenate(vals1, axis=0)
    val1_head = val1_all[:sub]
    row = lax.broadcasted_iota(jnp.int32, val1_head.shape, 0)
    slabs = [val1_all + vals0[0]]
    for a in range(1, sub):
        b_max = topk // (a + 1) - 1
        slab = val1_head + vals0[a]
        slabs.append(slab if b_max >= sub - 1 else jnp.where(row <= b_max, slab, -jnp.inf))
    slabs.append(jnp.concatenate(vals0[sub:], axis=0) + vals1[0])
    return jnp.concatenate(slabs, axis=0)


def _peer_select_kernel(q_ref, keys_ref, cnt_ref, e0_ref, rank_ref, e1_ref, *, topk, heads):
    for hh in range(heads):
        scores, vals, firsts, left, ranks = [], [], [], [], []
        for p in range(2):
            c0 = (2 * hh + p) * LANES
            qs = q_ref[:, c0:c0 + LANES].astype(BF16)
            s = lax.dot_general(keys_ref[2 * hh + p], qs, _NT, preferred_element_type=F32)
            v, f, rest, rk = _extract_topk(s, topk, want_rank=(p == 1))
            scores.append(s); vals.append(v); firsts.append(f); left.append(rest); ranks.append(rk)
        tops, _, _, _ = _extract_topk(_staircase_candidates(vals[0], vals[1], topk), topk)
        tau = tops[topk - 1]
        z = jnp.zeros_like(tau)
        for tv in tops:
            z = z + jnp.exp(tv - tops[0])
        val1_all = jnp.concatenate(vals[1], axis=0)
        rows = lax.broadcasted_iota(jnp.int32, scores[0].shape, 0).astype(F32)
        count = jnp.zeros(scores[0].shape, F32)
        for a in range(topk):
            n_sel = jnp.sum(jnp.where(val1_all + vals[0][a] >= tau, 1.0, 0.0), axis=0, keepdims=True)
            count = jnp.where(rows == firsts[0][a], n_sel, count)
        cnt_ref[hh] = count
        e0_ref[hh] = jnp.where(left[0] == -jnp.inf, jnp.exp(scores[0] - vals[0][0]), 0.0) / z
        rank_ref[hh] = ranks[1].astype(BF16)
        e1_ref[hh] = jnp.where(left[1] == -jnp.inf, jnp.exp(scores[1] - vals[1][0]), 0.0).astype(BF16)


def peer_select(q, keys_bf16, *, tm=256, heads_per_step=4):
    n = q.shape[0]
    n_heads = keys_bf16.shape[0] // 2
    n_keys = keys_bf16.shape[1]
    tm = min(tm, n)
    hs = heads_per_step
    assert n_heads % hs == 0
    big_f32 = jax.ShapeDtypeStruct((n_heads, n_keys, n), F32)
    big_bf16 = jax.ShapeDtypeStruct((n_heads, n_keys, n), BF16)
    big_spec = pl.BlockSpec((hs, n_keys, tm), lambda i, h: (h, 0, i))
    kern = functools.partial(_peer_select_kernel, topk=PEER_TOPK, heads=hs)
    return pl.pallas_call(
        kern,
        out_shape=(big_f32, big_f32, big_bf16, big_bf16),
        grid=(n // tm, n_heads // hs),
        in_specs=[
            pl.BlockSpec((tm, hs * 2 * LANES), lambda i, h: (i, h)),
            pl.BlockSpec((hs * 2, n_keys, keys_bf16.shape[2]), lambda i, h: (h, 0, 0)),
        ],
        out_specs=(big_spec, big_spec, big_spec, big_spec),
        compiler_params=_params("parallel", "arbitrary"),
    )(q, keys_bf16)


def _peer_main_kernel(x_ref, g_ref, u_ref, v_ref, cnt_ref, e0_ref, rank_ref, e1_ref, o_ref,
                      h_ref, gate_ref, *, n_heads, n_keys, row_chunk):
    e = pl.program_id(1)
    tm = h_ref.shape[0]
    te = u_ref.shape[0]
    groups = te // n_keys

    @pl.when(e == 0)
    def _prologue():
        for r0 in range(0, tm, row_chunk):
            x = x_ref[r0:r0 + row_chunk, :]
            ms = jnp.mean(x * x, axis=-1, keepdims=True)
            h_ref[r0:r0 + row_chunk, :] = ((x * lax.rsqrt(ms + RMS_EPS)) * g_ref[...]).astype(BF16)
            o_ref[r0:r0 + row_chunk, :] = x

    act = lax.dot_general(u_ref[...], h_ref[...], _NT, preferred_element_type=F32)
    for gi in range(groups):
        i = e * groups + gi
        route = jnp.zeros((n_keys, tm), BF16)
        for h in range(n_heads):
            cnt = cnt_ref[h, pl.ds(i, 1), :].astype(BF16)
            w0 = e0_ref[h, pl.ds(i, 1), :].astype(BF16)
            route = route + jnp.where(rank_ref[h] < cnt, e1_ref[h] * w0, jnp.zeros((), BF16))
        a = act[gi * n_keys:(gi + 1) * n_keys, :]
        gelu = 0.5 * a * (1.0 + lax.erf(a * np.float32(np.sqrt(0.5))))
        gate_ref[gi * n_keys:(gi + 1) * n_keys, :] = (gelu * route.astype(F32)).astype(BF16)
    o_ref[...] += lax.dot_general(gate_ref[...], v_ref[...], _TN, preferred_element_type=F32)


def peer_main(x, g, u_bf16, v_bf16, count, e0, rank, e1, *, tm=512, te=512):
    n, d = x.shape
    n_exp = u_bf16.shape[0]
    n_heads, n_keys, _ = count.shape
    tm = min(tm, n)
    assert n % tm == 0 and n_exp % te == 0 and te % n_keys == 0 and tm % LANES == 0
    once = pl.Buffered(1)
    big_spec = pl.BlockSpec((n_heads, n_keys, tm), lambda i, e: (0, 0, i), pipeline_mode=once)
    kern = functools.partial(_peer_main_kernel, n_heads=n_heads, n_keys=n_keys,
                             row_chunk=min(128, tm))
    return pl.pallas_call(
        kern,
        out_shape=jax.ShapeDtypeStruct((n, d), F32),
        grid=(n // tm, n_exp // te),
        in_specs=[
            pl.BlockSpec((tm, d), lambda i, e: (i, 0), pipeline_mode=once),
            pl.BlockSpec((1, d), lambda i, e: (0, 0)),
            pl.BlockSpec((te, d), lambda i, e: (e, 0)),
            pl.BlockSpec((te, d), lambda i, e: (e, 0)),
            big_spec, big_spec, big_spec, big_spec,
        ],
        out_specs=pl.BlockSpec((tm, d), lambda i, e: (i, 0), pipeline_mode=once),
        scratch_shapes=[
            pltpu.VMEM((tm, d), BF16),
            pltpu.VMEM((te, tm), BF16),
        ],
        compiler_params=_params("parallel", "arbitrary"),
    )(x, g.reshape(1, d).astype(F32), u_bf16, v_bf16, count, e0, rank, e1)


def _layer(x, mem, norm_mix_g, w_in, moba_q_norm_g, moba_k_norm_g, moba_out_norm_g,
           sb_out_norm_g, w_out, norm_xattn_g, norm_mem_g, w_xq, w_xkv, xattn_q_norm_g,
           xattn_k_norm_g, w_xo, norm_ffn_g, w_peer_q, peer_sub_keys, peer_u, peer_v):
    b, t, d = x.shape
    n = b * t
    xf = x.reshape(n, d)
    mix_width = w_in.shape[1] // 3
    n_heads = mix_width // (2 * HEAD_DIM)
    grp = n_heads * HEAD_DIM
    slopes = jnp.asarray(2.0 ** (-8.0 * np.arange(1, n_heads + 1) / n_heads), dtype=F32)

    col_gain = jnp.concatenate([jnp.tile(moba_q_norm_g, n_heads), jnp.tile(moba_k_norm_g, n_heads),
                                jnp.ones((w_in.shape[1] - 2 * grp,), F32)])
    qkv = norm_matmul([xf], [norm_mix_g], w_in.astype(BF16), col_gain=col_gain,
                      n_norm_cols=2 * grp, head_major_bt=(b, t), tn=1024)
    qk_norm_bound = HEAD_DIM * jnp.max(jnp.abs(moba_q_norm_g)) * jnp.max(jnp.abs(moba_k_norm_g))
    o_moba = moba_attention(qkv, slopes, qk_norm_bound, q_head0=0, k_head0=n_heads,
                            v_head0=2 * n_heads, n_heads=n_heads)
    o_sb = stick_breaking_attention(qkv, q_head0=3 * n_heads, k_head0=4 * n_heads,
                                    v_head0=5 * n_heads, n_heads=n_heads)
    x1 = norm_matmul([o_moba.reshape(n, grp), o_sb.reshape(n, grp)],
                     [moba_out_norm_g, sb_out_norm_g], w_out.astype(BF16), residual=xf, tn=1024)

    xw = w_xq.shape[1]
    n_mem = mem.shape[1]
    kv_gain = jnp.concatenate([jnp.tile(xattn_k_norm_g, N_XATTN_HEADS), jnp.ones((xw,), F32)])
    kv = norm_matmul([mem.reshape(b * n_mem, d)], [norm_mem_g], w_xkv.astype(BF16),
                     col_gain=kv_gain, n_norm_cols=xw)
    xq = norm_matmul([x1], [norm_xattn_g], w_xq.astype(BF16),
                     col_gain=jnp.tile(xattn_q_norm_g, N_XATTN_HEADS), n_norm_cols=xw)
    o_x = cross_attention_core(xq.reshape(b, t, xw), kv.reshape(b, n_mem, 2 * xw),
                               n_heads=N_XATTN_HEADS)
    x2 = norm_matmul([o_x.reshape(n, xw)], [None], w_xo.astype(BF16), residual=x1, tn=2048)

    pq = norm_matmul([x2], [norm_ffn_g], w_peer_q.astype(BF16), tn=1024)
    keys = peer_sub_keys.reshape(PEER_HEADS * 2, PEER_N_KEYS, -1).astype(BF16)
    count, e0, rank, e1 = peer_select(pq, keys)
    x3 = peer_main(x2, norm_ffn_g, peer_u.astype(BF16), peer_v.astype(BF16), count, e0, rank, e1)
    return x3.reshape(b, t, d)


def kernel(x, mem, norm_mix_g, w_in, moba_q_norm_g, moba_k_norm_g, moba_out_norm_g, sb_out_norm_g,
           w_out, norm_xattn_g, norm_mem_g, w_xq, w_xkv, xattn_q_norm_g, xattn_k_norm_g, w_xo,
           norm_ffn_g, w_peer_q, peer_sub_keys, peer_u, peer_v):
    depth = w_in.shape[0]
    for l in range(depth):
        x = _layer(x, mem, norm_mix_g[l], w_in[l], moba_q_norm_g[l], moba_k_norm_g[l],
                   moba_out_norm_g[l], sb_out_norm_g[l], w_out[l], norm_xattn_g[l], norm_mem_g[l],
                   w_xq[l], w_xkv[l], xattn_q_norm_g[l], xattn_k_norm_g[l], w_xo[l], norm_ffn_g[l],
                   w_peer_q[l], peer_sub_keys[l], peer_u[l], peer_v[l])
    return x
```

```python
import functools

import numpy as np
import jax
import jax.numpy as jnp
from jax import lax
from jax.experimental import pallas as pl
from jax.experimental.pallas import tpu as pltpu

HEAD_DIM = 128
MOBA_BLOCK = 256
MOBA_TOPK = 3
N_XATTN_HEADS = 4
PEER_HEADS = 8
PEER_N_KEYS = 128
PEER_TOPK = 16
RMS_EPS = 1e-6
NEG_INF = -1e30
EXP_ZERO = -110.0
EXP2_ZERO = -160.0
LOG2E = 1.4426950408889634
NORM_MARGIN = 1.02

LANES = 128
VMEM_LIMIT = 56 * 1024 * 1024

F32 = jnp.float32
BF16 = jnp.bfloat16

_NT = (((1,), (1,)), ((), ()))
_TN = (((0,), (0,)), ((), ()))


def _params(*sem):
    return pltpu.CompilerParams(dimension_semantics=sem, vmem_limit_bytes=VMEM_LIMIT)


def _split_bf16(x):
    hi = x.astype(BF16)
    lo = (x - hi.astype(F32)).astype(BF16)
    return hi, lo


def _norm_matmul_kernel(*refs, k_sizes, has_gain, n_norm_tiles, n_col_tiles, has_res,
                        head_major, row_chunk):
    n_in = len(k_sizes)
    pos = 0
    x_refs = refs[pos:pos + n_in]; pos += n_in
    g_refs = []
    for hg in has_gain:
        if hg:
            g_refs.append(refs[pos]); pos += 1
        else:
            g_refs.append(None)
    w_ref = refs[pos]; pos += 1
    cg_ref = None
    if n_norm_tiles > 0:
        cg_ref = refs[pos]; pos += 1
    res_ref = None
    if has_res:
        res_ref = refs[pos]; pos += 1
    o_ref = refs[pos]; pos += 1
    h_ref = refs[pos]

    j = pl.program_id(1)
    tm = h_ref.shape[0]

    @pl.when(j == 0)
    def _prologue():
        off = 0
        for x_ref, g_ref, ksz in zip(x_refs, g_refs, k_sizes):
            for r0 in range(0, tm, row_chunk):
                x = x_ref[r0:r0 + row_chunk, :]
                if g_ref is not None:
                    ms = jnp.mean(x * x, axis=-1, keepdims=True)
                    x = (x * lax.rsqrt(ms + RMS_EPS)) * g_ref[...]
                h_ref[r0:r0 + row_chunk, off:off + ksz] = x.astype(BF16)
            off += ksz

    acc = jnp.dot(h_ref[...], w_ref[...], preferred_element_type=F32)
    tn = acc.shape[1]

    def finish(get_group):
        for hh in range(tn // LANES):
            y = get_group(hh)
            if res_ref is not None:
                y = y + res_ref[:, hh * LANES:(hh + 1) * LANES]
            if head_major:
                o_ref[0, hh] = y
            else:
                o_ref[:, hh * LANES:(hh + 1) * LANES] = y

    def plain_group(hh):
        return acc[:, hh * LANES:(hh + 1) * LANES]

    def normed_group(hh):
        a = acc[:, hh * LANES:(hh + 1) * LANES]
        ms = jnp.mean(a * a, axis=-1, keepdims=True)
        return (a * lax.rsqrt(ms + RMS_EPS)) * cg_ref[:, hh * LANES:(hh + 1) * LANES]

    if n_norm_tiles == 0:
        finish(plain_group)
    elif n_norm_tiles >= n_col_tiles:
        finish(normed_group)
    else:
        @pl.when(j < n_norm_tiles)
        def _():
            finish(normed_group)

        @pl.when(j >= n_norm_tiles)
        def _():
            finish(plain_group)


def norm_matmul(xs, gains, w, *, col_gain=None, n_norm_cols=0, residual=None,
                head_major_bt=None, tm=512, tn=512):
    m = xs[0].shape[0]
    k_sizes = tuple(int(x.shape[1]) for x in xs)
    k_total = sum(k_sizes)
    n_cols = w.shape[1]
    tm = min(tm, m)
    tn = min(tn, n_cols)
    assert m % tm == 0 and n_cols % tn == 0 and w.shape[0] == k_total
    assert n_norm_cols % tn == 0
    n_col_tiles = n_cols // tn
    n_norm_tiles = n_norm_cols // tn
    has_gain = tuple(g is not None for g in gains)

    args, in_specs = [], []
    for x, ksz in zip(xs, k_sizes):
        args.append(x)
        in_specs.append(pl.BlockSpec((tm, ksz), lambda i, j: (i, 0)))
    for g, ksz in zip(gains, k_sizes):
        if g is not None:
            args.append(g.reshape(1, ksz).astype(F32))
            in_specs.append(pl.BlockSpec((1, ksz), lambda i, j: (0, 0)))
    args.append(w)
    in_specs.append(pl.BlockSpec((k_total, tn), lambda i, j: (0, j)))
    if n_norm_tiles > 0:
        args.append(col_gain.reshape(1, n_cols).astype(F32))
        in_specs.append(pl.BlockSpec((1, tn), lambda i, j: (0, j)))
    if residual is not None:
        args.append(residual)
        in_specs.append(pl.BlockSpec((tm, tn), lambda i, j: (i, j)))

    if head_major_bt is not None:
        b, t = head_major_bt
        assert b * t == m and t % tm == 0
        tiles_per_b = t // tm
        out_shape = jax.ShapeDtypeStruct((b, n_cols // LANES, t, LANES), F32)
        out_spec = pl.BlockSpec((1, tn // LANES, tm, LANES),
                                lambda i, j: (i // tiles_per_b, j, i % tiles_per_b, 0))
    else:
        out_shape = jax.ShapeDtypeStruct((m, n_cols), F32)
        out_spec = pl.BlockSpec((tm, tn), lambda i, j: (i, j))

    kern = functools.partial(
        _norm_matmul_kernel, k_sizes=k_sizes, has_gain=has_gain, n_norm_tiles=n_norm_tiles,
        n_col_tiles=n_col_tiles, has_res=residual is not None,
        head_major=head_major_bt is not None, row_chunk=min(128, tm))
    return pl.pallas_call(
        kern,
        out_shape=out_shape,
        grid=(m // tm, n_col_tiles),
        in_specs=in_specs,
        out_specs=out_spec,
        scratch_shapes=[pltpu.VMEM((tm, k_total), BF16)],
        compiler_params=_params("parallel", "arbitrary"),
    )(*args)


def _select_blocks(kmean, q_tile, tile, nb):
    blk = q_tile.shape[0]
    if tile == 0:
        return jnp.zeros((nb, blk), F32)
    kh, kl = _split_bf16(kmean)
    qh, ql = _split_bf16(q_tile)
    gate = (lax.dot_general(kh, qh, _NT, preferred_element_type=F32)
            + lax.dot_general(kh, ql, _NT, preferred_element_type=F32)
            + lax.dot_general(kl, qh, _NT, preferred_element_type=F32))
    blkid = lax.broadcasted_iota(jnp.int32, (nb, blk), 0)
    valid = blkid < tile
    gate = jnp.where(valid, gate, NEG_INF)
    terms = []
    for mth in range(tile):
        gm = gate[mth:mth + 1, :]
        terms.append(jnp.where(gm > gate, 1.0,
                               jnp.where(gm == gate, jnp.where(blkid > mth, 1.0, 0.0), 0.0)))
    while len(terms) > 1:
        terms = [terms[k] + terms[k + 1] for k in range(0, len(terms) - 1, 2)] + terms[len(terms) & ~1:]
    return jnp.where(valid, jnp.where(terms[0] < MOBA_TOPK, 1.0, 0.0), 0.0)


def _moba_kernel(slopes_ref, first_ref, q_ref, qall_ref, k_ref, v_ref, o_ref,
                 kb_ref, vt_ref, kmean_ref, sel_all_ref, m_ref, l_ref, acc_ref, s_ref,
                 *, nb, blk, scale):
    h = pl.program_id(1)
    qi = pl.program_id(2)

    scale2 = scale * LOG2E
    slope2 = slopes_ref[h] * LOG2E

    @pl.when(qi == 0)
    def _prep():
        for n in range(nb):
            kblk = k_ref[0, 0, n * blk:(n + 1) * blk, :]
            kb_ref[n] = kblk.astype(BF16)
            kmean_ref[n:n + 1, :] = jnp.mean(kblk, axis=0, keepdims=True)
            vt_ref[n] = v_ref[0, 0, n * blk:(n + 1) * blk, :].T.astype(BF16)
        kmean = kmean_ref[...]
        for tile in range(nb):
            sel_all_ref[tile] = _select_blocks(
                kmean, qall_ref[0, 0, tile * blk:(tile + 1) * blk, :], tile, nb)

    q = q_ref[0, 0]
    qb = q.astype(BF16)
    sel_ref = sel_all_ref.at[qi]

    t2 = lax.broadcasted_iota(jnp.int32, (2 * blk, blk), 1)
    j2 = lax.broadcasted_iota(jnp.int32, (2 * blk, blk), 0)
    bias_pair = (t2 - j2).astype(F32) * (-slope2)

    def pair_scores(pair):
        n0 = 2 * pair
        sel0 = sel_ref[pl.ds(n0, 1), :]
        sel1 = sel_ref[pl.ds(n0 + 1, 1), :]
        kpair = jnp.concatenate([kb_ref[n0], kb_ref[n0 + 1]], axis=0)
        s = lax.dot_general(kpair, qb, _NT, preferred_element_type=F32) * scale2 + bias_pair
        return jnp.concatenate([jnp.where(sel0 > 0, s[:blk], NEG_INF),
                                jnp.where(sel1 > 0, s[blk:], NEG_INF)], axis=0)

    def pair_offset(pair):
        return slope2 * ((qi - 2 * pair) * blk).astype(F32)

    last_pair = nb // 2 - 1

    def body(pair, carry):
        n0 = 2 * pair
        s = s_ref[pair % 2]
        s_ref[(pair + 1) % 2] = pair_scores(jnp.minimum(pair + 1, last_pair))
        off = pair_offset(pair)
        m_old = m_ref[...]
        m_new = jnp.maximum(m_old, jnp.max(s, axis=0, keepdims=True) - off)
        alpha = jnp.exp2(m_old - m_new)
        p = jnp.exp2(s - (m_new + off))
        pb = p.astype(BF16)
        l_ref[...] = alpha * l_ref[...] + jnp.sum(p, axis=0, keepdims=True)
        acc_ref[...] = (alpha * acc_ref[...]
                        + jnp.dot(vt_ref[n0], pb[:blk], preferred_element_type=F32)
                        + jnp.dot(vt_ref[n0 + 1], pb[blk:], preferred_element_type=F32))
        m_ref[...] = m_new
        return carry

    first_pair = first_ref[h, qi]
    s_ref[first_pair % 2] = pair_scores(jnp.minimum(first_pair, last_pair))

    t_idx = lax.broadcasted_iota(jnp.int32, (blk, blk), 1)
    j_idx = lax.broadcasted_iota(jnp.int32, (blk, blk), 0)
    dist_own = (t_idx - j_idx).astype(F32)
    s = lax.dot_general(kb_ref[qi], qb, _NT, preferred_element_type=F32) * scale2
    s = s - slope2 * dist_own
    s = jnp.where(dist_own >= 0, s, NEG_INF)
    m0 = jnp.max(s, axis=0, keepdims=True)
    p = jnp.exp2(s - m0)
    m_ref[...] = m0
    l_ref[...] = jnp.sum(p, axis=0, keepdims=True)
    acc_ref[...] = jnp.dot(vt_ref[qi], p.astype(BF16), preferred_element_type=F32)

    lax.fori_loop(first_pair, (qi + 1) // 2, body, 0)
    o_ref[0] = (acc_ref[...] / l_ref[...]).T


def _moba_first_pairs(slopes, qk_norm_bound, scale, nb, blk):
    slope2 = slopes * LOG2E
    room = 2.0 * (qk_norm_bound * NORM_MARGIN * scale * LOG2E) - EXP2_ZERO
    d = jnp.arange(nb, dtype=F32)
    n_live = jnp.sum(d[None, :] * (slope2[:, None] * blk) + slope2[:, None] <= room, axis=1)
    tiles = jnp.arange(nb, dtype=jnp.int32)
    return jnp.maximum(tiles[None, :] - n_live[:, None].astype(jnp.int32), 0) // 2


def moba_attention(qkv, slopes, qk_norm_bound, *, q_head0, k_head0, v_head0, n_heads):
    b, _, t, dh = qkv.shape
    blk = MOBA_BLOCK
    assert t % (2 * blk) == 0
    nb = t // blk
    scale = dh ** -0.5
    first_pairs = _moba_first_pairs(slopes, qk_norm_bound, scale, nb, blk)
    kern = functools.partial(_moba_kernel, nb=nb, blk=blk, scale=scale)
    return pl.pallas_call(
        kern,
        out_shape=jax.ShapeDtypeStruct((b, t, n_heads * dh), F32),
        grid=(b, n_heads, nb),
        in_specs=[
            pl.BlockSpec(memory_space=pltpu.SMEM),
            pl.BlockSpec(memory_space=pltpu.SMEM),
            pl.BlockSpec((1, 1, blk, dh), lambda bi, h, qi: (bi, q_head0 + h, qi, 0)),
            pl.BlockSpec((1, 1, t, dh), lambda bi, h, qi: (bi, q_head0 + h, 0, 0)),
            pl.BlockSpec((1, 1, t, dh), lambda bi, h, qi: (bi, k_head0 + h, 0, 0)),
            pl.BlockSpec((1, 1, t, dh), lambda bi, h, qi: (bi, v_head0 + h, 0, 0)),
        ],
        out_specs=pl.BlockSpec((1, blk, dh), lambda bi, h, qi: (bi, qi, h)),
        scratch_shapes=[
            pltpu.VMEM((nb, blk, dh), BF16),
            pltpu.VMEM((nb, dh, blk), BF16),
            pltpu.VMEM((nb, dh), F32),
            pltpu.VMEM((nb, nb, blk), F32),
            pltpu.VMEM((1, blk), F32),
            pltpu.VMEM((1, blk), F32),
            pltpu.VMEM((dh, blk), F32),
            pltpu.VMEM((2, 2 * blk, blk), F32),
        ],
        compiler_params=_params("parallel", "parallel", "arbitrary"),
    )(slopes, first_pairs, qkv, qkv, qkv, qkv)


def _sb_kernel(q_ref, k_ref, v_ref, o_ref, kb_ref, vb_ref, *, nb, blk, q_tiles, scale):
    step = pl.program_id(2)

    @pl.when(step == 0)
    def _prep():
        for n in range(nb):
            kb_ref[n] = k_ref[0, 0, n * blk:(n + 1) * blk, :].astype(BF16)
            vb_ref[n] = v_ref[0, 0, n * blk:(n + 1) * blk, :].astype(BF16)

    row = lax.broadcasted_iota(jnp.int32, (blk, blk), 0)
    col = lax.broadcasted_iota(jnp.int32, (blk, blk), 1)
    after = jnp.where(row > col, 1.0, 0.0).astype(BF16)
    strict = col < row

    def log_sigmoids(z):
        t = jnp.log(1.0 + jnp.exp(-jnp.abs(z)))
        log_beta = jnp.minimum(z, 0.0) - t
        return log_beta, log_beta - z

    def sum_after(x):
        hi, lo = _split_bf16(x)
        return (jnp.dot(hi, after, preferred_element_type=F32)
                + jnp.dot(lo, after, preferred_element_type=F32))

    def row_sum(x):
        return jnp.sum(x, axis=1, keepdims=True)

    def pair_step(qb, n_late, late_is_own, acc, carry):
        has_early = n_late >= 1
        n_early = jnp.maximum(n_late - 1, 0)
        kpair = jnp.concatenate([kb_ref[n_early], kb_ref[n_late]], axis=0)
        z = lax.dot_general(qb, kpair, _NT, preferred_element_type=F32) * scale
        log_beta, log_om = log_sigmoids(z)
        om_early, om_late = log_om[:, :blk], log_om[:, blk:]
        if late_is_own:
            om_late = jnp.where(strict, om_late, 0.0)
            sum_late = row_sum(om_late)
            a_late = jnp.where(strict, jnp.exp(log_beta[:, blk:] + sum_after(om_late)), 0.0)
            left = sum_late
        else:
            sum_late = row_sum(om_late)
            a_late = jnp.exp(log_beta[:, blk:] + sum_after(om_late) + carry)
            left = carry + sum_late
        a_early = jnp.exp(log_beta[:, :blk] + sum_after(om_early) + left)
        v_early = vb_ref[n_early]
        v_early = jnp.where(has_early, v_early, jnp.zeros_like(v_early))
        contrib = (jnp.dot(a_late.astype(BF16), vb_ref[n_late], preferred_element_type=F32)
                   + jnp.dot(a_early.astype(BF16), v_early, preferred_element_type=F32))
        acc = contrib if acc is None else acc + contrib
        return acc, left + row_sum(om_early)

    tiles = [q_tiles * step + k for k in range(q_tiles)]
    qbs = [q_ref[0, 0, k * blk:(k + 1) * blk, :].astype(BF16) for k in range(q_tiles)]
    firsts = [pair_step(qb, qi, True, None, None) for qb, qi in zip(qbs, tiles)]

    for k, (qb, qi, (acc0, carry0)) in enumerate(zip(qbs, tiles, firsts)):
        n_pairs = qi // 2

        def cond(state, n_pairs=n_pairs):
            pair, _, _, carry_max = state
            return jnp.logical_and(pair < n_pairs, carry_max > EXP_ZERO)

        def body(state, qb=qb, qi=qi):
            pair, acc, carry, _ = state
            acc, carry = pair_step(qb, qi - 2 - 2 * pair, False, acc, carry)
            return pair + 1, acc, carry, jnp.max(carry)

        _, acc, _, _ = lax.while_loop(cond, body, (jnp.int32(0), acc0, carry0, jnp.max(carry0)))
        o_ref[0, k * blk:(k + 1) * blk, :] = acc


def stick_breaking_attention(qkv, *, q_head0, k_head0, v_head0, n_heads, blk=256, q_tiles=2):
    b, _, t, dh = qkv.shape
    assert t % (blk * q_tiles) == 0
    nb = t // blk
    tq = blk * q_tiles
    kern = functools.partial(_sb_kernel, nb=nb, blk=blk, q_tiles=q_tiles, scale=dh ** -0.5)
    return pl.pallas_call(
        kern,
        out_shape=jax.ShapeDtypeStruct((b, t, n_heads * dh), F32),
        grid=(b, n_heads, t // tq),
        in_specs=[
            pl.BlockSpec((1, 1, tq, dh), lambda bi, h, qi: (bi, q_head0 + h, qi, 0)),
            pl.BlockSpec((1, 1, t, dh), lambda bi, h, qi: (bi, k_head0 + h, 0, 0)),
            pl.BlockSpec((1, 1, t, dh), lambda bi, h, qi: (bi, v_head0 + h, 0, 0)),
        ],
        out_specs=pl.BlockSpec((1, tq, dh), lambda bi, h, qi: (bi, qi, h)),
        scratch_shapes=[
            pltpu.VMEM((nb, blk, dh), BF16),
            pltpu.VMEM((nb, blk, dh), BF16),
        ],
        compiler_params=_params("parallel", "parallel", "arbitrary"),
    )(qkv, qkv, qkv)


def _xattn_kernel(q_ref, kv_ref, o_ref, *, n_heads, dh, scale):
    width = n_heads * dh
    for hh in range(n_heads):
        qh = q_ref[0, :, hh * dh:(hh + 1) * dh].astype(BF16)
        kh = kv_ref[0, :, hh * dh:(hh + 1) * dh].astype(BF16)
        vh = kv_ref[0, :, width + hh * dh:width + (hh + 1) * dh].astype(BF16)
        s = lax.dot_general(qh, kh, _NT, preferred_element_type=F32) * scale
        e = jnp.exp(s - jnp.max(s, axis=-1, keepdims=True))
        p = e / jnp.sum(e, axis=-1, keepdims=True)
        o_ref[0, :, hh * dh:(hh + 1) * dh] = jnp.dot(p.astype(BF16), vh, preferred_element_type=F32)


def cross_attention_core(q, kv, *, n_heads, tq=512):
    b, t, width = q.shape
    mlen = kv.shape[1]
    dh = width // n_heads
    tq = min(tq, t)
    kern = functools.partial(_xattn_kernel, n_heads=n_heads, dh=dh, scale=dh ** -0.5)
    return pl.pallas_call(
        kern,
        out_shape=jax.ShapeDtypeStruct((b, t, width), F32),
        grid=(b, t // tq),
        in_specs=[
            pl.BlockSpec((1, tq, width), lambda bi, i: (bi, i, 0)),
            pl.BlockSpec((1, mlen, 2 * width), lambda bi, i: (bi, 0, 0)),
        ],
        out_specs=pl.BlockSpec((1, tq, width), lambda bi, i: (bi, i, 0)),
        compiler_params=_params("parallel", "arbitrary"),
    )(q, kv)


def _extract_topk(work, n_rounds, *, exact, want_rank=False, n_masked=0):
    rows = lax.broadcasted_iota(jnp.int32, work.shape, 0).astype(F32)
    n_rows = float(work.shape[0])
    rank = jnp.full(work.shape, float(n_rounds), F32) if want_rank else None
    vals, firsts = [], []
    for r in range(n_rounds):
        mx = jnp.max(work, axis=0, keepdims=True)
        if exact:
            first = jnp.min(jnp.where(work == mx, rows, n_rows), axis=0, keepdims=True)
            hit = rows == first
            firsts.append(first)
        else:
            hit = work == mx
        work = jnp.where(hit, -jnp.inf, work)
        if want_rank:
            rank = jnp.where(hit, float(r), rank)
        vals.append(mx)
    if exact:
        tie = jnp.zeros_like(vals[0])
    else:
        removed = jnp.sum(jnp.where(work == -jnp.inf, 1.0, 0.0), axis=0, keepdims=True)
        tie = jnp.where(removed == float(n_rounds + n_masked), 0.0, 1.0)
    return vals, firsts, work, rank, tie


def _staircase_candidates(vals0, vals1, topk):
    sub = 8
    val1_all = jnp.concatenate(vals1, axis=0)
    val1_head = val1_all[:sub]
    row = lax.broadcasted_iota(jnp.int32, val1_head.shape, 0)
    slabs = [val1_all + vals0[0]]
    n_masked = 0
    for a in range(1, sub):
        b_max = topk // (a + 1) - 1
        slab = val1_head + vals0[a]
        if b_max < sub - 1:
            slab = jnp.where(row <= b_max, slab, -jnp.inf)
            n_masked += sub - 1 - b_max
        slabs.append(slab)
    slabs.append(jnp.concatenate(vals0[sub:], axis=0) + vals1[0])
    return jnp.concatenate(slabs, axis=0), n_masked


def _route_tables(scores0, scores1, topk, exact):
    vals0, firsts0, left0, _, tie0 = _extract_topk(scores0, topk, exact=exact)
    vals1, _, left1, rank1, tie1 = _extract_topk(scores1, topk, exact=exact, want_rank=True)
    cand, n_masked = _staircase_candidates(vals0, vals1, topk)
    tops, firsts2, _, _, tie2 = _extract_topk(cand, topk, exact=exact, n_masked=n_masked)
    tau = tops[topk - 1]
    z = jnp.zeros_like(tau)
    for tv in tops:
        z = z + jnp.exp(tv - tops[0])
    val1_all = jnp.concatenate(vals1, axis=0)
    rows = lax.broadcasted_iota(jnp.int32, scores0.shape, 0).astype(F32)
    count = jnp.zeros(scores0.shape, F32)
    sub = 8
    for a in range(topk):
        if exact:
            lo = 0 if a == 0 else (topk + (a - 1) * sub if a < sub else topk + (sub - 1) * sub + a - sub)
            hi = lo + (topk if a == 0 else (sub if a < sub else 1))
            n_sel = jnp.zeros_like(tau)
            for f in firsts2:
                n_sel = n_sel + jnp.where(f >= float(lo), jnp.where(f < float(hi), 1.0, 0.0), 0.0)
            at_rank_a = rows == firsts0[a]
        else:
            n_sel = jnp.sum(jnp.where(val1_all + vals0[a] >= tau, 1.0, 0.0), axis=0, keepdims=True)
            at_rank_a = scores0 == vals0[a]
        count = jnp.where(at_rank_a, n_sel, count)
    e0 = jnp.where(left0 == -jnp.inf, jnp.exp(scores0 - vals0[0]), 0.0) / z
    e1 = jnp.where(left1 == -jnp.inf, jnp.exp(scores1 - vals1[0]), 0.0)
    tie = jnp.maximum(jnp.maximum(tie0, tie1), tie2)
    return count, e0, rank1.astype(BF16), e1.astype(BF16), tie


def _peer_select_kernel(q_ref, keys_ref, cnt_ref, e0_ref, rank_ref, e1_ref, *, topk, heads):
    def head_scores(hh):
        out = []
        for p in range(2):
            c0 = (2 * hh + p) * LANES
            qs = q_ref[:, c0:c0 + LANES].astype(BF16)
            out.append(lax.dot_general(keys_ref[2 * hh + p], qs, _NT,
                                       preferred_element_type=F32))
        return out

    def write_tables(exact):
        tie = None
        for hh in range(heads):
            s0, s1 = head_scores(hh)
            cnt_ref[hh], e0_ref[hh], rank_ref[hh], e1_ref[hh], t = _route_tables(s0, s1, topk, exact)
            tie = t if tie is None else jnp.maximum(tie, t)
        return tie

    tie = write_tables(exact=False)

    @pl.when(jnp.max(tie) > 0.0)
    def _redo_exact():
        write_tables(exact=True)


def peer_select(q, keys_bf16, *, tm=256, heads_per_step=4):
    n = q.shape[0]
    n_heads = keys_bf16.shape[0] // 2
    n_keys = keys_bf16.shape[1]
    tm = min(tm, n)
    hs = heads_per_step
    assert n_heads % hs == 0
    big_f32 = jax.ShapeDtypeStruct((n_heads, n_keys, n), F32)
    big_bf16 = jax.ShapeDtypeStruct((n_heads, n_keys, n), BF16)
    big_spec = pl.BlockSpec((hs, n_keys, tm), lambda i, h: (h, 0, i))
    kern = functools.partial(_peer_select_kernel, topk=PEER_TOPK, heads=hs)
    return pl.pallas_call(
        kern,
        out_shape=(big_f32, big_f32, big_bf16, big_bf16),
        grid=(n // tm, n_heads // hs),
        in_specs=[
            pl.BlockSpec((tm, hs * 2 * LANES), lambda i, h: (i, h)),
            pl.BlockSpec((hs * 2, n_keys, keys_bf16.shape[2]), lambda i, h: (h, 0, 0)),
        ],
        out_specs=(big_spec, big_spec, big_spec, big_spec),
        compiler_params=_params("parallel", "arbitrary"),
    )(q, keys_bf16)


def _peer_main_kernel(x_ref, g_ref, u_ref, v_ref, cnt_ref, e0_ref, rank_ref, e1_ref, o_ref,
                      h_ref, gate_ref, *, n_heads, n_keys, row_chunk):
    e = pl.program_id(1)
    tm = h_ref.shape[0]
    te = u_ref.shape[0]
    groups = te // n_keys

    @pl.when(e == 0)
    def _prologue():
        for r0 in range(0, tm, row_chunk):
            x = x_ref[r0:r0 + row_chunk, :]
            ms = jnp.mean(x * x, axis=-1, keepdims=True)
            h_ref[r0:r0 + row_chunk, :] = ((x * lax.rsqrt(ms + RMS_EPS)) * g_ref[...]).astype(BF16)
            o_ref[r0:r0 + row_chunk, :] = x

    act = lax.dot_general(u_ref[...], h_ref[...], _NT, preferred_element_type=F32)
    for gi in range(groups):
        i = e * groups + gi
        route = jnp.zeros((n_keys, tm), BF16)
        for h in range(n_heads):
            cnt = cnt_ref[h, pl.ds(i, 1), :].astype(BF16)
            w0 = e0_ref[h, pl.ds(i, 1), :].astype(BF16)
            route = route + jnp.where(rank_ref[h] < cnt, e1_ref[h] * w0, jnp.zeros((), BF16))
        a = act[gi * n_keys:(gi + 1) * n_keys, :]
        gelu = 0.5 * a * (1.0 + lax.erf(a * np.float32(np.sqrt(0.5))))
        gate_ref[gi * n_keys:(gi + 1) * n_keys, :] = (gelu * route.astype(F32)).astype(BF16)
    o_ref[...] += lax.dot_general(gate_ref[...], v_ref[...], _TN, preferred_element_type=F32)


def peer_main(x, g, u_bf16, v_bf16, count, e0, rank, e1, *, tm=512, te=512):
    n, d = x.shape
    n_exp = u_bf16.shape[0]
    n_heads, n_keys, _ = count.shape
    tm = min(tm, n)
    assert n % tm == 0 and n_exp % te == 0 and te % n_keys == 0 and tm % LANES == 0
    once = pl.Buffered(1)
    big_spec = pl.BlockSpec((n_heads, n_keys, tm), lambda i, e: (0, 0, i), pipeline_mode=once)
    kern = functools.partial(_peer_main_kernel, n_heads=n_heads, n_keys=n_keys,
                             row_chunk=min(128, tm))
    return pl.pallas_call(
        kern,
        out_shape=jax.ShapeDtypeStruct((n, d), F32),
        grid=(n // tm, n_exp // te),
        in_specs=[
            pl.BlockSpec((tm, d), lambda i, e: (i, 0), pipeline_mode=once),
            pl.BlockSpec((1, d), lambda i, e: (0, 0)),
            pl.BlockSpec((te, d), lambda i, e: (e, 0)),
            pl.BlockSpec((te, d), lambda i, e: (e, 0)),
            big_spec, big_spec, big_spec, big_spec,
        ],
        out_specs=pl.BlockSpec((tm, d), lambda i, e: (i, 0), pipeline_mode=once),
        scratch_shapes=[
            pltpu.VMEM((tm, d), BF16),
            pltpu.VMEM((te, tm), BF16),
        ],
        compiler_params=_params("parallel", "arbitrary"),
    )(x, g.reshape(1, d).astype(F32), u_bf16, v_bf16, count, e0, rank, e1)


def _layer(x, mem, norm_mix_g, w_in, moba_q_norm_g, moba_k_norm_g, moba_out_norm_g,
           sb_out_norm_g, w_out, norm_xattn_g, norm_mem_g, w_xq, w_xkv, xattn_q_norm_g,
           xattn_k_norm_g, w_xo, norm_ffn_g, w_peer_q, peer_sub_keys, peer_u, peer_v):
    b, t, d = x.shape
    n = b * t
    xf = x.reshape(n, d)
    mix_width = w_in.shape[1] // 3
    n_heads = mix_width // (2 * HEAD_DIM)
    grp = n_heads * HEAD_DIM
    slopes = jnp.asarray(2.0 ** (-8.0 * np.arange(1, n_heads + 1) / n_heads), dtype=F32)

    col_gain = jnp.concatenate([jnp.tile(moba_q_norm_g, n_heads), jnp.tile(moba_k_norm_g, n_heads),
                                jnp.ones((w_in.shape[1] - 2 * grp,), F32)])
    qkv = norm_matmul([xf], [norm_mix_g], w_in.astype(BF16), col_gain=col_gain,
                      n_norm_cols=2 * grp, head_major_bt=(b, t), tn=1024)
    qk_norm_bound = HEAD_DIM * jnp.max(jnp.abs(moba_q_norm_g)) * jnp.max(jnp.abs(moba_k_norm_g))
    o_moba = moba_attention(qkv, slopes, qk_norm_bound, q_head0=0, k_head0=n_heads,
                            v_head0=2 * n_heads, n_heads=n_heads)
    o_sb = stick_breaking_attention(qkv, q_head0=3 * n_heads, k_head0=4 * n_heads,
                                    v_head0=5 * n_heads, n_heads=n_heads)
    x1 = norm_matmul([o_moba.reshape(n, grp), o_sb.reshape(n, grp)],
                     [moba_out_norm_g, sb_out_norm_g], w_out.astype(BF16), residual=xf, tn=1024)

    xw = w_xq.shape[1]
    n_mem = mem.shape[1]
    kv_gain = jnp.concatenate([jnp.tile(xattn_k_norm_g, N_XATTN_HEADS), jnp.ones((xw,), F32)])
    kv = norm_matmul([mem.reshape(b * n_mem, d)], [norm_mem_g], w_xkv.astype(BF16),
                     col_gain=kv_gain, n_norm_cols=xw)
    xq = norm_matmul([x1], [norm_xattn_g], w_xq.astype(BF16),
                     col_gain=jnp.tile(xattn_q_norm_g, N_XATTN_HEADS), n_norm_cols=xw)
    o_x = cross_attention_core(xq.reshape(b, t, xw), kv.reshape(b, n_mem, 2 * xw),
                               n_heads=N_XATTN_HEADS)
    x2 = norm_matmul([o_x.reshape(n, xw)], [None], w_xo.astype(BF16), residual=x1, tn=2048)

    pq = norm_matmul([x2], [norm_ffn_g], w_peer_q.astype(BF16), tn=1024)
    keys = peer_sub_keys.reshape(PEER_HEADS * 2, PEER_N_KEYS, -1).astype(BF16)
    count, e0, rank, e1 = peer_select(pq, keys)
    x3 = peer_main(x2, norm_ffn_g, peer_u.astype(BF16), peer_v.astype(BF16), count, e0, rank, e1)
    return x3.reshape(b, t, d)


def kernel(x, mem, norm_mix_g, w_in, moba_q_norm_g, moba_k_norm_g, moba_out_norm_g, sb_out_norm_g,
           w_out, norm_xattn_g, norm_mem_g, w_xq, w_xkv, xattn_q_norm_g, xattn_k_norm_g, w_xo,
           norm_ffn_g, w_peer_q, peer_sub_keys, peer_u, peer_v):
    depth = w_in.shape[0]
    for l in range(depth):
        x = _layer(x, mem, norm_mix_g[l], w_in[l], moba_q_norm_g[l], moba_k_norm_g[l],
                   moba_out_norm_g[l], sb_out_norm_g[l], w_out[l], norm_xattn_g[l], norm_mem_g[l],
                   w_xq[l], w_xkv[l], xattn_q_norm_g[l], xattn_k_norm_g[l], w_xo[l], norm_ffn_g[l],
                   w_peer_q[l], peer_sub_keys[l], peer_u[l], peer_v[l])
    return x
```

```python
import functools

import numpy as np
import jax
import jax.numpy as jnp
from jax import lax
from jax.experimental import pallas as pl
from jax.experimental.pallas import tpu as pltpu

HEAD_DIM = 128
MOBA_BLOCK = 256
MOBA_TOPK = 3
N_XATTN_HEADS = 4
PEER_HEADS = 8
PEER_N_KEYS = 128
PEER_TOPK = 16
RMS_EPS = 1e-6
NEG_INF = -1e30
EXP_ZERO = -110.0
EXP2_ZERO = -160.0
LOG2E = 1.4426950408889634
NORM_MARGIN = 1.02

LANES = 128
VMEM_LIMIT = 56 * 1024 * 1024

F32 = jnp.float32
BF16 = jnp.bfloat16

_NT = (((1,), (1,)), ((), ()))
_TN = (((0,), (0,)), ((), ()))


def _params(*sem):
    return pltpu.CompilerParams(dimension_semantics=sem, vmem_limit_bytes=VMEM_LIMIT)


def _split_bf16(x):
    hi = x.astype(BF16)
    lo = (x - hi.astype(F32)).astype(BF16)
    return hi, lo


def _norm_matmul_kernel(*refs, k_sizes, has_gain, n_norm_tiles, n_col_tiles, has_res,
                        head_major, row_chunk):
    n_in = len(k_sizes)
    pos = 0
    x_refs = refs[pos:pos + n_in]; pos += n_in
    g_refs = []
    for hg in has_gain:
        if hg:
            g_refs.append(refs[pos]); pos += 1
        else:
            g_refs.append(None)
    w_ref = refs[pos]; pos += 1
    cg_ref = None
    if n_norm_tiles > 0:
        cg_ref = refs[pos]; pos += 1
    res_ref = None
    if has_res:
        res_ref = refs[pos]; pos += 1
    o_ref = refs[pos]; pos += 1
    h_ref = refs[pos]

    j = pl.program_id(1)
    tm = h_ref.shape[0]

    @pl.when(j == 0)
    def _prologue():
        off = 0
        for x_ref, g_ref, ksz in zip(x_refs, g_refs, k_sizes):
            for r0 in range(0, tm, row_chunk):
                x = x_ref[r0:r0 + row_chunk, :]
                if g_ref is not None:
                    ms = jnp.mean(x * x, axis=-1, keepdims=True)
                    x = (x * lax.rsqrt(ms + RMS_EPS)) * g_ref[...]
                h_ref[r0:r0 + row_chunk, off:off + ksz] = x.astype(BF16)
            off += ksz

    acc = jnp.dot(h_ref[...], w_ref[...], preferred_element_type=F32)
    tn = acc.shape[1]

    def finish(get_group):
        for hh in range(tn // LANES):
            y = get_group(hh)
            if res_ref is not None:
                y = y + res_ref[:, hh * LANES:(hh + 1) * LANES]
            if head_major:
                o_ref[0, hh] = y
            else:
                o_ref[:, hh * LANES:(hh + 1) * LANES] = y

    def plain_group(hh):
        return acc[:, hh * LANES:(hh + 1) * LANES]

    def normed_group(hh):
        a = acc[:, hh * LANES:(hh + 1) * LANES]
        ms = jnp.mean(a * a, axis=-1, keepdims=True)
        return (a * lax.rsqrt(ms + RMS_EPS)) * cg_ref[:, hh * LANES:(hh + 1) * LANES]

    if n_norm_tiles == 0:
        finish(plain_group)
    elif n_norm_tiles >= n_col_tiles:
        finish(normed_group)
    else:
        @pl.when(j < n_norm_tiles)
        def _():
            finish(normed_group)

        @pl.when(j >= n_norm_tiles)
        def _():
            finish(plain_group)


def norm_matmul(xs, gains, w, *, col_gain=None, n_norm_cols=0, residual=None,
                head_major_bt=None, tm=512, tn=512):
    m = xs[0].shape[0]
    k_sizes = tuple(int(x.shape[1]) for x in xs)
    k_total = sum(k_sizes)
    n_cols = w.shape[1]
    tm = min(tm, m)
    tn = min(tn, n_cols)
    assert m % tm == 0 and n_cols % tn == 0 and w.shape[0] == k_total
    assert n_norm_cols % tn == 0
    n_col_tiles = n_cols // tn
    n_norm_tiles = n_norm_cols // tn
    has_gain = tuple(g is not None for g in gains)

    args, in_specs = [], []
    for x, ksz in zip(xs, k_sizes):
        args.append(x)
        in_specs.append(pl.BlockSpec((tm, ksz), lambda i, j: (i, 0)))
    for g, ksz in zip(gains, k_sizes):
        if g is not None:
            args.append(g.reshape(1, ksz).astype(F32))
            in_specs.append(pl.BlockSpec((1, ksz), lambda i, j: (0, 0)))
    args.append(w)
    in_specs.append(pl.BlockSpec((k_total, tn), lambda i, j: (0, j)))
    if n_norm_tiles > 0:
        args.append(col_gain.reshape(1, n_cols).astype(F32))
        in_specs.append(pl.BlockSpec((1, tn), lambda i, j: (0, j)))
    if residual is not None:
        args.append(residual)
        in_specs.append(pl.BlockSpec((tm, tn), lambda i, j: (i, j)))

    if head_major_bt is not None:
        b, t = head_major_bt
        assert b * t == m and t % tm == 0
        tiles_per_b = t // tm
        out_shape = jax.ShapeDtypeStruct((b, n_cols // LANES, t, LANES), F32)
        out_spec = pl.BlockSpec((1, tn // LANES, tm, LANES),
                                lambda i, j: (i // tiles_per_b, j, i % tiles_per_b, 0))
    else:
        out_shape = jax.ShapeDtypeStruct((m, n_cols), F32)
        out_spec = pl.BlockSpec((tm, tn), lambda i, j: (i, j))

    kern = functools.partial(
        _norm_matmul_kernel, k_sizes=k_sizes, has_gain=has_gain, n_norm_tiles=n_norm_tiles,
        n_col_tiles=n_col_tiles, has_res=residual is not None,
        head_major=head_major_bt is not None, row_chunk=min(128, tm))
    return pl.pallas_call(
        kern,
        out_shape=out_shape,
        grid=(m // tm, n_col_tiles),
        in_specs=in_specs,
        out_specs=out_spec,
        scratch_shapes=[pltpu.VMEM((tm, k_total), BF16)],
        compiler_params=_params("parallel", "arbitrary"),
    )(*args)


def _select_blocks(kmean, q_tile, tile, nb):
    blk = q_tile.shape[0]
    if tile == 0:
        return jnp.zeros((nb, blk), F32)
    kh, kl = _split_bf16(kmean)
    qh, ql = _split_bf16(q_tile)
    gate = (lax.dot_general(kh, qh, _NT, preferred_element_type=F32)
            + lax.dot_general(kh, ql, _NT, preferred_element_type=F32)
            + lax.dot_general(kl, qh, _NT, preferred_element_type=F32))
    blkid = lax.broadcasted_iota(jnp.int32, (nb, blk), 0)
    valid = blkid < tile
    gate = jnp.where(valid, gate, NEG_INF)
    terms = []
    for mth in range(tile):
        gm = gate[mth:mth + 1, :]
        terms.append(jnp.where(gm > gate, 1.0,
                               jnp.where(gm == gate, jnp.where(blkid > mth, 1.0, 0.0), 0.0)))
    while len(terms) > 1:
        terms = [terms[k] + terms[k + 1] for k in range(0, len(terms) - 1, 2)] + terms[len(terms) & ~1:]
    return jnp.where(valid, jnp.where(terms[0] < MOBA_TOPK, 1.0, 0.0), 0.0)


def _moba_kernel(slopes_ref, first_ref, q_ref, qall_ref, k_ref, v_ref, o_ref,
                 kb_ref, vt_ref, kmean_ref, sel_all_ref, m_ref, l_ref, acc_ref, s_ref,
                 *, nb, blk, q_tiles, scale):
    h = pl.program_id(1)
    step = pl.program_id(2)

    scale2 = scale * LOG2E
    slope2 = slopes_ref[h] * LOG2E

    @pl.when(step == 0)
    def _prep():
        for n in range(nb):
            kblk = k_ref[0, 0, n * blk:(n + 1) * blk, :]
            kb_ref[n] = kblk.astype(BF16)
            kmean_ref[n:n + 1, :] = jnp.mean(kblk, axis=0, keepdims=True)
            vt_ref[n] = v_ref[0, 0, n * blk:(n + 1) * blk, :].T.astype(BF16)
        kmean = kmean_ref[...]
        for tile in range(nb):
            sel_all_ref[tile] = _select_blocks(
                kmean, qall_ref[0, 0, tile * blk:(tile + 1) * blk, :], tile, nb)

    t2 = lax.broadcasted_iota(jnp.int32, (2 * blk, blk), 1)
    j2 = lax.broadcasted_iota(jnp.int32, (2 * blk, blk), 0)
    bias_pair = (t2 - j2).astype(F32) * (-slope2)
    t_idx = lax.broadcasted_iota(jnp.int32, (blk, blk), 1)
    j_idx = lax.broadcasted_iota(jnp.int32, (blk, blk), 0)
    dist_own = (t_idx - j_idx).astype(F32)
    last_pair = nb // 2 - 1

    def tile_stages(k):
        qi = q_tiles * step + k
        qb = q_ref[0, 0, k * blk:(k + 1) * blk, :].astype(BF16)
        sel_ref = sel_all_ref.at[qi]
        m_k, l_k, acc_k, s_k = m_ref.at[k], l_ref.at[k], acc_ref.at[k], s_ref.at[k]
        first_pair = first_ref[h, qi]

        def pair_scores(pair):
            n0 = 2 * pair
            sel0 = sel_ref[pl.ds(n0, 1), :]
            sel1 = sel_ref[pl.ds(n0 + 1, 1), :]
            kpair = jnp.concatenate([kb_ref[n0], kb_ref[n0 + 1]], axis=0)
            s = lax.dot_general(kpair, qb, _NT, preferred_element_type=F32) * scale2 + bias_pair
            return jnp.concatenate([jnp.where(sel0 > 0, s[:blk], NEG_INF),
                                    jnp.where(sel1 > 0, s[blk:], NEG_INF)], axis=0)

        def start():
            s_k[first_pair % 2] = pair_scores(jnp.minimum(first_pair, last_pair))
            s = lax.dot_general(kb_ref[qi], qb, _NT, preferred_element_type=F32) * scale2
            s = s - slope2 * dist_own
            s = jnp.where(dist_own >= 0, s, NEG_INF)
            m0 = jnp.max(s, axis=0, keepdims=True)
            p = jnp.exp2(s - m0)
            m_k[...] = m0
            l_k[...] = jnp.sum(p, axis=0, keepdims=True)
            acc_k[...] = jnp.dot(vt_ref[qi], p.astype(BF16), preferred_element_type=F32)

        def body(pair, carry):
            n0 = 2 * pair
            s = s_k[pair % 2]
            s_k[(pair + 1) % 2] = pair_scores(jnp.minimum(pair + 1, last_pair))
            off = slope2 * ((qi - n0) * blk).astype(F32)
            m_old = m_k[...]
            m_new = jnp.maximum(m_old, jnp.max(s, axis=0, keepdims=True) - off)
            alpha = jnp.exp2(m_old - m_new)
            p = jnp.exp2(s - (m_new + off))
            pb = p.astype(BF16)
            l_k[...] = alpha * l_k[...] + jnp.sum(p, axis=0, keepdims=True)
            acc_k[...] = (alpha * acc_k[...]
                          + jnp.dot(vt_ref[n0], pb[:blk], preferred_element_type=F32)
                          + jnp.dot(vt_ref[n0 + 1], pb[blk:], preferred_element_type=F32))
            m_k[...] = m_new
            return carry

        def sweep():
            lax.fori_loop(first_pair, (qi + 1) // 2, body, 0)

        def finish():
            o_ref[0, k * blk:(k + 1) * blk, :] = (acc_k[...] / l_k[...]).T

        return start, sweep, finish

    stages = [tile_stages(k) for k in range(q_tiles)]
    for stage in range(3):
        for tile in stages:
            tile[stage]()


def _moba_first_pairs(slopes, qk_norm_bound, scale, nb, blk):
    slope2 = slopes * LOG2E
    room = 2.0 * (qk_norm_bound * NORM_MARGIN * scale * LOG2E) - EXP2_ZERO
    d = jnp.arange(nb, dtype=F32)
    n_live = jnp.sum(d[None, :] * (slope2[:, None] * blk) + slope2[:, None] <= room, axis=1)
    tiles = jnp.arange(nb, dtype=jnp.int32)
    return jnp.maximum(tiles[None, :] - n_live[:, None].astype(jnp.int32), 0) // 2


def moba_attention(qkv, slopes, qk_norm_bound, *, q_head0, k_head0, v_head0, n_heads, q_tiles=4):
    b, _, t, dh = qkv.shape
    blk = MOBA_BLOCK
    assert t % (2 * blk) == 0
    nb = t // blk
    assert nb % q_tiles == 0
    tq = q_tiles * blk
    scale = dh ** -0.5
    first_pairs = _moba_first_pairs(slopes, qk_norm_bound, scale, nb, blk)
    kern = functools.partial(_moba_kernel, nb=nb, blk=blk, q_tiles=q_tiles, scale=scale)
    return pl.pallas_call(
        kern,
        out_shape=jax.ShapeDtypeStruct((b, t, n_heads * dh), F32),
        grid=(b, n_heads, nb // q_tiles),
        in_specs=[
            pl.BlockSpec(memory_space=pltpu.SMEM),
            pl.BlockSpec(memory_space=pltpu.SMEM),
            pl.BlockSpec((1, 1, tq, dh), lambda bi, h, qi: (bi, q_head0 + h, qi, 0)),
            pl.BlockSpec((1, 1, t, dh), lambda bi, h, qi: (bi, q_head0 + h, 0, 0)),
            pl.BlockSpec((1, 1, t, dh), lambda bi, h, qi: (bi, k_head0 + h, 0, 0)),
            pl.BlockSpec((1, 1, t, dh), lambda bi, h, qi: (bi, v_head0 + h, 0, 0)),
        ],
        out_specs=pl.BlockSpec((1, tq, dh), lambda bi, h, qi: (bi, qi, h)),
        scratch_shapes=[
            pltpu.VMEM((nb, blk, dh), BF16),
            pltpu.VMEM((nb, dh, blk), BF16),
            pltpu.VMEM((nb, dh), F32),
            pltpu.VMEM((nb, nb, blk), F32),
            pltpu.VMEM((q_tiles, 1, blk), F32),
            pltpu.VMEM((q_tiles, 1, blk), F32),
            pltpu.VMEM((q_tiles, dh, blk), F32),
            pltpu.VMEM((q_tiles, 2, 2 * blk, blk), F32),
        ],
        compiler_params=_params("parallel", "parallel", "arbitrary"),
    )(slopes, first_pairs, qkv, qkv, qkv, qkv)


def _sb_kernel(q_ref, k_ref, v_ref, o_ref, kb_ref, vb_ref, *, nb, blk, q_tiles, scale):
    step = pl.program_id(2)

    @pl.when(step == 0)
    def _prep():
        for n in range(nb):
            kb_ref[n] = k_ref[0, 0, n * blk:(n + 1) * blk, :].astype(BF16)
            vb_ref[n] = v_ref[0, 0, n * blk:(n + 1) * blk, :].astype(BF16)

    row = lax.broadcasted_iota(jnp.int32, (blk, blk), 0)
    col = lax.broadcasted_iota(jnp.int32, (blk, blk), 1)
    after = jnp.where(row > col, 1.0, 0.0).astype(BF16)
    strict = col < row

    def log_sigmoids(z):
        t = jnp.log(1.0 + jnp.exp(-jnp.abs(z)))
        log_beta = jnp.minimum(z, 0.0) - t
        return log_beta, log_beta - z

    def sum_after(x):
        hi, lo = _split_bf16(x)
        return (jnp.dot(hi, after, preferred_element_type=F32)
                + jnp.dot(lo, after, preferred_element_type=F32))

    def row_sum(x):
        return jnp.sum(x, axis=1, keepdims=True)

    def pair_step(qb, n_late, late_is_own, acc, carry):
        has_early = n_late >= 1
        n_early = jnp.maximum(n_late - 1, 0)
        kpair = jnp.concatenate([kb_ref[n_early], kb_ref[n_late]], axis=0)
        z = lax.dot_general(qb, kpair, _NT, preferred_element_type=F32) * scale
        log_beta, log_om = log_sigmoids(z)
        om_early, om_late = log_om[:, :blk], log_om[:, blk:]
        if late_is_own:
            om_late = jnp.where(strict, om_late, 0.0)
            sum_late = row_sum(om_late)
            a_late = jnp.where(strict, jnp.exp(log_beta[:, blk:] + sum_after(om_late)), 0.0)
            left = sum_late
        else:
            sum_late = row_sum(om_late)
            a_late = jnp.exp(log_beta[:, blk:] + sum_after(om_late) + carry)
            left = carry + sum_late
        a_early = jnp.exp(log_beta[:, :blk] + sum_after(om_early) + left)
        v_early = vb_ref[n_early]
        v_early = jnp.where(has_early, v_early, jnp.zeros_like(v_early))
        contrib = (jnp.dot(a_late.astype(BF16), vb_ref[n_late], preferred_element_type=F32)
                   + jnp.dot(a_early.astype(BF16), v_early, preferred_element_type=F32))
        acc = contrib if acc is None else acc + contrib
        return acc, left + row_sum(om_early)

    tiles = [q_tiles * step + k for k in range(q_tiles)]
    qbs = [q_ref[0, 0, k * blk:(k + 1) * blk, :].astype(BF16) for k in range(q_tiles)]
    firsts = [pair_step(qb, qi, True, None, None) for qb, qi in zip(qbs, tiles)]

    for k, (qb, qi, (acc0, carry0)) in enumerate(zip(qbs, tiles, firsts)):
        n_pairs = qi // 2

        def cond(state, n_pairs=n_pairs):
            pair, _, _, carry_max = state
            return jnp.logical_and(pair < n_pairs, carry_max > EXP_ZERO)

        def body(state, qb=qb, qi=qi):
            pair, acc, carry, _ = state
            acc, carry = pair_step(qb, qi - 2 - 2 * pair, False, acc, carry)
            return pair + 1, acc, carry, jnp.max(carry)

        _, acc, _, _ = lax.while_loop(cond, body, (jnp.int32(0), acc0, carry0, jnp.max(carry0)))
        o_ref[0, k * blk:(k + 1) * blk, :] = acc


def stick_breaking_attention(qkv, *, q_head0, k_head0, v_head0, n_heads, blk=256, q_tiles=2):
    b, _, t, dh = qkv.shape
    assert t % (blk * q_tiles) == 0
    nb = t // blk
    tq = blk * q_tiles
    kern = functools.partial(_sb_kernel, nb=nb, blk=blk, q_tiles=q_tiles, scale=dh ** -0.5)
    return pl.pallas_call(
        kern,
        out_shape=jax.ShapeDtypeStruct((b, t, n_heads * dh), F32),
        grid=(b, n_heads, t // tq),
        in_specs=[
            pl.BlockSpec((1, 1, tq, dh), lambda bi, h, qi: (bi, q_head0 + h, qi, 0)),
            pl.BlockSpec((1, 1, t, dh), lambda bi, h, qi: (bi, k_head0 + h, 0, 0)),
            pl.BlockSpec((1, 1, t, dh), lambda bi, h, qi: (bi, v_head0 + h, 0, 0)),
        ],
        out_specs=pl.BlockSpec((1, tq, dh), lambda bi, h, qi: (bi, qi, h)),
        scratch_shapes=[
            pltpu.VMEM((nb, blk, dh), BF16),
            pltpu.VMEM((nb, blk, dh), BF16),
        ],
        compiler_params=_params("parallel", "parallel", "arbitrary"),
    )(qkv, qkv, qkv)


def _xattn_kernel(q_ref, kv_ref, o_ref, *, n_heads, dh, scale):
    width = n_heads * dh
    for hh in range(n_heads):
        qh = q_ref[0, :, hh * dh:(hh + 1) * dh].astype(BF16)
        kh = kv_ref[0, :, hh * dh:(hh + 1) * dh].astype(BF16)
        vh = kv_ref[0, :, width + hh * dh:width + (hh + 1) * dh].astype(BF16)
        s = lax.dot_general(qh, kh, _NT, preferred_element_type=F32) * scale
        e = jnp.exp(s - jnp.max(s, axis=-1, keepdims=True))
        p = e / jnp.sum(e, axis=-1, keepdims=True)
        o_ref[0, :, hh * dh:(hh + 1) * dh] = jnp.dot(p.astype(BF16), vh, preferred_element_type=F32)


def cross_attention_core(q, kv, *, n_heads, tq=512):
    b, t, width = q.shape
    mlen = kv.shape[1]
    dh = width // n_heads
    tq = min(tq, t)
    kern = functools.partial(_xattn_kernel, n_heads=n_heads, dh=dh, scale=dh ** -0.5)
    return pl.pallas_call(
        kern,
        out_shape=jax.ShapeDtypeStruct((b, t, width), F32),
        grid=(b, t // tq),
        in_specs=[
            pl.BlockSpec((1, tq, width), lambda bi, i: (bi, i, 0)),
            pl.BlockSpec((1, mlen, 2 * width), lambda bi, i: (bi, 0, 0)),
        ],
        out_specs=pl.BlockSpec((1, tq, width), lambda bi, i: (bi, i, 0)),
        compiler_params=_params("parallel", "arbitrary"),
    )(q, kv)


def _extract_topk(work, n_rounds, *, exact, want_rank=False, n_masked=0):
    rows = lax.broadcasted_iota(jnp.int32, work.shape, 0).astype(F32)
    n_rows = float(work.shape[0])
    rank = jnp.full(work.shape, float(n_rounds), F32) if want_rank else None
    vals, firsts = [], []
    for r in range(n_rounds):
        mx = jnp.max(work, axis=0, keepdims=True)
        if exact:
            first = jnp.min(jnp.where(work == mx, rows, n_rows), axis=0, keepdims=True)
            hit = rows == first
            firsts.append(first)
        else:
            hit = work == mx
        work = jnp.where(hit, -jnp.inf, work)
        if want_rank:
            rank = jnp.where(hit, float(r), rank)
        vals.append(mx)
    if exact:
        tie = jnp.zeros_like(vals[0])
    else:
        removed = jnp.sum(jnp.where(work == -jnp.inf, 1.0, 0.0), axis=0, keepdims=True)
        tie = jnp.where(removed == float(n_rounds + n_masked), 0.0, 1.0)
    return vals, firsts, work, rank, tie


def _staircase_candidates(vals0, vals1, topk):
    sub = 8
    val1_all = jnp.concatenate(vals1, axis=0)
    val1_head = val1_all[:sub]
    row = lax.broadcasted_iota(jnp.int32, val1_head.shape, 0)
    slabs = [val1_all + vals0[0]]
    n_masked = 0
    for a in range(1, sub):
        b_max = topk // (a + 1) - 1
        slab = val1_head + vals0[a]
        if b_max < sub - 1:
            slab = jnp.where(row <= b_max, slab, -jnp.inf)
            n_masked += sub - 1 - b_max
        slabs.append(slab)
    slabs.append(jnp.concatenate(vals0[sub:], axis=0) + vals1[0])
    return jnp.concatenate(slabs, axis=0), n_masked


def _route_tables(scores0, scores1, topk, exact):
    vals0, firsts0, left0, _, tie0 = _extract_topk(scores0, topk, exact=exact)
    vals1, _, left1, rank1, tie1 = _extract_topk(scores1, topk, exact=exact, want_rank=True)
    cand, n_masked = _staircase_candidates(vals0, vals1, topk)
    tops, firsts2, _, _, tie2 = _extract_topk(cand, topk, exact=exact, n_masked=n_masked)
    tau = tops[topk - 1]
    z = jnp.zeros_like(tau)
    for tv in tops:
        z = z + jnp.exp(tv - tops[0])
    val1_all = jnp.concatenate(vals1, axis=0)
    rows = lax.broadcasted_iota(jnp.int32, scores0.shape, 0).astype(F32)
    count = jnp.zeros(scores0.shape, F32)
    sub = 8
    for a in range(topk):
        if exact:
            lo = 0 if a == 0 else (topk + (a - 1) * sub if a < sub else topk + (sub - 1) * sub + a - sub)
            hi = lo + (topk if a == 0 else (sub if a < sub else 1))
            n_sel = jnp.zeros_like(tau)
            for f in firsts2:
                n_sel = n_sel + jnp.where(f >= float(lo), jnp.where(f < float(hi), 1.0, 0.0), 0.0)
            at_rank_a = rows == firsts0[a]
        else:
            n_sel = jnp.sum(jnp.where(val1_all + vals0[a] >= tau, 1.0, 0.0), axis=0, keepdims=True)
            at_rank_a = scores0 == vals0[a]
        count = jnp.where(at_rank_a, n_sel, count)
    e0 = jnp.where(left0 == -jnp.inf, jnp.exp(scores0 - vals0[0]), 0.0) / z
    e1 = jnp.where(left1 == -jnp.inf, jnp.exp(scores1 - vals1[0]), 0.0)
    tie = jnp.maximum(jnp.maximum(tie0, tie1), tie2)
    return count, e0, rank1.astype(BF16), e1.astype(BF16), tie


def _peer_select_kernel(q_ref, keys_ref, cnt_ref, e0_ref, rank_ref, e1_ref, *, topk, heads):
    def head_scores(hh):
        out = []
        for p in range(2):
            c0 = (2 * hh + p) * LANES
            qs = q_ref[:, c0:c0 + LANES].astype(BF16)
            out.append(lax.dot_general(keys_ref[2 * hh + p], qs, _NT,
                                       preferred_element_type=F32))
        return out

    def write_tables(exact):
        tie = None
        for hh in range(heads):
            s0, s1 = head_scores(hh)
            cnt_ref[hh], e0_ref[hh], rank_ref[hh], e1_ref[hh], t = _route_tables(s0, s1, topk, exact)
            tie = t if tie is None else jnp.maximum(tie, t)
        return tie

    tie = write_tables(exact=False)

    @pl.when(jnp.max(tie) > 0.0)
    def _redo_exact():
        write_tables(exact=True)


def peer_select(q, keys_bf16, *, tm=256, heads_per_step=4):
    n = q.shape[0]
    n_heads = keys_bf16.shape[0] // 2
    n_keys = keys_bf16.shape[1]
    tm = min(tm, n)
    hs = heads_per_step
    assert n_heads % hs == 0
    big_f32 = jax.ShapeDtypeStruct((n_heads, n_keys, n), F32)
    big_bf16 = jax.ShapeDtypeStruct((n_heads, n_keys, n), BF16)
    big_spec = pl.BlockSpec((hs, n_keys, tm), lambda i, h: (h, 0, i))
    kern = functools.partial(_peer_select_kernel, topk=PEER_TOPK, heads=hs)
    return pl.pallas_call(
        kern,
        out_shape=(big_f32, big_f32, big_bf16, big_bf16),
        grid=(n // tm, n_heads // hs),
        in_specs=[
            pl.BlockSpec((tm, hs * 2 * LANES), lambda i, h: (i, h)),
            pl.BlockSpec((hs * 2, n_keys, keys_bf16.shape[2]), lambda i, h: (h, 0, 0)),
        ],
        out_specs=(big_spec, big_spec, big_spec, big_spec),
        compiler_params=_params("parallel", "arbitrary"),
    )(q, keys_bf16)


def _peer_main_kernel(x_ref, g_ref, u_ref, v_ref, cnt_ref, e0_ref, rank_ref, e1_ref, o_ref,
                      h_ref, gate_ref, *, n_heads, n_keys, row_chunk):
    e = pl.program_id(1)
    tm = h_ref.shape[0]
    te = u_ref.shape[0]
    groups = te // n_keys

    @pl.when(e == 0)
    def _prologue():
        for r0 in range(0, tm, row_chunk):
            x = x_ref[r0:r0 + row_chunk, :]
            ms = jnp.mean(x * x, axis=-1, keepdims=True)
            h_ref[r0:r0 + row_chunk, :] = ((x * lax.rsqrt(ms + RMS_EPS)) * g_ref[...]).astype(BF16)
            o_ref[r0:r0 + row_chunk, :] = x

    act = lax.dot_general(u_ref[...], h_ref[...], _NT, preferred_element_type=F32)
    for gi in range(groups):
        i = e * groups + gi
        route = jnp.zeros((n_keys, tm), BF16)
        for h in range(n_heads):
            cnt = cnt_ref[h, pl.ds(i, 1), :].astype(BF16)
            w0 = e0_ref[h, pl.ds(i, 1), :].astype(BF16)
            route = route + jnp.where(rank_ref[h] < cnt, e1_ref[h] * w0, jnp.zeros((), BF16))
        a = act[gi * n_keys:(gi + 1) * n_keys, :]
        gelu = 0.5 * a * (1.0 + lax.erf(a * np.float32(np.sqrt(0.5))))
        gate_ref[gi * n_keys:(gi + 1) * n_keys, :] = (gelu * route.astype(F32)).astype(BF16)
    o_ref[...] += lax.dot_general(gate_ref[...], v_ref[...], _TN, preferred_element_type=F32)


def peer_main(x, g, u_bf16, v_bf16, count, e0, rank, e1, *, tm=512, te=512):
    n, d = x.shape
    n_exp = u_bf16.shape[0]
    n_heads, n_keys, _ = count.shape
    tm = min(tm, n)
    assert n % tm == 0 and n_exp % te == 0 and te % n_keys == 0 and tm % LANES == 0
    once = pl.Buffered(1)
    big_spec = pl.BlockSpec((n_heads, n_keys, tm), lambda i, e: (0, 0, i), pipeline_mode=once)
    kern = functools.partial(_peer_main_kernel, n_heads=n_heads, n_keys=n_keys,
                             row_chunk=min(128, tm))
    return pl.pallas_call(
        kern,
        out_shape=jax.ShapeDtypeStruct((n, d), F32),
        grid=(n // tm, n_exp // te),
        in_specs=[
            pl.BlockSpec((tm, d), lambda i, e: (i, 0), pipeline_mode=once),
            pl.BlockSpec((1, d), lambda i, e: (0, 0)),
            pl.BlockSpec((te, d), lambda i, e: (e, 0)),
            pl.BlockSpec((te, d), lambda i, e: (e, 0)),
            big_spec, big_spec, big_spec, big_spec,
        ],
        out_specs=pl.BlockSpec((tm, d), lambda i, e: (i, 0), pipeline_mode=once),
        scratch_shapes=[
            pltpu.VMEM((tm, d), BF16),
            pltpu.VMEM((te, tm), BF16),
        ],
        compiler_params=_params("parallel", "arbitrary"),
    )(x, g.reshape(1, d).astype(F32), u_bf16, v_bf16, count, e0, rank, e1)


def _layer(x, mem, norm_mix_g, w_in, moba_q_norm_g, moba_k_norm_g, moba_out_norm_g,
           sb_out_norm_g, w_out, norm_xattn_g, norm_mem_g, w_xq, w_xkv, xattn_q_norm_g,
           xattn_k_norm_g, w_xo, norm_ffn_g, w_peer_q, peer_sub_keys, peer_u, peer_v):
    b, t, d = x.shape
    n = b * t
    xf = x.reshape(n, d)
    mix_width = w_in.shape[1] // 3
    n_heads = mix_width // (2 * HEAD_DIM)
    grp = n_heads * HEAD_DIM
    slopes = jnp.asarray(2.0 ** (-8.0 * np.arange(1, n_heads + 1) / n_heads), dtype=F32)

    col_gain = jnp.concatenate([jnp.tile(moba_q_norm_g, n_heads), jnp.tile(moba_k_norm_g, n_heads),
                                jnp.ones((w_in.shape[1] - 2 * grp,), F32)])
    qkv = norm_matmul([xf], [norm_mix_g], w_in.astype(BF16), col_gain=col_gain,
                      n_norm_cols=2 * grp, head_major_bt=(b, t), tn=1024)
    qk_norm_bound = HEAD_DIM * jnp.max(jnp.abs(moba_q_norm_g)) * jnp.max(jnp.abs(moba_k_norm_g))
    o_moba = moba_attention(qkv, slopes, qk_norm_bound, q_head0=0, k_head0=n_heads,
                            v_head0=2 * n_heads, n_heads=n_heads)
    o_sb = stick_breaking_attention(qkv, q_head0=3 * n_heads, k_head0=4 * n_heads,
                                    v_head0=5 * n_heads, n_heads=n_heads)
    x1 = norm_matmul([o_moba.reshape(n, grp), o_sb.reshape(n, grp)],
                     [moba_out_norm_g, sb_out_norm_g], w_out.astype(BF16), residual=xf, tn=1024)

    xw = w_xq.shape[1]
    n_mem = mem.shape[1]
    kv_gain = jnp.concatenate([jnp.tile(xattn_k_norm_g, N_XATTN_HEADS), jnp.ones((xw,), F32)])
    kv = norm_matmul([mem.reshape(b * n_mem, d)], [norm_mem_g], w_xkv.astype(BF16),
                     col_gain=kv_gain, n_norm_cols=xw)
    xq = norm_matmul([x1], [norm_xattn_g], w_xq.astype(BF16),
                     col_gain=jnp.tile(xattn_q_norm_g, N_XATTN_HEADS), n_norm_cols=xw)
    o_x = cross_attention_core(xq.reshape(b, t, xw), kv.reshape(b, n_mem, 2 * xw),
                               n_heads=N_XATTN_HEADS)
    x2 = norm_matmul([o_x.reshape(n, xw)], [None], w_xo.astype(BF16), residual=x1, tn=2048)

    pq = norm_matmul([x2], [norm_ffn_g], w_peer_q.astype(BF16), tn=1024)
    keys = peer_sub_keys.reshape(PEER_HEADS * 2, PEER_N_KEYS, -1).astype(BF16)
    count, e0, rank, e1 = peer_select(pq, keys)
    x3 = peer_main(x2, norm_ffn_g, peer_u.astype(BF16), peer_v.astype(BF16), count, e0, rank, e1)
    return x3.reshape(b, t, d)


def kernel(x, mem, norm_mix_g, w_in, moba_q_norm_g, moba_k_norm_g, moba_out_norm_g, sb_out_norm_g,
           w_out, norm_xattn_g, norm_mem_g, w_xq, w_xkv, xattn_q_norm_g, xattn_k_norm_g, w_xo,
           norm_ffn_g, w_peer_q, peer_sub_keys, peer_u, peer_v):
    depth = w_in.shape[0]
    for l in range(depth):
        x = _layer(x, mem, norm_mix_g[l], w_in[l], moba_q_norm_g[l], moba_k_norm_g[l],
                   moba_out_norm_g[l], sb_out_norm_g[l], w_out[l], norm_xattn_g[l], norm_mem_g[l],
                   w_xq[l], w_xkv[l], xattn_q_norm_g[l], xattn_k_norm_g[l], w_xo[l], norm_ffn_g[l],
                   w_peer_q[l], peer_sub_keys[l], peer_u[l], peer_v[l])
    return x
```

```python
import functools

import numpy as np
import jax
import jax.numpy as jnp
from jax import lax
from jax.experimental import pallas as pl
from jax.experimental.pallas import tpu as pltpu

HEAD_DIM = 128
MOBA_BLOCK = 256
MOBA_TOPK = 3
N_XATTN_HEADS = 4
PEER_HEADS = 8
PEER_N_KEYS = 128
PEER_TOPK = 16
RMS_EPS = 1e-6
NEG_INF = -1e30
EXP_ZERO = -110.0
EXP2_ZERO = -160.0
LOG2E = 1.4426950408889634
NORM_MARGIN = 1.02

LANES = 128
VMEM_LIMIT = 56 * 1024 * 1024

F32 = jnp.float32
BF16 = jnp.bfloat16

_NT = (((1,), (1,)), ((), ()))
_TN = (((0,), (0,)), ((), ()))


def _params(*sem):
    return pltpu.CompilerParams(dimension_semantics=sem, vmem_limit_bytes=VMEM_LIMIT)


def _split_bf16(x):
    hi = x.astype(BF16)
    lo = (x - hi.astype(F32)).astype(BF16)
    return hi, lo


def _norm_matmul_kernel(*refs, k_sizes, has_gain, n_norm_tiles, n_col_tiles, has_res,
                        head_major, row_chunk):
    n_in = len(k_sizes)
    pos = 0
    x_refs = refs[pos:pos + n_in]; pos += n_in
    g_refs = []
    for hg in has_gain:
        if hg:
            g_refs.append(refs[pos]); pos += 1
        else:
            g_refs.append(None)
    w_ref = refs[pos]; pos += 1
    cg_ref = None
    if n_norm_tiles > 0:
        cg_ref = refs[pos]; pos += 1
    res_ref = None
    if has_res:
        res_ref = refs[pos]; pos += 1
    o_ref = refs[pos]; pos += 1
    h_ref = refs[pos]

    j = pl.program_id(1)
    tm = h_ref.shape[0]

    @pl.when(j == 0)
    def _prologue():
        off = 0
        for x_ref, g_ref, ksz in zip(x_refs, g_refs, k_sizes):
            for r0 in range(0, tm, row_chunk):
                x = x_ref[r0:r0 + row_chunk, :]
                if g_ref is not None:
                    ms = jnp.mean(x * x, axis=-1, keepdims=True)
                    x = (x * lax.rsqrt(ms + RMS_EPS)) * g_ref[...]
                h_ref[r0:r0 + row_chunk, off:off + ksz] = x.astype(BF16)
            off += ksz

    acc = jnp.dot(h_ref[...], w_ref[...], preferred_element_type=F32)
    tn = acc.shape[1]

    def finish(get_group):
        for hh in range(tn // LANES):
            y = get_group(hh)
            if res_ref is not None:
                y = y + res_ref[:, hh * LANES:(hh + 1) * LANES]
            if head_major:
                o_ref[0, hh] = y
            else:
                o_ref[:, hh * LANES:(hh + 1) * LANES] = y

    def plain_group(hh):
        return acc[:, hh * LANES:(hh + 1) * LANES]

    def normed_group(hh):
        a = acc[:, hh * LANES:(hh + 1) * LANES]
        ms = jnp.mean(a * a, axis=-1, keepdims=True)
        return (a * lax.rsqrt(ms + RMS_EPS)) * cg_ref[:, hh * LANES:(hh + 1) * LANES]

    if n_norm_tiles == 0:
        finish(plain_group)
    elif n_norm_tiles >= n_col_tiles:
        finish(normed_group)
    else:
        @pl.when(j < n_norm_tiles)
        def _():
            finish(normed_group)

        @pl.when(j >= n_norm_tiles)
        def _():
            finish(plain_group)


def norm_matmul(xs, gains, w, *, col_gain=None, n_norm_cols=0, residual=None,
                head_major_bt=None, tm=512, tn=512):
    m = xs[0].shape[0]
    k_sizes = tuple(int(x.shape[1]) for x in xs)
    k_total = sum(k_sizes)
    n_cols = w.shape[1]
    tm = min(tm, m)
    tn = min(tn, n_cols)
    assert m % tm == 0 and n_cols % tn == 0 and w.shape[0] == k_total
    assert n_norm_cols % tn == 0
    n_col_tiles = n_cols // tn
    n_norm_tiles = n_norm_cols // tn
    has_gain = tuple(g is not None for g in gains)

    args, in_specs = [], []
    for x, ksz in zip(xs, k_sizes):
        args.append(x)
        in_specs.append(pl.BlockSpec((tm, ksz), lambda i, j: (i, 0)))
    for g, ksz in zip(gains, k_sizes):
        if g is not None:
            args.append(g.reshape(1, ksz).astype(F32))
            in_specs.append(pl.BlockSpec((1, ksz), lambda i, j: (0, 0)))
    args.append(w)
    in_specs.append(pl.BlockSpec((k_total, tn), lambda i, j: (0, j)))
    if n_norm_tiles > 0:
        args.append(col_gain.reshape(1, n_cols).astype(F32))
        in_specs.append(pl.BlockSpec((1, tn), lambda i, j: (0, j)))
    if residual is not None:
        args.append(residual)
        in_specs.append(pl.BlockSpec((tm, tn), lambda i, j: (i, j)))

    if head_major_bt is not None:
        b, t = head_major_bt
        assert b * t == m and t % tm == 0
        tiles_per_b = t // tm
        out_shape = jax.ShapeDtypeStruct((b, n_cols // LANES, t, LANES), F32)
        out_spec = pl.BlockSpec((1, tn // LANES, tm, LANES),
                                lambda i, j: (i // tiles_per_b, j, i % tiles_per_b, 0))
    else:
        out_shape = jax.ShapeDtypeStruct((m, n_cols), F32)
        out_spec = pl.BlockSpec((tm, tn), lambda i, j: (i, j))

    kern = functools.partial(
        _norm_matmul_kernel, k_sizes=k_sizes, has_gain=has_gain, n_norm_tiles=n_norm_tiles,
        n_col_tiles=n_col_tiles, has_res=residual is not None,
        head_major=head_major_bt is not None, row_chunk=min(128, tm))
    return pl.pallas_call(
        kern,
        out_shape=out_shape,
        grid=(m // tm, n_col_tiles),
        in_specs=in_specs,
        out_specs=out_spec,
        scratch_shapes=[pltpu.VMEM((tm, k_total), BF16)],
        compiler_params=_params("parallel", "arbitrary"),
    )(*args)


def _select_blocks(kmean, q_tile, tile, nb):
    blk = q_tile.shape[0]
    if tile == 0:
        return jnp.zeros((nb, blk), F32)
    kh, kl = _split_bf16(kmean)
    qh, ql = _split_bf16(q_tile)
    gate = (lax.dot_general(kh, qh, _NT, preferred_element_type=F32)
            + lax.dot_general(kh, ql, _NT, preferred_element_type=F32)
            + lax.dot_general(kl, qh, _NT, preferred_element_type=F32))
    blkid = lax.broadcasted_iota(jnp.int32, (nb, blk), 0)
    valid = blkid < tile
    gate = jnp.where(valid, gate, NEG_INF)
    terms = []
    for mth in range(tile):
        gm = gate[mth:mth + 1, :]
        terms.append(jnp.where(gm > gate, 1.0,
                               jnp.where(gm == gate, jnp.where(blkid > mth, 1.0, 0.0), 0.0)))
    while len(terms) > 1:
        terms = [terms[k] + terms[k + 1] for k in range(0, len(terms) - 1, 2)] + terms[len(terms) & ~1:]
    return jnp.where(valid, jnp.where(terms[0] < MOBA_TOPK, 1.0, 0.0), 0.0)


def _moba_kernel(slopes_ref, first_ref, q_ref, qall_ref, k_ref, v_ref, o_ref,
                 kb_ref, vt_ref, kmean_ref, sel_all_ref, m_ref, l_ref, acc_ref, *s_refs,
                 nb, blk, q_tiles, scale):
    h = pl.program_id(1)
    step = pl.program_id(2)

    scale2 = scale * LOG2E
    slope2 = slopes_ref[h] * LOG2E

    @pl.when(step == 0)
    def _prep():
        for n in range(nb):
            kblk = k_ref[0, 0, n * blk:(n + 1) * blk, :]
            kb_ref[n] = kblk.astype(BF16)
            kmean_ref[n:n + 1, :] = jnp.mean(kblk, axis=0, keepdims=True)
            vt_ref[n] = v_ref[0, 0, n * blk:(n + 1) * blk, :].T.astype(BF16)
        kmean = kmean_ref[...]
        for tile in range(nb):
            sel_all_ref[tile] = _select_blocks(
                kmean, qall_ref[0, 0, tile * blk:(tile + 1) * blk, :], tile, nb)

    t2 = lax.broadcasted_iota(jnp.int32, (2 * blk, blk), 1)
    j2 = lax.broadcasted_iota(jnp.int32, (2 * blk, blk), 0)
    bias_pair = (t2 - j2).astype(F32) * (-slope2)
    t_idx = lax.broadcasted_iota(jnp.int32, (blk, blk), 1)
    j_idx = lax.broadcasted_iota(jnp.int32, (blk, blk), 0)
    dist_own = (t_idx - j_idx).astype(F32)
    last_pair = nb // 2 - 1

    first_pair = first_ref[h, q_tiles * step]
    end_pair = (q_tiles * step + q_tiles) // 2

    def tile_stages(k):
        qi = q_tiles * step + k
        qb = q_ref[0, 0, k * blk:(k + 1) * blk, :].astype(BF16)
        sel_ref = sel_all_ref.at[qi]
        m_k, l_k, acc_k, s_k = m_ref.at[k], l_ref.at[k], acc_ref.at[k], s_refs[k]

        def pair_scores(pair):
            n0 = 2 * pair
            sel0 = sel_ref[pl.ds(n0, 1), :]
            sel1 = sel_ref[pl.ds(n0 + 1, 1), :]
            kpair = jnp.concatenate([kb_ref[n0], kb_ref[n0 + 1]], axis=0)
            s = lax.dot_general(kpair, qb, _NT, preferred_element_type=F32) * scale2 + bias_pair
            return jnp.concatenate([jnp.where(sel0 > 0, s[:blk], NEG_INF),
                                    jnp.where(sel1 > 0, s[blk:], NEG_INF)], axis=0)

        def start():
            s_k[first_pair % 2] = pair_scores(jnp.minimum(first_pair, last_pair))
            s = lax.dot_general(kb_ref[qi], qb, _NT, preferred_element_type=F32) * scale2
            s = s - slope2 * dist_own
            s = jnp.where(dist_own >= 0, s, NEG_INF)
            m0 = jnp.max(s, axis=0, keepdims=True)
            p = jnp.exp2(s - m0)
            m_k[...] = m0
            l_k[...] = jnp.sum(p, axis=0, keepdims=True)
            acc_k[...] = jnp.dot(vt_ref[qi], p.astype(BF16), preferred_element_type=F32)

        def sweep_step(pair):
            n0 = 2 * pair
            s = s_k[pair % 2]
            s_k[(pair + 1) % 2] = pair_scores(jnp.minimum(pair + 1, last_pair))
            off = slope2 * ((qi - n0) * blk).astype(F32)
            m_old = m_k[...]
            m_new = jnp.maximum(m_old, jnp.max(s, axis=0, keepdims=True) - off)
            alpha = jnp.exp2(m_old - m_new)
            p = jnp.exp2(s - (m_new + off))
            pb = p.astype(BF16)
            l_k[...] = alpha * l_k[...] + jnp.sum(p, axis=0, keepdims=True)
            acc_k[...] = (alpha * acc_k[...]
                          + jnp.dot(vt_ref[n0], pb[:blk], preferred_element_type=F32)
                          + jnp.dot(vt_ref[n0 + 1], pb[blk:], preferred_element_type=F32))
            m_k[...] = m_new

        def finish():
            o_ref[0, k * blk:(k + 1) * blk, :] = (acc_k[...] / l_k[...]).T

        return start, sweep_step, finish

    tiles = [tile_stages(k) for k in range(q_tiles)]
    for start, _, _ in tiles:
        start()

    def sweep_all(pair, carry):
        for _, sweep_step, _ in tiles:
            sweep_step(pair)
        return carry

    lax.fori_loop(first_pair, end_pair, sweep_all, 0)
    for _, _, finish in tiles:
        finish()


def _moba_first_pairs(slopes, qk_norm_bound, scale, nb, blk):
    slope2 = slopes * LOG2E
    room = 2.0 * (qk_norm_bound * NORM_MARGIN * scale * LOG2E) - EXP2_ZERO
    d = jnp.arange(nb, dtype=F32)
    n_live = jnp.sum(d[None, :] * (slope2[:, None] * blk) + slope2[:, None] <= room, axis=1)
    tiles = jnp.arange(nb, dtype=jnp.int32)
    return jnp.maximum(tiles[None, :] - n_live[:, None].astype(jnp.int32), 0) // 2


def moba_attention(qkv, slopes, qk_norm_bound, *, q_head0, k_head0, v_head0, n_heads, q_tiles=4):
    b, _, t, dh = qkv.shape
    blk = MOBA_BLOCK
    assert t % (2 * blk) == 0
    nb = t // blk
    assert nb % q_tiles == 0
    tq = q_tiles * blk
    scale = dh ** -0.5
    first_pairs = _moba_first_pairs(slopes, qk_norm_bound, scale, nb, blk)
    kern = functools.partial(_moba_kernel, nb=nb, blk=blk, q_tiles=q_tiles, scale=scale)
    return pl.pallas_call(
        kern,
        out_shape=jax.ShapeDtypeStruct((b, t, n_heads * dh), F32),
        grid=(b, n_heads, nb // q_tiles),
        in_specs=[
            pl.BlockSpec(memory_space=pltpu.SMEM),
            pl.BlockSpec(memory_space=pltpu.SMEM),
            pl.BlockSpec((1, 1, tq, dh), lambda bi, h, qi: (bi, q_head0 + h, qi, 0)),
            pl.BlockSpec((1, 1, t, dh), lambda bi, h, qi: (bi, q_head0 + h, 0, 0)),
            pl.BlockSpec((1, 1, t, dh), lambda bi, h, qi: (bi, k_head0 + h, 0, 0)),
            pl.BlockSpec((1, 1, t, dh), lambda bi, h, qi: (bi, v_head0 + h, 0, 0)),
        ],
        out_specs=pl.BlockSpec((1, tq, dh), lambda bi, h, qi: (bi, qi, h)),
        scratch_shapes=[
            pltpu.VMEM((nb, blk, dh), BF16),
            pltpu.VMEM((nb, dh, blk), BF16),
            pltpu.VMEM((nb, dh), F32),
            pltpu.VMEM((nb, nb, blk), F32),
            pltpu.VMEM((q_tiles, 1, blk), F32),
            pltpu.VMEM((q_tiles, 1, blk), F32),
            pltpu.VMEM((q_tiles, dh, blk), F32),
        ] + [
            pltpu.VMEM((2, 2 * blk, blk), F32) for _ in range(q_tiles)
        ],
        compiler_params=_params("parallel", "parallel", "arbitrary"),
    )(slopes, first_pairs, qkv, qkv, qkv, qkv)


def _sb_kernel(q_ref, k_ref, v_ref, o_ref, kb_ref, vb_ref, *, nb, blk, q_tiles, scale):
    step = pl.program_id(2)

    @pl.when(step == 0)
    def _prep():
        for n in range(nb):
            kb_ref[n] = k_ref[0, 0, n * blk:(n + 1) * blk, :].astype(BF16)
            vb_ref[n] = v_ref[0, 0, n * blk:(n + 1) * blk, :].astype(BF16)

    row = lax.broadcasted_iota(jnp.int32, (blk, blk), 0)
    col = lax.broadcasted_iota(jnp.int32, (blk, blk), 1)
    after = jnp.where(row > col, 1.0, 0.0).astype(BF16)
    strict = col < row

    def log_sigmoids(z):
        t = jnp.log(1.0 + jnp.exp(-jnp.abs(z)))
        log_beta = jnp.minimum(z, 0.0) - t
        return log_beta, log_beta - z

    def sum_after(x):
        hi, lo = _split_bf16(x)
        return (jnp.dot(hi, after, preferred_element_type=F32)
                + jnp.dot(lo, after, preferred_element_type=F32))

    def row_sum(x):
        return jnp.sum(x, axis=1, keepdims=True)

    def pair_step(qb, n_late, late_is_own, acc, carry):
        has_early = n_late >= 1
        n_early = jnp.maximum(n_late - 1, 0)
        kpair = jnp.concatenate([kb_ref[n_early], kb_ref[n_late]], axis=0)
        z = lax.dot_general(qb, kpair, _NT, preferred_element_type=F32) * scale
        log_beta, log_om = log_sigmoids(z)
        om_early, om_late = log_om[:, :blk], log_om[:, blk:]
        if late_is_own:
            om_late = jnp.where(strict, om_late, 0.0)
            sum_late = row_sum(om_late)
            a_late = jnp.where(strict, jnp.exp(log_beta[:, blk:] + sum_after(om_late)), 0.0)
            left = sum_late
        else:
            sum_late = row_sum(om_late)
            a_late = jnp.exp(log_beta[:, blk:] + sum_after(om_late) + carry)
            left = carry + sum_late
        a_early = jnp.exp(log_beta[:, :blk] + sum_after(om_early) + left)
        v_early = vb_ref[n_early]
        v_early = jnp.where(has_early, v_early, jnp.zeros_like(v_early))
        contrib = (jnp.dot(a_late.astype(BF16), vb_ref[n_late], preferred_element_type=F32)
                   + jnp.dot(a_early.astype(BF16), v_early, preferred_element_type=F32))
        acc = contrib if acc is None else acc + contrib
        return acc, left + row_sum(om_early)

    tiles = [q_tiles * step + k for k in range(q_tiles)]
    qbs = [q_ref[0, 0, k * blk:(k + 1) * blk, :].astype(BF16) for k in range(q_tiles)]
    firsts = [pair_step(qb, qi, True, None, None) for qb, qi in zip(qbs, tiles)]

    for k, (qb, qi, (acc0, carry0)) in enumerate(zip(qbs, tiles, firsts)):
        n_pairs = qi // 2

        def cond(state, n_pairs=n_pairs):
            pair, _, _, carry_max = state
            return jnp.logical_and(pair < n_pairs, carry_max > EXP_ZERO)

        def body(state, qb=qb, qi=qi):
            pair, acc, carry, _ = state
            acc, carry = pair_step(qb, qi - 2 - 2 * pair, False, acc, carry)
            return pair + 1, acc, carry, jnp.max(carry)

        _, acc, _, _ = lax.while_loop(cond, body, (jnp.int32(0), acc0, carry0, jnp.max(carry0)))
        o_ref[0, k * blk:(k + 1) * blk, :] = acc


def stick_breaking_attention(qkv, *, q_head0, k_head0, v_head0, n_heads, blk=256, q_tiles=2):
    b, _, t, dh = qkv.shape
    assert t % (blk * q_tiles) == 0
    nb = t // blk
    tq = blk * q_tiles
    kern = functools.partial(_sb_kernel, nb=nb, blk=blk, q_tiles=q_tiles, scale=dh ** -0.5)
    return pl.pallas_call(
        kern,
        out_shape=jax.ShapeDtypeStruct((b, t, n_heads * dh), F32),
        grid=(b, n_heads, t // tq),
        in_specs=[
            pl.BlockSpec((1, 1, tq, dh), lambda bi, h, qi: (bi, q_head0 + h, qi, 0)),
            pl.BlockSpec((1, 1, t, dh), lambda bi, h, qi: (bi, k_head0 + h, 0, 0)),
            pl.BlockSpec((1, 1, t, dh), lambda bi, h, qi: (bi, v_head0 + h, 0, 0)),
        ],
        out_specs=pl.BlockSpec((1, tq, dh), lambda bi, h, qi: (bi, qi, h)),
        scratch_shapes=[
            pltpu.VMEM((nb, blk, dh), BF16),
            pltpu.VMEM((nb, blk, dh), BF16),
        ],
        compiler_params=_params("parallel", "parallel", "arbitrary"),
    )(qkv, qkv, qkv)


def _xattn_kernel(q_ref, kv_ref, o_ref, *, n_heads, dh, scale):
    width = n_heads * dh
    for hh in range(n_heads):
        qh = q_ref[0, :, hh * dh:(hh + 1) * dh].astype(BF16)
        kh = kv_ref[0, :, hh * dh:(hh + 1) * dh].astype(BF16)
        vh = kv_ref[0, :, width + hh * dh:width + (hh + 1) * dh].astype(BF16)
        s = lax.dot_general(qh, kh, _NT, preferred_element_type=F32) * scale
        e = jnp.exp(s - jnp.max(s, axis=-1, keepdims=True))
        p = e / jnp.sum(e, axis=-1, keepdims=True)
        o_ref[0, :, hh * dh:(hh + 1) * dh] = jnp.dot(p.astype(BF16), vh, preferred_element_type=F32)


def cross_attention_core(q, kv, *, n_heads, tq=512):
    b, t, width = q.shape
    mlen = kv.shape[1]
    dh = width // n_heads
    tq = min(tq, t)
    kern = functools.partial(_xattn_kernel, n_heads=n_heads, dh=dh, scale=dh ** -0.5)
    return pl.pallas_call(
        kern,
        out_shape=jax.ShapeDtypeStruct((b, t, width), F32),
        grid=(b, t // tq),
        in_specs=[
            pl.BlockSpec((1, tq, width), lambda bi, i: (bi, i, 0)),
            pl.BlockSpec((1, mlen, 2 * width), lambda bi, i: (bi, 0, 0)),
        ],
        out_specs=pl.BlockSpec((1, tq, width), lambda bi, i: (bi, i, 0)),
        compiler_params=_params("parallel", "arbitrary"),
    )(q, kv)


def _extract_topk(work, n_rounds, *, exact, want_rank=False, n_masked=0):
    rows = lax.broadcasted_iota(jnp.int32, work.shape, 0).astype(F32)
    n_rows = float(work.shape[0])
    rank = jnp.full(work.shape, float(n_rounds), F32) if want_rank else None
    vals, firsts = [], []
    for r in range(n_rounds):
        mx = jnp.max(work, axis=0, keepdims=True)
        if exact:
            first = jnp.min(jnp.where(work == mx, rows, n_rows), axis=0, keepdims=True)
            hit = rows == first
            firsts.append(first)
        else:
            hit = work == mx
        work = jnp.where(hit, -jnp.inf, work)
        if want_rank:
            rank = jnp.where(hit, float(r), rank)
        vals.append(mx)
    if exact:
        tie = jnp.zeros_like(vals[0])
    else:
        removed = jnp.sum(jnp.where(work == -jnp.inf, 1.0, 0.0), axis=0, keepdims=True)
        tie = jnp.where(removed == float(n_rounds + n_masked), 0.0, 1.0)
    return vals, firsts, work, rank, tie


def _staircase_candidates(vals0, vals1, topk):
    sub = 8
    val1_all = jnp.concatenate(vals1, axis=0)
    val1_head = val1_all[:sub]
    row = lax.broadcasted_iota(jnp.int32, val1_head.shape, 0)
    slabs = [val1_all + vals0[0]]
    n_masked = 0
    for a in range(1, sub):
        b_max = topk // (a + 1) - 1
        slab = val1_head + vals0[a]
        if b_max < sub - 1:
            slab = jnp.where(row <= b_max, slab, -jnp.inf)
            n_masked += sub - 1 - b_max
        slabs.append(slab)
    slabs.append(jnp.concatenate(vals0[sub:], axis=0) + vals1[0])
    return jnp.concatenate(slabs, axis=0), n_masked


def _route_tables(scores0, scores1, topk, exact):
    vals0, firsts0, left0, _, tie0 = _extract_topk(scores0, topk, exact=exact)
    vals1, _, left1, rank1, tie1 = _extract_topk(scores1, topk, exact=exact, want_rank=True)
    cand, n_masked = _staircase_candidates(vals0, vals1, topk)
    tops, firsts2, _, _, tie2 = _extract_topk(cand, topk, exact=exact, n_masked=n_masked)
    tau = tops[topk - 1]
    z = jnp.zeros_like(tau)
    for tv in tops:
        z = z + jnp.exp(tv - tops[0])
    val1_all = jnp.concatenate(vals1, axis=0)
    rows = lax.broadcasted_iota(jnp.int32, scores0.shape, 0).astype(F32)
    count = jnp.zeros(scores0.shape, F32)
    sub = 8
    for a in range(topk):
        if exact:
            lo = 0 if a == 0 else (topk + (a - 1) * sub if a < sub else topk + (sub - 1) * sub + a - sub)
            hi = lo + (topk if a == 0 else (sub if a < sub else 1))
            n_sel = jnp.zeros_like(tau)
            for f in firsts2:
                n_sel = n_sel + jnp.where(f >= float(lo), jnp.where(f < float(hi), 1.0, 0.0), 0.0)
            at_rank_a = rows == firsts0[a]
        else:
            n_sel = jnp.sum(jnp.where(val1_all + vals0[a] >= tau, 1.0, 0.0), axis=0, keepdims=True)
            at_rank_a = scores0 == vals0[a]
        count = jnp.where(at_rank_a, n_sel, count)
    e0 = jnp.where(left0 == -jnp.inf, jnp.exp(scores0 - vals0[0]), 0.0) / z
    e1 = jnp.where(left1 == -jnp.inf, jnp.exp(scores1 - vals1[0]), 0.0)
    tie = jnp.maximum(jnp.maximum(tie0, tie1), tie2)
    return count, e0, rank1.astype(BF16), e1.astype(BF16), tie


def _peer_select_kernel(q_ref, keys_ref, cnt_ref, e0_ref, rank_ref, e1_ref, *, topk, heads):
    def head_scores(hh):
        out = []
        for p in range(2):
            c0 = (2 * hh + p) * LANES
            qs = q_ref[:, c0:c0 + LANES].astype(BF16)
            out.append(lax.dot_general(keys_ref[2 * hh + p], qs, _NT,
                                       preferred_element_type=F32))
        return out

    def write_tables(exact):
        tie = None
        for hh in range(heads):
            s0, s1 = head_scores(hh)
            cnt_ref[hh], e0_ref[hh], rank_ref[hh], e1_ref[hh], t = _route_tables(s0, s1, topk, exact)
            tie = t if tie is None else jnp.maximum(tie, t)
        return tie

    tie = write_tables(exact=False)

    @pl.when(jnp.max(tie) > 0.0)
    def _redo_exact():
        write_tables(exact=True)


def peer_select(q, keys_bf16, *, tm=256, heads_per_step=4):
    n = q.shape[0]
    n_heads = keys_bf16.shape[0] // 2
    n_keys = keys_bf16.shape[1]
    tm = min(tm, n)
    hs = heads_per_step
    assert n_heads % hs == 0
    big_f32 = jax.ShapeDtypeStruct((n_heads, n_keys, n), F32)
    big_bf16 = jax.ShapeDtypeStruct((n_heads, n_keys, n), BF16)
    big_spec = pl.BlockSpec((hs, n_keys, tm), lambda i, h: (h, 0, i))
    kern = functools.partial(_peer_select_kernel, topk=PEER_TOPK, heads=hs)
    return pl.pallas_call(
        kern,
        out_shape=(big_f32, big_f32, big_bf16, big_bf16),
        grid=(n // tm, n_heads // hs),
        in_specs=[
            pl.BlockSpec((tm, hs * 2 * LANES), lambda i, h: (i, h)),
            pl.BlockSpec((hs * 2, n_keys, keys_bf16.shape[2]), lambda i, h: (h, 0, 0)),
        ],
        out_specs=(big_spec, big_spec, big_spec, big_spec),
        compiler_params=_params("parallel", "arbitrary"),
    )(q, keys_bf16)


def _peer_main_kernel(x_ref, g_ref, u_ref, v_ref, cnt_ref, e0_ref, rank_ref, e1_ref, o_ref,
                      h_ref, gate_ref, *, n_heads, n_keys, row_chunk):
    e = pl.program_id(1)
    tm = h_ref.shape[0]
    te = u_ref.shape[0]
    groups = te // n_keys

    @pl.when(e == 0)
    def _prologue():
        for r0 in range(0, tm, row_chunk):
            x = x_ref[r0:r0 + row_chunk, :]
            ms = jnp.mean(x * x, axis=-1, keepdims=True)
            h_ref[r0:r0 + row_chunk, :] = ((x * lax.rsqrt(ms + RMS_EPS)) * g_ref[...]).astype(BF16)
            o_ref[r0:r0 + row_chunk, :] = x

    act = lax.dot_general(u_ref[...], h_ref[...], _NT, preferred_element_type=F32)
    for gi in range(groups):
        i = e * groups + gi
        route = jnp.zeros((n_keys, tm), BF16)
        for h in range(n_heads):
            cnt = cnt_ref[h, pl.ds(i, 1), :].astype(BF16)
            w0 = e0_ref[h, pl.ds(i, 1), :].astype(BF16)
            route = route + jnp.where(rank_ref[h] < cnt, e1_ref[h] * w0, jnp.zeros((), BF16))
        a = act[gi * n_keys:(gi + 1) * n_keys, :]
        gelu = 0.5 * a * (1.0 + lax.erf(a * np.float32(np.sqrt(0.5))))
        gate_ref[gi * n_keys:(gi + 1) * n_keys, :] = (gelu * route.astype(F32)).astype(BF16)
    o_ref[...] += lax.dot_general(gate_ref[...], v_ref[...], _TN, preferred_element_type=F32)


def peer_main(x, g, u_bf16, v_bf16, count, e0, rank, e1, *, tm=512, te=512):
    n, d = x.shape
    n_exp = u_bf16.shape[0]
    n_heads, n_keys, _ = count.shape
    tm = min(tm, n)
    assert n % tm == 0 and n_exp % te == 0 and te % n_keys == 0 and tm % LANES == 0
    once = pl.Buffered(1)
    big_spec = pl.BlockSpec((n_heads, n_keys, tm), lambda i, e: (0, 0, i), pipeline_mode=once)
    kern = functools.partial(_peer_main_kernel, n_heads=n_heads, n_keys=n_keys,
                             row_chunk=min(128, tm))
    return pl.pallas_call(
        kern,
        out_shape=jax.ShapeDtypeStruct((n, d), F32),
        grid=(n // tm, n_exp // te),
        in_specs=[
            pl.BlockSpec((tm, d), lambda i, e: (i, 0), pipeline_mode=once),
            pl.BlockSpec((1, d), lambda i, e: (0, 0)),
            pl.BlockSpec((te, d), lambda i, e: (e, 0)),
            pl.BlockSpec((te, d), lambda i, e: (e, 0)),
            big_spec, big_spec, big_spec, big_spec,
        ],
        out_specs=pl.BlockSpec((tm, d), lambda i, e: (i, 0), pipeline_mode=once),
        scratch_shapes=[
            pltpu.VMEM((tm, d), BF16),
            pltpu.VMEM((te, tm), BF16),
        ],
        compiler_params=_params("parallel", "arbitrary"),
    )(x, g.reshape(1, d).astype(F32), u_bf16, v_bf16, count, e0, rank, e1)


def _layer(x, mem, norm_mix_g, w_in, moba_q_norm_g, moba_k_norm_g, moba_out_norm_g,
           sb_out_norm_g, w_out, norm_xattn_g, norm_mem_g, w_xq, w_xkv, xattn_q_norm_g,
           xattn_k_norm_g, w_xo, norm_ffn_g, w_peer_q, peer_sub_keys, peer_u, peer_v):
    b, t, d = x.shape
    n = b * t
    xf = x.reshape(n, d)
    mix_width = w_in.shape[1] // 3
    n_heads = mix_width // (2 * HEAD_DIM)
    grp = n_heads * HEAD_DIM
    slopes = jnp.asarray(2.0 ** (-8.0 * np.arange(1, n_heads + 1) / n_heads), dtype=F32)

    col_gain = jnp.concatenate([jnp.tile(moba_q_norm_g, n_heads), jnp.tile(moba_k_norm_g, n_heads),
                                jnp.ones((w_in.shape[1] - 2 * grp,), F32)])
    qkv = norm_matmul([xf], [norm_mix_g], w_in.astype(BF16), col_gain=col_gain,
                      n_norm_cols=2 * grp, head_major_bt=(b, t), tn=1024)
    qk_norm_bound = HEAD_DIM * jnp.max(jnp.abs(moba_q_norm_g)) * jnp.max(jnp.abs(moba_k_norm_g))
    o_moba = moba_attention(qkv, slopes, qk_norm_bound, q_head0=0, k_head0=n_heads,
                            v_head0=2 * n_heads, n_heads=n_heads)
    o_sb = stick_breaking_attention(qkv, q_head0=3 * n_heads, k_head0=4 * n_heads,
                                    v_head0=5 * n_heads, n_heads=n_heads)
    x1 = norm_matmul([o_moba.reshape(n, grp), o_sb.reshape(n, grp)],
                     [moba_out_norm_g, sb_out_norm_g], w_out.astype(BF16), residual=xf, tn=1024)

    xw = w_xq.shape[1]
    n_mem = mem.shape[1]
    kv_gain = jnp.concatenate([jnp.tile(xattn_k_norm_g, N_XATTN_HEADS), jnp.ones((xw,), F32)])
    kv = norm_matmul([mem.reshape(b * n_mem, d)], [norm_mem_g], w_xkv.astype(BF16),
                     col_gain=kv_gain, n_norm_cols=xw)
    xq = norm_matmul([x1], [norm_xattn_g], w_xq.astype(BF16),
                     col_gain=jnp.tile(xattn_q_norm_g, N_XATTN_HEADS), n_norm_cols=xw)
    o_x = cross_attention_core(xq.reshape(b, t, xw), kv.reshape(b, n_mem, 2 * xw),
                               n_heads=N_XATTN_HEADS)
    x2 = norm_matmul([o_x.reshape(n, xw)], [None], w_xo.astype(BF16), residual=x1, tn=2048)

    pq = norm_matmul([x2], [norm_ffn_g], w_peer_q.astype(BF16), tn=1024)
    keys = peer_sub_keys.reshape(PEER_HEADS * 2, PEER_N_KEYS, -1).astype(BF16)
    count, e0, rank, e1 = peer_select(pq, keys)
    x3 = peer_main(x2, norm_ffn_g, peer_u.astype(BF16), peer_v.astype(BF16), count, e0, rank, e1)
    return x3.reshape(b, t, d)


def kernel(x, mem, norm_mix_g, w_in, moba_q_norm_g, moba_k_norm_g, moba_out_norm_g, sb_out_norm_g,
           w_out, norm_xattn_g, norm_mem_g, w_xq, w_xkv, xattn_q_norm_g, xattn_k_norm_g, w_xo,
           norm_ffn_g, w_peer_q, peer_sub_keys, peer_u, peer_v):
    depth = w_in.shape[0]
    for l in range(depth):
        x = _layer(x, mem, norm_mix_g[l], w_in[l], moba_q_norm_g[l], moba_k_norm_g[l],
                   moba_out_norm_g[l], sb_out_norm_g[l], w_out[l], norm_xattn_g[l], norm_mem_g[l],
                   w_xq[l], w_xkv[l], xattn_q_norm_g[l], xattn_k_norm_g[l], w_xo[l], norm_ffn_g[l],
                   w_peer_q[l], peer_sub_keys[l], peer_u[l], peer_v[l])
    return x
```

```python
import functools

import numpy as np
import jax
import jax.numpy as jnp
from jax import lax
from jax.experimental import pallas as pl
from jax.experimental.pallas import tpu as pltpu

HEAD_DIM = 128
MOBA_BLOCK = 256
MOBA_TOPK = 3
N_XATTN_HEADS = 4
PEER_HEADS = 8
PEER_N_KEYS = 128
PEER_TOPK = 16
RMS_EPS = 1e-6
NEG_INF = -1e30
EXP_ZERO = -110.0
EXP2_ZERO = -160.0
LOG2E = 1.4426950408889634
NORM_MARGIN = 1.02

LANES = 128
VMEM_LIMIT = 56 * 1024 * 1024

F32 = jnp.float32
BF16 = jnp.bfloat16

_NT = (((1,), (1,)), ((), ()))
_TN = (((0,), (0,)), ((), ()))


def _params(*sem):
    return pltpu.CompilerParams(dimension_semantics=sem, vmem_limit_bytes=VMEM_LIMIT)


def _split_bf16(x):
    hi = x.astype(BF16)
    lo = (x - hi.astype(F32)).astype(BF16)
    return hi, lo


def _norm_matmul_kernel(*refs, k_sizes, has_gain, n_norm_tiles, n_col_tiles, has_res,
                        head_major, row_chunk):
    n_in = len(k_sizes)
    pos = 0
    x_refs = refs[pos:pos + n_in]; pos += n_in
    g_refs = []
    for hg in has_gain:
        if hg:
            g_refs.append(refs[pos]); pos += 1
        else:
            g_refs.append(None)
    w_ref = refs[pos]; pos += 1
    cg_ref = None
    if n_norm_tiles > 0:
        cg_ref = refs[pos]; pos += 1
    res_ref = None
    if has_res:
        res_ref = refs[pos]; pos += 1
    o_ref = refs[pos]; pos += 1
    h_ref = refs[pos]

    j = pl.program_id(1)
    tm = h_ref.shape[0]

    @pl.when(j == 0)
    def _prologue():
        off = 0
        for x_ref, g_ref, ksz in zip(x_refs, g_refs, k_sizes):
            for r0 in range(0, tm, row_chunk):
                x = x_ref[r0:r0 + row_chunk, :]
                if g_ref is not None:
                    ms = jnp.mean(x * x, axis=-1, keepdims=True)
                    x = (x * lax.rsqrt(ms + RMS_EPS)) * g_ref[...]
                h_ref[r0:r0 + row_chunk, off:off + ksz] = x.astype(BF16)
            off += ksz

    acc = jnp.dot(h_ref[...], w_ref[...], preferred_element_type=F32)
    tn = acc.shape[1]

    def finish(get_group):
        for hh in range(tn // LANES):
            y = get_group(hh)
            if res_ref is not None:
                y = y + res_ref[:, hh * LANES:(hh + 1) * LANES]
            if head_major:
                o_ref[0, hh] = y
            else:
                o_ref[:, hh * LANES:(hh + 1) * LANES] = y

    def plain_group(hh):
        return acc[:, hh * LANES:(hh + 1) * LANES]

    def normed_group(hh):
        a = acc[:, hh * LANES:(hh + 1) * LANES]
        ms = jnp.mean(a * a, axis=-1, keepdims=True)
        return (a * lax.rsqrt(ms + RMS_EPS)) * cg_ref[:, hh * LANES:(hh + 1) * LANES]

    if n_norm_tiles == 0:
        finish(plain_group)
    elif n_norm_tiles >= n_col_tiles:
        finish(normed_group)
    else:
        @pl.when(j < n_norm_tiles)
        def _():
            finish(normed_group)

        @pl.when(j >= n_norm_tiles)
        def _():
            finish(plain_group)


def norm_matmul(xs, gains, w, *, col_gain=None, n_norm_cols=0, residual=None,
                head_major_bt=None, tm=512, tn=512):
    m = xs[0].shape[0]
    k_sizes = tuple(int(x.shape[1]) for x in xs)
    k_total = sum(k_sizes)
    n_cols = w.shape[1]
    tm = min(tm, m)
    tn = min(tn, n_cols)
    assert m % tm == 0 and n_cols % tn == 0 and w.shape[0] == k_total
    assert n_norm_cols % tn == 0
    n_col_tiles = n_cols // tn
    n_norm_tiles = n_norm_cols // tn
    has_gain = tuple(g is not None for g in gains)

    args, in_specs = [], []
    for x, ksz in zip(xs, k_sizes):
        args.append(x)
        in_specs.append(pl.BlockSpec((tm, ksz), lambda i, j: (i, 0)))
    for g, ksz in zip(gains, k_sizes):
        if g is not None:
            args.append(g.reshape(1, ksz).astype(F32))
            in_specs.append(pl.BlockSpec((1, ksz), lambda i, j: (0, 0)))
    args.append(w)
    in_specs.append(pl.BlockSpec((k_total, tn), lambda i, j: (0, j)))
    if n_norm_tiles > 0:
        args.append(col_gain.reshape(1, n_cols).astype(F32))
        in_specs.append(pl.BlockSpec((1, tn), lambda i, j: (0, j)))
    if residual is not None:
        args.append(residual)
        in_specs.append(pl.BlockSpec((tm, tn), lambda i, j: (i, j)))

    if head_major_bt is not None:
        b, t = head_major_bt
        assert b * t == m and t % tm == 0
        tiles_per_b = t // tm
        out_shape = jax.ShapeDtypeStruct((b, n_cols // LANES, t, LANES), F32)
        out_spec = pl.BlockSpec((1, tn // LANES, tm, LANES),
                                lambda i, j: (i // tiles_per_b, j, i % tiles_per_b, 0))
    else:
        out_shape = jax.ShapeDtypeStruct((m, n_cols), F32)
        out_spec = pl.BlockSpec((tm, tn), lambda i, j: (i, j))

    kern = functools.partial(
        _norm_matmul_kernel, k_sizes=k_sizes, has_gain=has_gain, n_norm_tiles=n_norm_tiles,
        n_col_tiles=n_col_tiles, has_res=residual is not None,
        head_major=head_major_bt is not None, row_chunk=min(128, tm))
    return pl.pallas_call(
        kern,
        out_shape=out_shape,
        grid=(m // tm, n_col_tiles),
        in_specs=in_specs,
        out_specs=out_spec,
        scratch_shapes=[pltpu.VMEM((tm, k_total), BF16)],
        compiler_params=_params("parallel", "arbitrary"),
    )(*args)


def _select_blocks(kmean, q_tile, tile, nb):
    blk = q_tile.shape[0]
    if tile == 0:
        return jnp.zeros((nb, blk), F32)
    kh, kl = _split_bf16(kmean)
    qh, ql = _split_bf16(q_tile)
    gate = (lax.dot_general(kh, qh, _NT, preferred_element_type=F32)
            + lax.dot_general(kh, ql, _NT, preferred_element_type=F32)
            + lax.dot_general(kl, qh, _NT, preferred_element_type=F32))
    blkid = lax.broadcasted_iota(jnp.int32, (nb, blk), 0)
    valid = blkid < tile
    gate = jnp.where(valid, gate, NEG_INF)
    terms = []
    for mth in range(tile):
        gm = gate[mth:mth + 1, :]
        terms.append(jnp.where(gm > gate, 1.0,
                               jnp.where(gm == gate, jnp.where(blkid > mth, 1.0, 0.0), 0.0)))
    while len(terms) > 1:
        terms = [terms[k] + terms[k + 1] for k in range(0, len(terms) - 1, 2)] + terms[len(terms) & ~1:]
    return jnp.where(valid, jnp.where(terms[0] < MOBA_TOPK, 1.0, 0.0), 0.0)


def _moba_kernel(slopes_ref, first_ref, q_ref, qall_ref, k_ref, v_ref, o_ref,
                 kb_ref, vt_ref, kmean_ref, sel_all_ref, m_ref, l_ref, acc_ref, *s_refs,
                 nb, blk, q_tiles, scale):
    h = pl.program_id(1)
    step = pl.program_id(2)

    scale2 = scale * LOG2E
    slope2 = slopes_ref[h] * LOG2E

    @pl.when(step == 0)
    def _prep():
        for n in range(nb):
            kblk = k_ref[0, 0, n * blk:(n + 1) * blk, :]
            kb_ref[n] = kblk.astype(BF16)
            kmean_ref[n:n + 1, :] = jnp.mean(kblk, axis=0, keepdims=True)
            vt_ref[n] = v_ref[0, 0, n * blk:(n + 1) * blk, :].T.astype(BF16)
        kmean = kmean_ref[...]
        for tile in range(nb):
            sel_all_ref[tile] = _select_blocks(
                kmean, qall_ref[0, 0, tile * blk:(tile + 1) * blk, :], tile, nb)

    t2 = lax.broadcasted_iota(jnp.int32, (2 * blk, blk), 1)
    j2 = lax.broadcasted_iota(jnp.int32, (2 * blk, blk), 0)
    bias_pair = (t2 - j2).astype(F32) * (-slope2)
    t_idx = lax.broadcasted_iota(jnp.int32, (blk, blk), 1)
    j_idx = lax.broadcasted_iota(jnp.int32, (blk, blk), 0)
    dist_own = (t_idx - j_idx).astype(F32)
    last_pair = nb // 2 - 1

    first_pair = first_ref[h, q_tiles * step]
    end_pair = (q_tiles * step + q_tiles) // 2

    def tile_stages(k):
        qi = q_tiles * step + k
        qb = q_ref[0, 0, k * blk:(k + 1) * blk, :].astype(BF16)
        sel_ref = sel_all_ref.at[qi]
        m_k, l_k, acc_k, s_k = m_ref.at[k], l_ref.at[k], acc_ref.at[k], s_refs[k]

        def pair_scores(pair):
            n0 = 2 * pair
            sel0 = sel_ref[pl.ds(n0, 1), :]
            sel1 = sel_ref[pl.ds(n0 + 1, 1), :]
            kpair = jnp.concatenate([kb_ref[n0], kb_ref[n0 + 1]], axis=0)
            s = lax.dot_general(kpair, qb, _NT, preferred_element_type=F32) * scale2 + bias_pair
            return jnp.concatenate([jnp.where(sel0 > 0, s[:blk], NEG_INF),
                                    jnp.where(sel1 > 0, s[blk:], NEG_INF)], axis=0)

        def start():
            s_k[first_pair % 2] = pair_scores(jnp.minimum(first_pair, last_pair))
            s = lax.dot_general(kb_ref[qi], qb, _NT, preferred_element_type=F32) * scale2
            s = s - slope2 * dist_own
            s = jnp.where(dist_own >= 0, s, NEG_INF)
            m0 = jnp.max(s, axis=0, keepdims=True)
            p = jnp.exp2(s - m0)
            m_k[...] = m0
            l_k[...] = jnp.sum(p, axis=0, keepdims=True)
            acc_k[...] = jnp.dot(vt_ref[qi], p.astype(BF16), preferred_element_type=F32)

        def sweep_step(pair):
            n0 = 2 * pair
            s = s_k[pair % 2]
            s_k[(pair + 1) % 2] = pair_scores(jnp.minimum(pair + 1, last_pair))
            off = slope2 * ((qi - n0) * blk).astype(F32)
            m_old = m_k[...]
            m_new = jnp.maximum(m_old, jnp.max(s, axis=0, keepdims=True) - off)
            alpha = jnp.exp2(m_old - m_new)
            p = jnp.exp2(s - (m_new + off))
            pb = p.astype(BF16)
            l_k[...] = alpha * l_k[...] + jnp.sum(p, axis=0, keepdims=True)
            acc_k[...] = (alpha * acc_k[...]
                          + jnp.dot(vt_ref[n0], pb[:blk], preferred_element_type=F32)
                          + jnp.dot(vt_ref[n0 + 1], pb[blk:], preferred_element_type=F32))
            m_k[...] = m_new

        def finish():
            o_ref[0, k * blk:(k + 1) * blk, :] = (acc_k[...] / l_k[...]).T

        return start, sweep_step, finish

    tiles = [tile_stages(k) for k in range(q_tiles)]
    for start, _, _ in tiles:
        start()

    def sweep_all(pair, carry):
        for _, sweep_step, _ in tiles:
            sweep_step(pair)
        return carry

    lax.fori_loop(first_pair, end_pair, sweep_all, 0)
    for _, _, finish in tiles:
        finish()


def _moba_first_pairs(slopes, qk_norm_bound, scale, nb, blk):
    slope2 = slopes * LOG2E
    room = 2.0 * (qk_norm_bound * NORM_MARGIN * scale * LOG2E) - EXP2_ZERO
    d = jnp.arange(nb, dtype=F32)
    n_live = jnp.sum(d[None, :] * (slope2[:, None] * blk) + slope2[:, None] <= room, axis=1)
    tiles = jnp.arange(nb, dtype=jnp.int32)
    return jnp.maximum(tiles[None, :] - n_live[:, None].astype(jnp.int32), 0) // 2


def moba_attention(qkv, slopes, qk_norm_bound, *, q_head0, k_head0, v_head0, n_heads, q_tiles=4):
    b, _, t, dh = qkv.shape
    blk = MOBA_BLOCK
    assert t % (2 * blk) == 0
    nb = t // blk
    assert nb % q_tiles == 0
    tq = q_tiles * blk
    scale = dh ** -0.5
    first_pairs = _moba_first_pairs(slopes, qk_norm_bound, scale, nb, blk)
    kern = functools.partial(_moba_kernel, nb=nb, blk=blk, q_tiles=q_tiles, scale=scale)
    return pl.pallas_call(
        kern,
        out_shape=jax.ShapeDtypeStruct((b, t, n_heads * dh), F32),
        grid=(b, n_heads, nb // q_tiles),
        in_specs=[
            pl.BlockSpec(memory_space=pltpu.SMEM),
            pl.BlockSpec(memory_space=pltpu.SMEM),
            pl.BlockSpec((1, 1, tq, dh), lambda bi, h, qi: (bi, q_head0 + h, qi, 0)),
            pl.BlockSpec((1, 1, t, dh), lambda bi, h, qi: (bi, q_head0 + h, 0, 0)),
            pl.BlockSpec((1, 1, t, dh), lambda bi, h, qi: (bi, k_head0 + h, 0, 0)),
            pl.BlockSpec((1, 1, t, dh), lambda bi, h, qi: (bi, v_head0 + h, 0, 0)),
        ],
        out_specs=pl.BlockSpec((1, tq, dh), lambda bi, h, qi: (bi, qi, h)),
        scratch_shapes=[
            pltpu.VMEM((nb, blk, dh), BF16),
            pltpu.VMEM((nb, dh, blk), BF16),
            pltpu.VMEM((nb, dh), F32),
            pltpu.VMEM((nb, nb, blk), F32),
            pltpu.VMEM((q_tiles, 1, blk), F32),
            pltpu.VMEM((q_tiles, 1, blk), F32),
            pltpu.VMEM((q_tiles, dh, blk), F32),
        ] + [
            pltpu.VMEM((2, 2 * blk, blk), F32) for _ in range(q_tiles)
        ],
        compiler_params=_params("parallel", "parallel", "arbitrary"),
    )(slopes, first_pairs, qkv, qkv, qkv, qkv)


def _sb_kernel(q_ref, k_ref, v_ref, o_ref, kb_ref, vb_ref, *, nb, blk, q_tiles, scale):
    step = pl.program_id(2)

    @pl.when(step == 0)
    def _prep():
        for n in range(nb):
            kb_ref[n] = k_ref[0, 0, n * blk:(n + 1) * blk, :].astype(BF16)
            vb_ref[n] = v_ref[0, 0, n * blk:(n + 1) * blk, :].astype(BF16)

    row = lax.broadcasted_iota(jnp.int32, (blk, blk), 0)
    col = lax.broadcasted_iota(jnp.int32, (blk, blk), 1)
    after = jnp.where(row > col, 1.0, 0.0).astype(BF16)
    strict = col < row

    def log_sigmoids(z):
        t = jnp.log(1.0 + jnp.exp(-jnp.abs(z)))
        log_beta = jnp.minimum(z, 0.0) - t
        return log_beta, log_beta - z

    def sum_after(x):
        hi, lo = _split_bf16(x)
        return (jnp.dot(hi, after, preferred_element_type=F32)
                + jnp.dot(lo, after, preferred_element_type=F32))

    def row_sum(x):
        return jnp.sum(x, axis=1, keepdims=True)

    def pair_step(qb, n_late, late_is_own, acc, carry):
        has_early = n_late >= 1
        n_early = jnp.maximum(n_late - 1, 0)
        kpair = jnp.concatenate([kb_ref[n_early], kb_ref[n_late]], axis=0)
        z = lax.dot_general(qb, kpair, _NT, preferred_element_type=F32) * scale
        log_beta, log_om = log_sigmoids(z)
        om_early, om_late = log_om[:, :blk], log_om[:, blk:]
        if late_is_own:
            om_late = jnp.where(strict, om_late, 0.0)
            sum_late = row_sum(om_late)
            a_late = jnp.where(strict, jnp.exp(log_beta[:, blk:] + sum_after(om_late)), 0.0)
            left = sum_late
        else:
            sum_late = row_sum(om_late)
            a_late = jnp.exp(log_beta[:, blk:] + sum_after(om_late) + carry)
            left = carry + sum_late
        a_early = jnp.exp(log_beta[:, :blk] + sum_after(om_early) + left)
        v_early = vb_ref[n_early]
        v_early = jnp.where(has_early, v_early, jnp.zeros_like(v_early))
        contrib = (jnp.dot(a_late.astype(BF16), vb_ref[n_late], preferred_element_type=F32)
                   + jnp.dot(a_early.astype(BF16), v_early, preferred_element_type=F32))
        acc = contrib if acc is None else acc + contrib
        return acc, left + row_sum(om_early)

    tiles = [q_tiles * step + k for k in range(q_tiles)]
    qbs = [q_ref[0, 0, k * blk:(k + 1) * blk, :].astype(BF16) for k in range(q_tiles)]
    firsts = [pair_step(qb, qi, True, None, None) for qb, qi in zip(qbs, tiles)]

    for k, (qb, qi, (acc0, carry0)) in enumerate(zip(qbs, tiles, firsts)):
        n_pairs = qi // 2

        def cond(state, n_pairs=n_pairs):
            pair, _, _, carry_max = state
            return jnp.logical_and(pair < n_pairs, carry_max > EXP_ZERO)

        def body(state, qb=qb, qi=qi):
            pair, acc, carry, _ = state
            acc, carry = pair_step(qb, qi - 2 - 2 * pair, False, acc, carry)
            return pair + 1, acc, carry, jnp.max(carry)

        _, acc, _, _ = lax.while_loop(cond, body, (jnp.int32(0), acc0, carry0, jnp.max(carry0)))
        o_ref[0, k * blk:(k + 1) * blk, :] = acc


def stick_breaking_attention(qkv, *, q_head0, k_head0, v_head0, n_heads, blk=256, q_tiles=4):
    b, _, t, dh = qkv.shape
    assert t % (blk * q_tiles) == 0
    nb = t // blk
    tq = blk * q_tiles
    kern = functools.partial(_sb_kernel, nb=nb, blk=blk, q_tiles=q_tiles, scale=dh ** -0.5)
    return pl.pallas_call(
        kern,
        out_shape=jax.ShapeDtypeStruct((b, t, n_heads * dh), F32),
        grid=(b, n_heads, t // tq),
        in_specs=[
            pl.BlockSpec((1, 1, tq, dh), lambda bi, h, qi: (bi, q_head0 + h, qi, 0)),
            pl.BlockSpec((1, 1, t, dh), lambda bi, h, qi: (bi, k_head0 + h, 0, 0)),
            pl.BlockSpec((1, 1, t, dh), lambda bi, h, qi: (bi, v_head0 + h, 0, 0)),
        ],
        out_specs=pl.BlockSpec((1, tq, dh), lambda bi, h, qi: (bi, qi, h)),
        scratch_shapes=[
            pltpu.VMEM((nb, blk, dh), BF16),
            pltpu.VMEM((nb, blk, dh), BF16),
        ],
        compiler_params=_params("parallel", "parallel", "arbitrary"),
    )(qkv, qkv, qkv)


def _xattn_kernel(q_ref, kv_ref, o_ref, *, n_heads, dh, scale):
    width = n_heads * dh
    for hh in range(n_heads):
        qh = q_ref[0, :, hh * dh:(hh + 1) * dh].astype(BF16)
        kh = kv_ref[0, :, hh * dh:(hh + 1) * dh].astype(BF16)
        vh = kv_ref[0, :, width + hh * dh:width + (hh + 1) * dh].astype(BF16)
        s = lax.dot_general(qh, kh, _NT, preferred_element_type=F32) * scale
        e = jnp.exp(s - jnp.max(s, axis=-1, keepdims=True))
        p = e / jnp.sum(e, axis=-1, keepdims=True)
        o_ref[0, :, hh * dh:(hh + 1) * dh] = jnp.dot(p.astype(BF16), vh, preferred_element_type=F32)


def cross_attention_core(q, kv, *, n_heads, tq=512):
    b, t, width = q.shape
    mlen = kv.shape[1]
    dh = width // n_heads
    tq = min(tq, t)
    kern = functools.partial(_xattn_kernel, n_heads=n_heads, dh=dh, scale=dh ** -0.5)
    return pl.pallas_call(
        kern,
        out_shape=jax.ShapeDtypeStruct((b, t, width), F32),
        grid=(b, t // tq),
        in_specs=[
            pl.BlockSpec((1, tq, width), lambda bi, i: (bi, i, 0)),
            pl.BlockSpec((1, mlen, 2 * width), lambda bi, i: (bi, 0, 0)),
        ],
        out_specs=pl.BlockSpec((1, tq, width), lambda bi, i: (bi, i, 0)),
        compiler_params=_params("parallel", "arbitrary"),
    )(q, kv)


def _extract_topk(work, n_rounds, *, exact, want_rank=False, n_masked=0):
    rows = lax.broadcasted_iota(jnp.int32, work.shape, 0).astype(F32)
    n_rows = float(work.shape[0])
    rank = jnp.full(work.shape, float(n_rounds), F32) if want_rank else None
    vals, firsts = [], []
    for r in range(n_rounds):
        mx = jnp.max(work, axis=0, keepdims=True)
        if exact:
            first = jnp.min(jnp.where(work == mx, rows, n_rows), axis=0, keepdims=True)
            hit = rows == first
            firsts.append(first)
        else:
            hit = work == mx
        work = jnp.where(hit, -jnp.inf, work)
        if want_rank:
            rank = jnp.where(hit, float(r), rank)
        vals.append(mx)
    if exact:
        tie = jnp.zeros_like(vals[0])
    else:
        removed = jnp.sum(jnp.where(work == -jnp.inf, 1.0, 0.0), axis=0, keepdims=True)
        tie = jnp.where(removed == float(n_rounds + n_masked), 0.0, 1.0)
    return vals, firsts, work, rank, tie


def _staircase_candidates(vals0, vals1, topk):
    sub = 8
    val1_all = jnp.concatenate(vals1, axis=0)
    val1_head = val1_all[:sub]
    row = lax.broadcasted_iota(jnp.int32, val1_head.shape, 0)
    slabs = [val1_all + vals0[0]]
    n_masked = 0
    for a in range(1, sub):
        b_max = topk // (a + 1) - 1
        slab = val1_head + vals0[a]
        if b_max < sub - 1:
            slab = jnp.where(row <= b_max, slab, -jnp.inf)
            n_masked += sub - 1 - b_max
        slabs.append(slab)
    slabs.append(jnp.concatenate(vals0[sub:], axis=0) + vals1[0])
    return jnp.concatenate(slabs, axis=0), n_masked


def _route_tables(scores0, scores1, topk, exact):
    vals0, firsts0, left0, _, tie0 = _extract_topk(scores0, topk, exact=exact)
    vals1, _, left1, rank1, tie1 = _extract_topk(scores1, topk, exact=exact, want_rank=True)
    cand, n_masked = _staircase_candidates(vals0, vals1, topk)
    tops, firsts2, _, _, tie2 = _extract_topk(cand, topk, exact=exact, n_masked=n_masked)
    tau = tops[topk - 1]
    z = jnp.zeros_like(tau)
    for tv in tops:
        z = z + jnp.exp(tv - tops[0])
    val1_all = jnp.concatenate(vals1, axis=0)
    rows = lax.broadcasted_iota(jnp.int32, scores0.shape, 0).astype(F32)
    count = jnp.zeros(scores0.shape, F32)
    sub = 8
    for a in range(topk):
        if exact:
            lo = 0 if a == 0 else (topk + (a - 1) * sub if a < sub else topk + (sub - 1) * sub + a - sub)
            hi = lo + (topk if a == 0 else (sub if a < sub else 1))
            n_sel = jnp.zeros_like(tau)
            for f in firsts2:
                n_sel = n_sel + jnp.where(f >= float(lo), jnp.where(f < float(hi), 1.0, 0.0), 0.0)
            at_rank_a = rows == firsts0[a]
        else:
            n_sel = jnp.sum(jnp.where(val1_all + vals0[a] >= tau, 1.0, 0.0), axis=0, keepdims=True)
            at_rank_a = scores0 == vals0[a]
        count = jnp.where(at_rank_a, n_sel, count)
    e0 = jnp.where(left0 == -jnp.inf, jnp.exp(scores0 - vals0[0]), 0.0) / z
    e1 = jnp.where(left1 == -jnp.inf, jnp.exp(scores1 - vals1[0]), 0.0)
    tie = jnp.maximum(jnp.maximum(tie0, tie1), tie2)
    return count, e0, rank1.astype(BF16), e1.astype(BF16), tie


def _peer_select_kernel(q_ref, keys_ref, cnt_ref, e0_ref, rank_ref, e1_ref, *, topk, heads):
    def head_scores(hh):
        out = []
        for p in range(2):
            c0 = (2 * hh + p) * LANES
            qs = q_ref[:, c0:c0 + LANES].astype(BF16)
            out.append(lax.dot_general(keys_ref[2 * hh + p], qs, _NT,
                                       preferred_element_type=F32))
        return out

    def write_tables(exact):
        tie = None
        for hh in range(heads):
            s0, s1 = head_scores(hh)
            cnt_ref[hh], e0_ref[hh], rank_ref[hh], e1_ref[hh], t = _route_tables(s0, s1, topk, exact)
            tie = t if tie is None else jnp.maximum(tie, t)
        return tie

    tie = write_tables(exact=False)

    @pl.when(jnp.max(tie) > 0.0)
    def _redo_exact():
        write_tables(exact=True)


def peer_select(q, keys_bf16, *, tm=256, heads_per_step=4):
    n = q.shape[0]
    n_heads = keys_bf16.shape[0] // 2
    n_keys = keys_bf16.shape[1]
    tm = min(tm, n)
    hs = heads_per_step
    assert n_heads % hs == 0
    big_f32 = jax.ShapeDtypeStruct((n_heads, n_keys, n), F32)
    big_bf16 = jax.ShapeDtypeStruct((n_heads, n_keys, n), BF16)
    big_spec = pl.BlockSpec((hs, n_keys, tm), lambda i, h: (h, 0, i))
    kern = functools.partial(_peer_select_kernel, topk=PEER_TOPK, heads=hs)
    return pl.pallas_call(
        kern,
        out_shape=(big_f32, big_f32, big_bf16, big_bf16),
        grid=(n // tm, n_heads // hs),
        in_specs=[
            pl.BlockSpec((tm, hs * 2 * LANES), lambda i, h: (i, h)),
            pl.BlockSpec((hs * 2, n_keys, keys_bf16.shape[2]), lambda i, h: (h, 0, 0)),
        ],
        out_specs=(big_spec, big_spec, big_spec, big_spec),
        compiler_params=_params("parallel", "arbitrary"),
    )(q, keys_bf16)


def _peer_main_kernel(x_ref, g_ref, u_ref, v_ref, cnt_ref, e0_ref, rank_ref, e1_ref, o_ref,
                      h_ref, gate_ref, *, n_heads, n_keys, row_chunk):
    e = pl.program_id(1)
    tm = h_ref.shape[0]
    te = u_ref.shape[0]
    groups = te // n_keys

    @pl.when(e == 0)
    def _prologue():
        for r0 in range(0, tm, row_chunk):
            x = x_ref[r0:r0 + row_chunk, :]
            ms = jnp.mean(x * x, axis=-1, keepdims=True)
            h_ref[r0:r0 + row_chunk, :] = ((x * lax.rsqrt(ms + RMS_EPS)) * g_ref[...]).astype(BF16)
            o_ref[r0:r0 + row_chunk, :] = x

    act = lax.dot_general(u_ref[...], h_ref[...], _NT, preferred_element_type=F32)
    for gi in range(groups):
        i = e * groups + gi
        route = jnp.zeros((n_keys, tm), BF16)
        for h in range(n_heads):
            cnt = cnt_ref[h, pl.ds(i, 1), :].astype(BF16)
            w0 = e0_ref[h, pl.ds(i, 1), :].astype(BF16)
            route = route + jnp.where(rank_ref[h] < cnt, e1_ref[h] * w0, jnp.zeros((), BF16))
        a = act[gi * n_keys:(gi + 1) * n_keys, :]
        gelu = 0.5 * a * (1.0 + lax.erf(a * np.float32(np.sqrt(0.5))))
        gate_ref[gi * n_keys:(gi + 1) * n_keys, :] = gelu.astype(BF16) * route
    o_ref[...] += lax.dot_general(gate_ref[...], v_ref[...], _TN, preferred_element_type=F32)


def peer_main(x, g, u_bf16, v_bf16, count, e0, rank, e1, *, tm=512, te=512):
    n, d = x.shape
    n_exp = u_bf16.shape[0]
    n_heads, n_keys, _ = count.shape
    tm = min(tm, n)
    assert n % tm == 0 and n_exp % te == 0 and te % n_keys == 0 and tm % LANES == 0
    once = pl.Buffered(1)
    big_spec = pl.BlockSpec((n_heads, n_keys, tm), lambda i, e: (0, 0, i))
    kern = functools.partial(_peer_main_kernel, n_heads=n_heads, n_keys=n_keys,
                             row_chunk=min(128, tm))
    return pl.pallas_call(
        kern,
        out_shape=jax.ShapeDtypeStruct((n, d), F32),
        grid=(n // tm, n_exp // te),
        in_specs=[
            pl.BlockSpec((tm, d), lambda i, e: (i, 0), pipeline_mode=once),
            pl.BlockSpec((1, d), lambda i, e: (0, 0)),
            pl.BlockSpec((te, d), lambda i, e: (e, 0)),
            pl.BlockSpec((te, d), lambda i, e: (e, 0)),
            big_spec, big_spec, big_spec, big_spec,
        ],
        out_specs=pl.BlockSpec((tm, d), lambda i, e: (i, 0), pipeline_mode=once),
        scratch_shapes=[
            pltpu.VMEM((tm, d), BF16),
            pltpu.VMEM((te, tm), BF16),
        ],
        compiler_params=_params("parallel", "arbitrary"),
    )(x, g.reshape(1, d).astype(F32), u_bf16, v_bf16, count, e0, rank, e1)


def _layer(x, mem, norm_mix_g, w_in, moba_q_norm_g, moba_k_norm_g, moba_out_norm_g,
           sb_out_norm_g, w_out, norm_xattn_g, norm_mem_g, w_xq, w_xkv, xattn_q_norm_g,
           xattn_k_norm_g, w_xo, norm_ffn_g, w_peer_q, peer_sub_keys, peer_u, peer_v):
    b, t, d = x.shape
    n = b * t
    xf = x.reshape(n, d)
    mix_width = w_in.shape[1] // 3
    n_heads = mix_width // (2 * HEAD_DIM)
    grp = n_heads * HEAD_DIM
    slopes = jnp.asarray(2.0 ** (-8.0 * np.arange(1, n_heads + 1) / n_heads), dtype=F32)

    col_gain = jnp.concatenate([jnp.tile(moba_q_norm_g, n_heads), jnp.tile(moba_k_norm_g, n_heads),
                                jnp.ones((w_in.shape[1] - 2 * grp,), F32)])
    qkv = norm_matmul([xf], [norm_mix_g], w_in.astype(BF16), col_gain=col_gain,
                      n_norm_cols=2 * grp, head_major_bt=(b, t), tn=1024)
    qk_norm_bound = HEAD_DIM * jnp.max(jnp.abs(moba_q_norm_g)) * jnp.max(jnp.abs(moba_k_norm_g))
    o_moba = moba_attention(qkv, slopes, qk_norm_bound, q_head0=0, k_head0=n_heads,
                            v_head0=2 * n_heads, n_heads=n_heads)
    o_sb = stick_breaking_attention(qkv, q_head0=3 * n_heads, k_head0=4 * n_heads,
                                    v_head0=5 * n_heads, n_heads=n_heads)
    x1 = norm_matmul([o_moba.reshape(n, grp), o_sb.reshape(n, grp)],
                     [moba_out_norm_g, sb_out_norm_g], w_out.astype(BF16), residual=xf, tn=1024)

    xw = w_xq.shape[1]
    n_mem = mem.shape[1]
    kv_gain = jnp.concatenate([jnp.tile(xattn_k_norm_g, N_XATTN_HEADS), jnp.ones((xw,), F32)])
    kv = norm_matmul([mem.reshape(b * n_mem, d)], [norm_mem_g], w_xkv.astype(BF16),
                     col_gain=kv_gain, n_norm_cols=xw)
    xq = norm_matmul([x1], [norm_xattn_g], w_xq.astype(BF16),
                     col_gain=jnp.tile(xattn_q_norm_g, N_XATTN_HEADS), n_norm_cols=xw)
    o_x = cross_attention_core(xq.reshape(b, t, xw), kv.reshape(b, n_mem, 2 * xw),
                               n_heads=N_XATTN_HEADS)
    x2 = norm_matmul([o_x.reshape(n, xw)], [None], w_xo.astype(BF16), residual=x1, tn=2048)

    pq = norm_matmul([x2], [norm_ffn_g], w_peer_q.astype(BF16), tn=1024)
    keys = peer_sub_keys.reshape(PEER_HEADS * 2, PEER_N_KEYS, -1).astype(BF16)
    count, e0, rank, e1 = peer_select(pq, keys)
    x3 = peer_main(x2, norm_ffn_g, peer_u.astype(BF16), peer_v.astype(BF16), count, e0, rank, e1)
    return x3.reshape(b, t, d)


def kernel(x, mem, norm_mix_g, w_in, moba_q_norm_g, moba_k_norm_g, moba_out_norm_g, sb_out_norm_g,
           w_out, norm_xattn_g, norm_mem_g, w_xq, w_xkv, xattn_q_norm_g, xattn_k_norm_g, w_xo,
           norm_ffn_g, w_peer_q, peer_sub_keys, peer_u, peer_v):
    depth = w_in.shape[0]
    for l in range(depth):
        x = _layer(x, mem, norm_mix_g[l], w_in[l], moba_q_norm_g[l], moba_k_norm_g[l],
                   moba_out_norm_g[l], sb_out_norm_g[l], w_out[l], norm_xattn_g[l], norm_mem_g[l],
                   w_xq[l], w_xkv[l], xattn_q_norm_g[l], xattn_k_norm_g[l], w_xo[l], norm_ffn_g[l],
                   w_peer_q[l], peer_sub_keys[l], peer_u[l], peer_v[l])
    return x
```

```python
import functools

import numpy as np
import jax
import jax.numpy as jnp
from jax import lax
from jax.experimental import pallas as pl
from jax.experimental.pallas import tpu as pltpu

HEAD_DIM = 128
MOBA_BLOCK = 256
MOBA_TOPK = 3
N_XATTN_HEADS = 4
PEER_HEADS = 8
PEER_N_KEYS = 128
PEER_TOPK = 16
RMS_EPS = 1e-6
NEG_INF = -1e30
EXP_ZERO = -110.0
EXP2_ZERO = -160.0
LOG2E = 1.4426950408889634
NORM_MARGIN = 1.02

LANES = 128
VMEM_LIMIT = 56 * 1024 * 1024

F32 = jnp.float32
BF16 = jnp.bfloat16

_NT = (((1,), (1,)), ((), ()))
_TN = (((0,), (0,)), ((), ()))


def _params(*sem):
    return pltpu.CompilerParams(dimension_semantics=sem, vmem_limit_bytes=VMEM_LIMIT)


def _split_bf16(x):
    hi = x.astype(BF16)
    lo = (x - hi.astype(F32)).astype(BF16)
    return hi, lo


def _norm_matmul_kernel(*refs, k_sizes, has_gain, n_norm_tiles, n_col_tiles, has_res,
                        head_major, row_chunk):
    n_in = len(k_sizes)
    pos = 0
    x_refs = refs[pos:pos + n_in]; pos += n_in
    g_refs = []
    for hg in has_gain:
        if hg:
            g_refs.append(refs[pos]); pos += 1
        else:
            g_refs.append(None)
    w_ref = refs[pos]; pos += 1
    cg_ref = None
    if n_norm_tiles > 0:
        cg_ref = refs[pos]; pos += 1
    res_ref = None
    if has_res:
        res_ref = refs[pos]; pos += 1
    o_ref = refs[pos]; pos += 1
    h_ref = refs[pos]

    j = pl.program_id(1)
    tm = h_ref.shape[0]

    @pl.when(j == 0)
    def _prologue():
        off = 0
        for x_ref, g_ref, ksz in zip(x_refs, g_refs, k_sizes):
            for r0 in range(0, tm, row_chunk):
                x = x_ref[r0:r0 + row_chunk, :]
                if g_ref is not None:
                    ms = jnp.mean(x * x, axis=-1, keepdims=True)
                    x = (x * lax.rsqrt(ms + RMS_EPS)) * g_ref[...]
                h_ref[r0:r0 + row_chunk, off:off + ksz] = x.astype(BF16)
            off += ksz

    acc = jnp.dot(h_ref[...], w_ref[...], preferred_element_type=F32)
    tn = acc.shape[1]

    def finish(get_group):
        for hh in range(tn // LANES):
            y = get_group(hh)
            if res_ref is not None:
                y = y + res_ref[:, hh * LANES:(hh + 1) * LANES]
            if head_major:
                o_ref[0, hh] = y
            else:
                o_ref[:, hh * LANES:(hh + 1) * LANES] = y

    def plain_group(hh):
        return acc[:, hh * LANES:(hh + 1) * LANES]

    def normed_group(hh):
        a = acc[:, hh * LANES:(hh + 1) * LANES]
        ms = jnp.mean(a * a, axis=-1, keepdims=True)
        return (a * lax.rsqrt(ms + RMS_EPS)) * cg_ref[:, hh * LANES:(hh + 1) * LANES]

    if n_norm_tiles == 0:
        finish(plain_group)
    elif n_norm_tiles >= n_col_tiles:
        finish(normed_group)
    else:
        @pl.when(j < n_norm_tiles)
        def _():
            finish(normed_group)

        @pl.when(j >= n_norm_tiles)
        def _():
            finish(plain_group)


def norm_matmul(xs, gains, w, *, col_gain=None, n_norm_cols=0, residual=None,
                head_major_bt=None, tm=512, tn=512):
    m = xs[0].shape[0]
    k_sizes = tuple(int(x.shape[1]) for x in xs)
    k_total = sum(k_sizes)
    n_cols = w.shape[1]
    tm = min(tm, m)
    tn = min(tn, n_cols)
    assert m % tm == 0 and n_cols % tn == 0 and w.shape[0] == k_total
    assert n_norm_cols % tn == 0
    n_col_tiles = n_cols // tn
    n_norm_tiles = n_norm_cols // tn
    has_gain = tuple(g is not None for g in gains)

    args, in_specs = [], []
    for x, ksz in zip(xs, k_sizes):
        args.append(x)
        in_specs.append(pl.BlockSpec((tm, ksz), lambda i, j: (i, 0)))
    for g, ksz in zip(gains, k_sizes):
        if g is not None:
            args.append(g.reshape(1, ksz).astype(F32))
            in_specs.append(pl.BlockSpec((1, ksz), lambda i, j: (0, 0)))
    args.append(w)
    in_specs.append(pl.BlockSpec((k_total, tn), lambda i, j: (0, j)))
    if n_norm_tiles > 0:
        args.append(col_gain.reshape(1, n_cols).astype(F32))
        in_specs.append(pl.BlockSpec((1, tn), lambda i, j: (0, j)))
    if residual is not None:
        args.append(residual)
        in_specs.append(pl.BlockSpec((tm, tn), lambda i, j: (i, j)))

    if head_major_bt is not None:
        b, t = head_major_bt
        assert b * t == m and t % tm == 0
        tiles_per_b = t // tm
        out_shape = jax.ShapeDtypeStruct((b, n_cols // LANES, t, LANES), F32)
        out_spec = pl.BlockSpec((1, tn // LANES, tm, LANES),
                                lambda i, j: (i // tiles_per_b, j, i % tiles_per_b, 0))
    else:
        out_shape = jax.ShapeDtypeStruct((m, n_cols), F32)
        out_spec = pl.BlockSpec((tm, tn), lambda i, j: (i, j))

    kern = functools.partial(
        _norm_matmul_kernel, k_sizes=k_sizes, has_gain=has_gain, n_norm_tiles=n_norm_tiles,
        n_col_tiles=n_col_tiles, has_res=residual is not None,
        head_major=head_major_bt is not None, row_chunk=min(128, tm))
    return pl.pallas_call(
        kern,
        out_shape=out_shape,
        grid=(m // tm, n_col_tiles),
        in_specs=in_specs,
        out_specs=out_spec,
        scratch_shapes=[pltpu.VMEM((tm, k_total), BF16)],
        compiler_params=_params("parallel", "arbitrary"),
    )(*args)


def _select_blocks(kmean, q_tile, tile, nb):
    blk = q_tile.shape[0]
    if tile == 0:
        return jnp.zeros((nb, blk), F32)
    kh, kl = _split_bf16(kmean)
    qh, ql = _split_bf16(q_tile)
    gate = (lax.dot_general(kh, qh, _NT, preferred_element_type=F32)
            + lax.dot_general(kh, ql, _NT, preferred_element_type=F32)
            + lax.dot_general(kl, qh, _NT, preferred_element_type=F32))
    blkid = lax.broadcasted_iota(jnp.int32, (nb, blk), 0)
    valid = blkid < tile
    gate = jnp.where(valid, gate, NEG_INF)
    terms = []
    for mth in range(tile):
        gm = gate[mth:mth + 1, :]
        terms.append(jnp.where(gm > gate, 1.0,
                               jnp.where(gm == gate, jnp.where(blkid > mth, 1.0, 0.0), 0.0)))
    while len(terms) > 1:
        terms = [terms[k] + terms[k + 1] for k in range(0, len(terms) - 1, 2)] + terms[len(terms) & ~1:]
    return jnp.where(valid, jnp.where(terms[0] < MOBA_TOPK, 1.0, 0.0), 0.0)


def _moba_kernel(slopes_ref, first_ref, q_ref, qall_ref, k_ref, v_ref, o_ref,
                 kb_ref, vt_ref, kmean_ref, sel_all_ref, m_ref, l_ref, acc_ref, *s_refs,
                 nb, blk, q_tiles, scale):
    h = pl.program_id(1)
    step = pl.program_id(2)

    scale2 = scale * LOG2E
    slope2 = slopes_ref[h] * LOG2E

    @pl.when(step == 0)
    def _prep():
        for n in range(nb):
            kblk = k_ref[0, 0, n * blk:(n + 1) * blk, :]
            kb_ref[n] = kblk.astype(BF16)
            kmean_ref[n:n + 1, :] = jnp.mean(kblk, axis=0, keepdims=True)
            vt_ref[n] = v_ref[0, 0, n * blk:(n + 1) * blk, :].T.astype(BF16)
        kmean = kmean_ref[...]
        for tile in range(nb):
            sel_all_ref[tile] = _select_blocks(
                kmean, qall_ref[0, 0, tile * blk:(tile + 1) * blk, :], tile, nb)

    t2 = lax.broadcasted_iota(jnp.int32, (2 * blk, blk), 1)
    j2 = lax.broadcasted_iota(jnp.int32, (2 * blk, blk), 0)
    bias_pair = (t2 - j2).astype(F32) * (-slope2)
    t_idx = lax.broadcasted_iota(jnp.int32, (blk, blk), 1)
    j_idx = lax.broadcasted_iota(jnp.int32, (blk, blk), 0)
    dist_own = (t_idx - j_idx).astype(F32)
    last_pair = nb // 2 - 1

    first_pair = first_ref[h, q_tiles * step]
    end_pair = (q_tiles * step + q_tiles) // 2

    def tile_stages(k):
        qi = q_tiles * step + k
        qb = q_ref[0, 0, k * blk:(k + 1) * blk, :].astype(BF16)
        sel_ref = sel_all_ref.at[qi]
        m_k, l_k, acc_k, s_k = m_ref.at[k], l_ref.at[k], acc_ref.at[k], s_refs[k]

        def pair_scores(pair):
            n0 = 2 * pair
            sel0 = sel_ref[pl.ds(n0, 1), :]
            sel1 = sel_ref[pl.ds(n0 + 1, 1), :]
            kpair = jnp.concatenate([kb_ref[n0], kb_ref[n0 + 1]], axis=0)
            s = lax.dot_general(kpair, qb, _NT, preferred_element_type=F32) * scale2 + bias_pair
            return jnp.concatenate([jnp.where(sel0 > 0, s[:blk], NEG_INF),
                                    jnp.where(sel1 > 0, s[blk:], NEG_INF)], axis=0)

        def start():
            s_k[first_pair % 2] = pair_scores(jnp.minimum(first_pair, last_pair))
            s = lax.dot_general(kb_ref[qi], qb, _NT, preferred_element_type=F32) * scale2
            s = s - slope2 * dist_own
            s = jnp.where(dist_own >= 0, s, NEG_INF)
            m0 = jnp.max(s, axis=0, keepdims=True)
            p = jnp.exp2(s - m0)
            m_k[...] = m0
            l_k[...] = jnp.sum(p, axis=0, keepdims=True)
            acc_k[...] = jnp.dot(vt_ref[qi], p.astype(BF16), preferred_element_type=F32)

        def sweep_step(pair):
            n0 = 2 * pair
            s = s_k[pair % 2]
            s_k[(pair + 1) % 2] = pair_scores(jnp.minimum(pair + 1, last_pair))
            off = slope2 * ((qi - n0) * blk).astype(F32)
            m_old = m_k[...]
            m_new = jnp.maximum(m_old, jnp.max(s, axis=0, keepdims=True) - off)
            alpha = jnp.exp2(m_old - m_new)
            p = jnp.exp2(s - (m_new + off))
            pb = p.astype(BF16)
            l_k[...] = alpha * l_k[...] + jnp.sum(p, axis=0, keepdims=True)
            acc_k[...] = (alpha * acc_k[...]
                          + jnp.dot(vt_ref[n0], pb[:blk], preferred_element_type=F32)
                          + jnp.dot(vt_ref[n0 + 1], pb[blk:], preferred_element_type=F32))
            m_k[...] = m_new

        def finish():
            o_ref[0, k * blk:(k + 1) * blk, :] = (acc_k[...] / l_k[...]).T

        return start, sweep_step, finish

    tiles = [tile_stages(k) for k in range(q_tiles)]
    for start, _, _ in tiles:
        start()

    def sweep_all(pair, carry):
        for _, sweep_step, _ in tiles:
            sweep_step(pair)
        return carry

    lax.fori_loop(first_pair, end_pair, sweep_all, 0)
    for _, _, finish in tiles:
        finish()


def _moba_first_pairs(slopes, qk_norm_bound, scale, nb, blk):
    slope2 = slopes * LOG2E
    room = 2.0 * (qk_norm_bound * NORM_MARGIN * scale * LOG2E) - EXP2_ZERO
    d = jnp.arange(nb, dtype=F32)
    n_live = jnp.sum(d[None, :] * (slope2[:, None] * blk) + slope2[:, None] <= room, axis=1)
    tiles = jnp.arange(nb, dtype=jnp.int32)
    return jnp.maximum(tiles[None, :] - n_live[:, None].astype(jnp.int32), 0) // 2


def moba_attention(qkv, slopes, qk_norm_bound, *, q_head0, k_head0, v_head0, n_heads, q_tiles=4):
    b, _, t, dh = qkv.shape
    blk = MOBA_BLOCK
    assert t % (2 * blk) == 0
    nb = t // blk
    assert nb % q_tiles == 0
    tq = q_tiles * blk
    scale = dh ** -0.5
    first_pairs = _moba_first_pairs(slopes, qk_norm_bound, scale, nb, blk)
    kern = functools.partial(_moba_kernel, nb=nb, blk=blk, q_tiles=q_tiles, scale=scale)
    return pl.pallas_call(
        kern,
        out_shape=jax.ShapeDtypeStruct((b, t, n_heads * dh), F32),
        grid=(b, n_heads, nb // q_tiles),
        in_specs=[
            pl.BlockSpec(memory_space=pltpu.SMEM),
            pl.BlockSpec(memory_space=pltpu.SMEM),
            pl.BlockSpec((1, 1, tq, dh), lambda bi, h, qi: (bi, q_head0 + h, qi, 0)),
            pl.BlockSpec((1, 1, t, dh), lambda bi, h, qi: (bi, q_head0 + h, 0, 0)),
            pl.BlockSpec((1, 1, t, dh), lambda bi, h, qi: (bi, k_head0 + h, 0, 0)),
            pl.BlockSpec((1, 1, t, dh), lambda bi, h, qi: (bi, v_head0 + h, 0, 0)),
        ],
        out_specs=pl.BlockSpec((1, tq, dh), lambda bi, h, qi: (bi, qi, h)),
        scratch_shapes=[
            pltpu.VMEM((nb, blk, dh), BF16),
            pltpu.VMEM((nb, dh, blk), BF16),
            pltpu.VMEM((nb, dh), F32),
            pltpu.VMEM((nb, nb, blk), F32),
            pltpu.VMEM((q_tiles, 1, blk), F32),
            pltpu.VMEM((q_tiles, 1, blk), F32),
            pltpu.VMEM((q_tiles, dh, blk), F32),
        ] + [
            pltpu.VMEM((2, 2 * blk, blk), F32) for _ in range(q_tiles)
        ],
        compiler_params=_params("parallel", "parallel", "arbitrary"),
    )(slopes, first_pairs, qkv, qkv, qkv, qkv)


def _sb_kernel(q_ref, k_ref, v_ref, o_ref, kb_ref, vb_ref, *, nb, blk, q_tiles, scale):
    step = pl.program_id(2)

    @pl.when(step == 0)
    def _prep():
        for n in range(nb):
            kb_ref[n] = k_ref[0, 0, n * blk:(n + 1) * blk, :].astype(BF16)
            vb_ref[n] = v_ref[0, 0, n * blk:(n + 1) * blk, :].astype(BF16)

    row = lax.broadcasted_iota(jnp.int32, (blk, blk), 0)
    col = lax.broadcasted_iota(jnp.int32, (blk, blk), 1)
    after = jnp.where(row > col, 1.0, 0.0).astype(BF16)
    strict = col < row

    def log_sigmoids(z):
        t = jnp.log(1.0 + jnp.exp(-jnp.abs(z)))
        log_beta = jnp.minimum(z, 0.0) - t
        return log_beta, log_beta - z

    def sum_after(x):
        hi, lo = _split_bf16(x)
        return (jnp.dot(hi, after, preferred_element_type=F32)
                + jnp.dot(lo, after, preferred_element_type=F32))

    def row_sum(x):
        return jnp.sum(x, axis=1, keepdims=True)

    def pair_step(qb, n_late, late_is_own, acc, carry):
        has_early = n_late >= 1
        n_early = jnp.maximum(n_late - 1, 0)
        kpair = jnp.concatenate([kb_ref[n_early], kb_ref[n_late]], axis=0)
        z = lax.dot_general(qb, kpair, _NT, preferred_element_type=F32) * scale
        log_beta, log_om = log_sigmoids(z)
        om_early, om_late = log_om[:, :blk], log_om[:, blk:]
        if late_is_own:
            om_late = jnp.where(strict, om_late, 0.0)
            sum_late = row_sum(om_late)
            a_late = jnp.where(strict, jnp.exp(log_beta[:, blk:] + sum_after(om_late)), 0.0)
            left = sum_late
        else:
            sum_late = row_sum(om_late)
            a_late = jnp.exp(log_beta[:, blk:] + sum_after(om_late) + carry)
            left = carry + sum_late
        a_early = jnp.exp(log_beta[:, :blk] + sum_after(om_early) + left)
        v_early = vb_ref[n_early]
        v_early = jnp.where(has_early, v_early, jnp.zeros_like(v_early))
        contrib = (jnp.dot(a_late.astype(BF16), vb_ref[n_late], preferred_element_type=F32)
                   + jnp.dot(a_early.astype(BF16), v_early, preferred_element_type=F32))
        acc = contrib if acc is None else acc + contrib
        return acc, left + row_sum(om_early)

    tiles = [q_tiles * step + k for k in range(q_tiles)]
    qbs = [q_ref[0, 0, k * blk:(k + 1) * blk, :].astype(BF16) for k in range(q_tiles)]
    firsts = [pair_step(qb, qi, True, None, None) for qb, qi in zip(qbs, tiles)]

    for k, (qb, qi, (acc0, carry0)) in enumerate(zip(qbs, tiles, firsts)):
        n_pairs = qi // 2

        def cond(state, n_pairs=n_pairs):
            pair, _, _, carry_max = state
            return jnp.logical_and(pair < n_pairs, carry_max > EXP_ZERO)

        def body(state, qb=qb, qi=qi):
            pair, acc, carry, _ = state
            acc, carry = pair_step(qb, qi - 2 - 2 * pair, False, acc, carry)
            return pair + 1, acc, carry, jnp.max(carry)

        _, acc, _, _ = lax.while_loop(cond, body, (jnp.int32(0), acc0, carry0, jnp.max(carry0)))
        o_ref[0, k * blk:(k + 1) * blk, :] = acc


def stick_breaking_attention(qkv, *, q_head0, k_head0, v_head0, n_heads, blk=256, q_tiles=4):
    b, _, t, dh = qkv.shape
    assert t % (blk * q_tiles) == 0
    nb = t // blk
    tq = blk * q_tiles
    kern = functools.partial(_sb_kernel, nb=nb, blk=blk, q_tiles=q_tiles, scale=dh ** -0.5)
    return pl.pallas_call(
        kern,
        out_shape=jax.ShapeDtypeStruct((b, t, n_heads * dh), F32),
        grid=(b, n_heads, t // tq),
        in_specs=[
            pl.BlockSpec((1, 1, tq, dh), lambda bi, h, qi: (bi, q_head0 + h, qi, 0)),
            pl.BlockSpec((1, 1, t, dh), lambda bi, h, qi: (bi, k_head0 + h, 0, 0)),
            pl.BlockSpec((1, 1, t, dh), lambda bi, h, qi: (bi, v_head0 + h, 0, 0)),
        ],
        out_specs=pl.BlockSpec((1, tq, dh), lambda bi, h, qi: (bi, qi, h)),
        scratch_shapes=[
            pltpu.VMEM((nb, blk, dh), BF16),
            pltpu.VMEM((nb, blk, dh), BF16),
        ],
        compiler_params=_params("parallel", "parallel", "arbitrary"),
    )(qkv, qkv, qkv)


def _xattn_kernel(q_ref, kv_ref, o_ref, *, n_heads, dh, scale):
    width = n_heads * dh
    for hh in range(n_heads):
        qh = q_ref[0, :, hh * dh:(hh + 1) * dh].astype(BF16)
        kh = kv_ref[0, :, hh * dh:(hh + 1) * dh].astype(BF16)
        vh = kv_ref[0, :, width + hh * dh:width + (hh + 1) * dh].astype(BF16)
        s = lax.dot_general(qh, kh, _NT, preferred_element_type=F32) * scale
        e = jnp.exp(s - jnp.max(s, axis=-1, keepdims=True))
        p = e / jnp.sum(e, axis=-1, keepdims=True)
        o_ref[0, :, hh * dh:(hh + 1) * dh] = jnp.dot(p.astype(BF16), vh, preferred_element_type=F32)


def cross_attention_core(q, kv, *, n_heads, tq=512):
    b, t, width = q.shape
    mlen = kv.shape[1]
    dh = width // n_heads
    tq = min(tq, t)
    kern = functools.partial(_xattn_kernel, n_heads=n_heads, dh=dh, scale=dh ** -0.5)
    return pl.pallas_call(
        kern,
        out_shape=jax.ShapeDtypeStruct((b, t, width), F32),
        grid=(b, t // tq),
        in_specs=[
            pl.BlockSpec((1, tq, width), lambda bi, i: (bi, i, 0)),
            pl.BlockSpec((1, mlen, 2 * width), lambda bi, i: (bi, 0, 0)),
        ],
        out_specs=pl.BlockSpec((1, tq, width), lambda bi, i: (bi, i, 0)),
        compiler_params=_params("parallel", "arbitrary"),
    )(q, kv)


def _extract_topk(work, n_rounds, *, exact, want_rank=False, n_masked=0):
    rows = lax.broadcasted_iota(jnp.int32, work.shape, 0).astype(F32)
    n_rows = float(work.shape[0])
    rank = jnp.full(work.shape, float(n_rounds), F32) if want_rank else None
    vals, firsts = [], []
    for r in range(n_rounds):
        mx = jnp.max(work, axis=0, keepdims=True)
        if exact:
            first = jnp.min(jnp.where(work == mx, rows, n_rows), axis=0, keepdims=True)
            hit = rows == first
            firsts.append(first)
        else:
            hit = work == mx
        work = jnp.where(hit, -jnp.inf, work)
        if want_rank:
            rank = jnp.where(hit, float(r), rank)
        vals.append(mx)
    if exact:
        tie = jnp.zeros_like(vals[0])
    else:
        removed = jnp.sum(jnp.where(work == -jnp.inf, 1.0, 0.0), axis=0, keepdims=True)
        tie = jnp.where(removed == float(n_rounds + n_masked), 0.0, 1.0)
    return vals, firsts, work, rank, tie


def _staircase_candidates(vals0, vals1, topk):
    sub = 8
    val1_all = jnp.concatenate(vals1, axis=0)
    val1_head = val1_all[:sub]
    row = lax.broadcasted_iota(jnp.int32, val1_head.shape, 0)
    slabs = [val1_all + vals0[0]]
    n_masked = 0
    for a in range(1, sub):
        b_max = topk // (a + 1) - 1
        slab = val1_head + vals0[a]
        if b_max < sub - 1:
            slab = jnp.where(row <= b_max, slab, -jnp.inf)
            n_masked += sub - 1 - b_max
        slabs.append(slab)
    slabs.append(jnp.concatenate(vals0[sub:], axis=0) + vals1[0])
    return jnp.concatenate(slabs, axis=0), n_masked


def _route_tables(scores0, scores1, topk, exact):
    vals0, firsts0, left0, _, tie0 = _extract_topk(scores0, topk, exact=exact)
    vals1, _, left1, rank1, tie1 = _extract_topk(scores1, topk, exact=exact, want_rank=True)
    cand, n_masked = _staircase_candidates(vals0, vals1, topk)
    tops, firsts2, _, _, tie2 = _extract_topk(cand, topk, exact=exact, n_masked=n_masked)
    tau = tops[topk - 1]
    z = jnp.zeros_like(tau)
    for tv in tops:
        z = z + jnp.exp(tv - tops[0])
    val1_all = jnp.concatenate(vals1, axis=0)
    rows = lax.broadcasted_iota(jnp.int32, scores0.shape, 0).astype(F32)
    count = jnp.zeros(scores0.shape, F32)
    sub = 8
    for a in range(topk):
        if exact:
            lo = 0 if a == 0 else (topk + (a - 1) * sub if a < sub else topk + (sub - 1) * sub + a - sub)
            hi = lo + (topk if a == 0 else (sub if a < sub else 1))
            n_sel = jnp.zeros_like(tau)
            for f in firsts2:
                n_sel = n_sel + jnp.where(f >= float(lo), jnp.where(f < float(hi), 1.0, 0.0), 0.0)
            at_rank_a = rows == firsts0[a]
        else:
            n_sel = jnp.sum(jnp.where(val1_all + vals0[a] >= tau, 1.0, 0.0), axis=0, keepdims=True)
            at_rank_a = scores0 == vals0[a]
        count = jnp.where(at_rank_a, n_sel, count)
    e0 = jnp.where(left0 == -jnp.inf, jnp.exp(scores0 - vals0[0]), 0.0) / z
    e1 = jnp.where(left1 == -jnp.inf, jnp.exp(scores1 - vals1[0]), 0.0)
    tie = jnp.maximum(jnp.maximum(tie0, tie1), tie2)
    return count, e0, rank1.astype(BF16), e1.astype(BF16), tie


def _peer_select_kernel(q_ref, keys_ref, cnt_ref, e0_ref, rank_ref, e1_ref, *, topk, heads):
    def head_scores(hh):
        out = []
        for p in range(2):
            c0 = (2 * hh + p) * LANES
            qs = q_ref[:, c0:c0 + LANES].astype(BF16)
            out.append(lax.dot_general(keys_ref[2 * hh + p], qs, _NT,
                                       preferred_element_type=F32))
        return out

    def write_tables(hh, exact):
        s0, s1 = head_scores(hh)
        cnt_ref[hh], e0_ref[hh], rank_ref[hh], e1_ref[hh], tie = _route_tables(s0, s1, topk, exact)
        return tie

    ties = [write_tables(hh, exact=False) for hh in range(heads)]
    for hh in range(heads):
        @pl.when(jnp.max(ties[hh]) > 0.0)
        def _redo_exact(hh=hh):
            write_tables(hh, exact=True)


def peer_select(q, keys_bf16, *, tm=256, heads_per_step=4):
    n = q.shape[0]
    n_heads = keys_bf16.shape[0] // 2
    n_keys = keys_bf16.shape[1]
    tm = min(tm, n)
    hs = heads_per_step
    assert n_heads % hs == 0
    big_f32 = jax.ShapeDtypeStruct((n_heads, n_keys, n), F32)
    big_bf16 = jax.ShapeDtypeStruct((n_heads, n_keys, n), BF16)
    big_spec = pl.BlockSpec((hs, n_keys, tm), lambda i, h: (h, 0, i))
    kern = functools.partial(_peer_select_kernel, topk=PEER_TOPK, heads=hs)
    return pl.pallas_call(
        kern,
        out_shape=(big_f32, big_f32, big_bf16, big_bf16),
        grid=(n // tm, n_heads // hs),
        in_specs=[
            pl.BlockSpec((tm, hs * 2 * LANES), lambda i, h: (i, h)),
            pl.BlockSpec((hs * 2, n_keys, keys_bf16.shape[2]), lambda i, h: (h, 0, 0)),
        ],
        out_specs=(big_spec, big_spec, big_spec, big_spec),
        compiler_params=_params("parallel", "arbitrary"),
    )(q, keys_bf16)


def _peer_main_kernel(x_ref, g_ref, u_ref, v_ref, cnt_ref, e0_ref, rank_ref, e1_ref, o_ref,
                      h_ref, gate_ref, *, n_heads, n_keys, row_chunk):
    e = pl.program_id(1)
    tm = h_ref.shape[0]
    te = u_ref.shape[0]
    groups = te // n_keys

    @pl.when(e == 0)
    def _prologue():
        for r0 in range(0, tm, row_chunk):
            x = x_ref[r0:r0 + row_chunk, :]
            ms = jnp.mean(x * x, axis=-1, keepdims=True)
            h_ref[r0:r0 + row_chunk, :] = ((x * lax.rsqrt(ms + RMS_EPS)) * g_ref[...]).astype(BF16)
            o_ref[r0:r0 + row_chunk, :] = x

    act = lax.dot_general(u_ref[...], h_ref[...], _NT, preferred_element_type=F32)
    for gi in range(groups):
        i = e * groups + gi
        route = jnp.zeros((n_keys, tm), BF16)
        for h in range(n_heads):
            cnt = cnt_ref[h, pl.ds(i, 1), :].astype(BF16)
            w0 = e0_ref[h, pl.ds(i, 1), :].astype(BF16)
            route = route + jnp.where(rank_ref[h] < cnt, e1_ref[h] * w0, jnp.zeros((), BF16))
        a = act[gi * n_keys:(gi + 1) * n_keys, :]
        gelu = 0.5 * a * (1.0 + lax.erf(a * np.float32(np.sqrt(0.5))))
        gate_ref[gi * n_keys:(gi + 1) * n_keys, :] = gelu.astype(BF16) * route
    o_ref[...] += lax.dot_general(gate_ref[...], v_ref[...], _TN, preferred_element_type=F32)


def peer_main(x, g, u_bf16, v_bf16, count, e0, rank, e1, *, tm=512, te=512):
    n, d = x.shape
    n_exp = u_bf16.shape[0]
    n_heads, n_keys, _ = count.shape
    tm = min(tm, n)
    assert n % tm == 0 and n_exp % te == 0 and te % n_keys == 0 and tm % LANES == 0
    once = pl.Buffered(1)
    big_spec = pl.BlockSpec((n_heads, n_keys, tm), lambda i, e: (0, 0, i))
    kern = functools.partial(_peer_main_kernel, n_heads=n_heads, n_keys=n_keys,
                             row_chunk=min(128, tm))
    return pl.pallas_call(
        kern,
        out_shape=jax.ShapeDtypeStruct((n, d), F32),
        grid=(n // tm, n_exp // te),
        in_specs=[
            pl.BlockSpec((tm, d), lambda i, e: (i, 0), pipeline_mode=once),
            pl.BlockSpec((1, d), lambda i, e: (0, 0)),
            pl.BlockSpec((te, d), lambda i, e: (e, 0)),
            pl.BlockSpec((te, d), lambda i, e: (e, 0)),
            big_spec, big_spec, big_spec, big_spec,
        ],
        out_specs=pl.BlockSpec((tm, d), lambda i, e: (i, 0), pipeline_mode=once),
        scratch_shapes=[
            pltpu.VMEM((tm, d), BF16),
            pltpu.VMEM((te, tm), BF16),
        ],
        compiler_params=_params("parallel", "arbitrary"),
    )(x, g.reshape(1, d).astype(F32), u_bf16, v_bf16, count, e0, rank, e1)


def _layer(x, mem, norm_mix_g, w_in, moba_q_norm_g, moba_k_norm_g, moba_out_norm_g,
           sb_out_norm_g, w_out, norm_xattn_g, norm_mem_g, w_xq, w_xkv, xattn_q_norm_g,
           xattn_k_norm_g, w_xo, norm_ffn_g, w_peer_q, peer_sub_keys, peer_u, peer_v):
    b, t, d = x.shape
    n = b * t
    xf = x.reshape(n, d)
    mix_width = w_in.shape[1] // 3
    n_heads = mix_width // (2 * HEAD_DIM)
    grp = n_heads * HEAD_DIM
    slopes = jnp.asarray(2.0 ** (-8.0 * np.arange(1, n_heads + 1) / n_heads), dtype=F32)

    col_gain = jnp.concatenate([jnp.tile(moba_q_norm_g, n_heads), jnp.tile(moba_k_norm_g, n_heads),
                                jnp.ones((w_in.shape[1] - 2 * grp,), F32)])
    qkv = norm_matmul([xf], [norm_mix_g], w_in.astype(BF16), col_gain=col_gain,
                      n_norm_cols=2 * grp, head_major_bt=(b, t), tn=1024)
    qk_norm_bound = HEAD_DIM * jnp.max(jnp.abs(moba_q_norm_g)) * jnp.max(jnp.abs(moba_k_norm_g))
    o_moba = moba_attention(qkv, slopes, qk_norm_bound, q_head0=0, k_head0=n_heads,
                            v_head0=2 * n_heads, n_heads=n_heads)
    o_sb = stick_breaking_attention(qkv, q_head0=3 * n_heads, k_head0=4 * n_heads,
                                    v_head0=5 * n_heads, n_heads=n_heads)
    x1 = norm_matmul([o_moba.reshape(n, grp), o_sb.reshape(n, grp)],
                     [moba_out_norm_g, sb_out_norm_g], w_out.astype(BF16), residual=xf, tn=1024)

    xw = w_xq.shape[1]
    n_mem = mem.shape[1]
    kv_gain = jnp.concatenate([jnp.tile(xattn_k_norm_g, N_XATTN_HEADS), jnp.ones((xw,), F32)])
    kv = norm_matmul([mem.reshape(b * n_mem, d)], [norm_mem_g], w_xkv.astype(BF16),
                     col_gain=kv_gain, n_norm_cols=xw)
    xq = norm_matmul([x1], [norm_xattn_g], w_xq.astype(BF16),
                     col_gain=jnp.tile(xattn_q_norm_g, N_XATTN_HEADS), n_norm_cols=xw)
    o_x = cross_attention_core(xq.reshape(b, t, xw), kv.reshape(b, n_mem, 2 * xw),
                               n_heads=N_XATTN_HEADS)
    x2 = norm_matmul([o_x.reshape(n, xw)], [None], w_xo.astype(BF16), residual=x1, tn=2048)

    pq = norm_matmul([x2], [norm_ffn_g], w_peer_q.astype(BF16), tn=1024)
    keys = peer_sub_keys.reshape(PEER_HEADS * 2, PEER_N_KEYS, -1).astype(BF16)
    count, e0, rank, e1 = peer_select(pq, keys)
    x3 = peer_main(x2, norm_ffn_g, peer_u.astype(BF16), peer_v.astype(BF16), count, e0, rank, e1)
    return x3.reshape(b, t, d)


def kernel(x, mem, norm_mix_g, w_in, moba_q_norm_g, moba_k_norm_g, moba_out_norm_g, sb_out_norm_g,
           w_out, norm_xattn_g, norm_mem_g, w_xq, w_xkv, xattn_q_norm_g, xattn_k_norm_g, w_xo,
           norm_ffn_g, w_peer_q, peer_sub_keys, peer_u, peer_v):
    depth = w_in.shape[0]
    for l in range(depth):
        x = _layer(x, mem, norm_mix_g[l], w_in[l], moba_q_norm_g[l], moba_k_norm_g[l],
                   moba_out_norm_g[l], sb_out_norm_g[l], w_out[l], norm_xattn_g[l], norm_mem_g[l],
                   w_xq[l], w_xkv[l], xattn_q_norm_g[l], xattn_k_norm_g[l], w_xo[l], norm_ffn_g[l],
                   w_peer_q[l], peer_sub_keys[l], peer_u[l], peer_v[l])
    return x
```

```python
import functools

import numpy as np
import jax
import jax.numpy as jnp
from jax import lax
from jax.experimental import pallas as pl
from jax.experimental.pallas import tpu as pltpu

HEAD_DIM = 128
MOBA_BLOCK = 256
MOBA_TOPK = 3
N_XATTN_HEADS = 4
PEER_HEADS = 8
PEER_N_KEYS = 128
PEER_TOPK = 16
RMS_EPS = 1e-6
NEG_INF = -1e30
EXP_ZERO = -110.0
EXP2_ZERO = -160.0
LOG2E = 1.4426950408889634
NORM_MARGIN = 1.02

LANES = 128
VMEM_LIMIT = 56 * 1024 * 1024

F32 = jnp.float32
BF16 = jnp.bfloat16

_NT = (((1,), (1,)), ((), ()))
_TN = (((0,), (0,)), ((), ()))


def _params(*sem):
    return pltpu.CompilerParams(dimension_semantics=sem, vmem_limit_bytes=VMEM_LIMIT)


def _split_bf16(x):
    hi = x.astype(BF16)
    lo = (x - hi.astype(F32)).astype(BF16)
    return hi, lo


def _norm_matmul_kernel(*refs, k_sizes, has_gain, n_norm_tiles, n_col_tiles, has_res,
                        head_major, row_chunk):
    n_in = len(k_sizes)
    pos = 0
    x_refs = refs[pos:pos + n_in]; pos += n_in
    g_refs = []
    for hg in has_gain:
        if hg:
            g_refs.append(refs[pos]); pos += 1
        else:
            g_refs.append(None)
    w_ref = refs[pos]; pos += 1
    cg_ref = None
    if n_norm_tiles > 0:
        cg_ref = refs[pos]; pos += 1
    res_ref = None
    if has_res:
        res_ref = refs[pos]; pos += 1
    o_ref = refs[pos]; pos += 1
    h_ref = refs[pos]

    j = pl.program_id(1)
    tm = h_ref.shape[0]

    @pl.when(j == 0)
    def _prologue():
        off = 0
        for x_ref, g_ref, ksz in zip(x_refs, g_refs, k_sizes):
            for r0 in range(0, tm, row_chunk):
                x = x_ref[r0:r0 + row_chunk, :]
                if g_ref is not None:
                    ms = jnp.mean(x * x, axis=-1, keepdims=True)
                    x = (x * lax.rsqrt(ms + RMS_EPS)) * g_ref[...]
                h_ref[r0:r0 + row_chunk, off:off + ksz] = x.astype(BF16)
            off += ksz

    acc = jnp.dot(h_ref[...], w_ref[...], preferred_element_type=F32)
    tn = acc.shape[1]

    def finish(get_group):
        for hh in range(tn // LANES):
            y = get_group(hh)
            if res_ref is not None:
                y = y + res_ref[:, hh * LANES:(hh + 1) * LANES]
            if head_major:
                o_ref[0, hh] = y
            else:
                o_ref[:, hh * LANES:(hh + 1) * LANES] = y

    def plain_group(hh):
        return acc[:, hh * LANES:(hh + 1) * LANES]

    def normed_group(hh):
        a = acc[:, hh * LANES:(hh + 1) * LANES]
        ms = jnp.mean(a * a, axis=-1, keepdims=True)
        return (a * lax.rsqrt(ms + RMS_EPS)) * cg_ref[:, hh * LANES:(hh + 1) * LANES]

    if n_norm_tiles == 0:
        finish(plain_group)
    elif n_norm_tiles >= n_col_tiles:
        finish(normed_group)
    else:
        @pl.when(j < n_norm_tiles)
        def _():
            finish(normed_group)

        @pl.when(j >= n_norm_tiles)
        def _():
            finish(plain_group)


def norm_matmul(xs, gains, w, *, col_gain=None, n_norm_cols=0, residual=None,
                head_major_bt=None, tm=512, tn=512):
    m = xs[0].shape[0]
    k_sizes = tuple(int(x.shape[1]) for x in xs)
    k_total = sum(k_sizes)
    n_cols = w.shape[1]
    tm = min(tm, m)
    tn = min(tn, n_cols)
    assert m % tm == 0 and n_cols % tn == 0 and w.shape[0] == k_total
    assert n_norm_cols % tn == 0
    n_col_tiles = n_cols // tn
    n_norm_tiles = n_norm_cols // tn
    has_gain = tuple(g is not None for g in gains)

    args, in_specs = [], []
    for x, ksz in zip(xs, k_sizes):
        args.append(x)
        in_specs.append(pl.BlockSpec((tm, ksz), lambda i, j: (i, 0)))
    for g, ksz in zip(gains, k_sizes):
        if g is not None:
            args.append(g.reshape(1, ksz).astype(F32))
            in_specs.append(pl.BlockSpec((1, ksz), lambda i, j: (0, 0)))
    args.append(w)
    in_specs.append(pl.BlockSpec((k_total, tn), lambda i, j: (0, j)))
    if n_norm_tiles > 0:
        args.append(col_gain.reshape(1, n_cols).astype(F32))
        in_specs.append(pl.BlockSpec((1, tn), lambda i, j: (0, j)))
    if residual is not None:
        args.append(residual)
        in_specs.append(pl.BlockSpec((tm, tn), lambda i, j: (i, j)))

    if head_major_bt is not None:
        b, t = head_major_bt
        assert b * t == m and t % tm == 0
        tiles_per_b = t // tm
        out_shape = jax.ShapeDtypeStruct((b, n_cols // LANES, t, LANES), F32)
        out_spec = pl.BlockSpec((1, tn // LANES, tm, LANES),
                                lambda i, j: (i // tiles_per_b, j, i % tiles_per_b, 0))
    else:
        out_shape = jax.ShapeDtypeStruct((m, n_cols), F32)
        out_spec = pl.BlockSpec((tm, tn), lambda i, j: (i, j))

    kern = functools.partial(
        _norm_matmul_kernel, k_sizes=k_sizes, has_gain=has_gain, n_norm_tiles=n_norm_tiles,
        n_col_tiles=n_col_tiles, has_res=residual is not None,
        head_major=head_major_bt is not None, row_chunk=min(128, tm))
    return pl.pallas_call(
        kern,
        out_shape=out_shape,
        grid=(m // tm, n_col_tiles),
        in_specs=in_specs,
        out_specs=out_spec,
        scratch_shapes=[pltpu.VMEM((tm, k_total), BF16)],
        compiler_params=_params("parallel", "arbitrary"),
    )(*args)


def _select_blocks(kmean, q_tile, tile, nb):
    blk = q_tile.shape[0]
    if tile == 0:
        return jnp.zeros((nb, blk), F32)
    kh, kl = _split_bf16(kmean)
    qh, ql = _split_bf16(q_tile)
    gate = (lax.dot_general(kh, qh, _NT, preferred_element_type=F32)
            + lax.dot_general(kh, ql, _NT, preferred_element_type=F32)
            + lax.dot_general(kl, qh, _NT, preferred_element_type=F32))
    blkid = lax.broadcasted_iota(jnp.int32, (nb, blk), 0)
    valid = blkid < tile
    gate = jnp.where(valid, gate, NEG_INF)
    terms = []
    for mth in range(tile):
        gm = gate[mth:mth + 1, :]
        terms.append(jnp.where(gm > gate, 1.0,
                               jnp.where(gm == gate, jnp.where(blkid > mth, 1.0, 0.0), 0.0)))
    while len(terms) > 1:
        terms = [terms[k] + terms[k + 1] for k in range(0, len(terms) - 1, 2)] + terms[len(terms) & ~1:]
    return jnp.where(valid, jnp.where(terms[0] < MOBA_TOPK, 1.0, 0.0), 0.0)


def _moba_kernel(slopes_ref, first_ref, q_ref, qall_ref, k_ref, v_ref, o_ref,
                 kb_ref, vt_ref, kmean_ref, sel_all_ref, m_ref, l_ref, acc_ref, *s_refs,
                 nb, blk, q_tiles, scale):
    h = pl.program_id(1)
    step = pl.program_id(2)

    scale2 = scale * LOG2E
    slope2 = slopes_ref[h] * LOG2E

    @pl.when(step == 0)
    def _prep():
        for n in range(nb):
            kblk = k_ref[0, 0, n * blk:(n + 1) * blk, :]
            kb_ref[n] = kblk.astype(BF16)
            kmean_ref[n:n + 1, :] = jnp.mean(kblk, axis=0, keepdims=True)
            vt_ref[n] = v_ref[0, 0, n * blk:(n + 1) * blk, :].T.astype(BF16)
        kmean = kmean_ref[...]
        for tile in range(nb):
            sel_all_ref[tile] = _select_blocks(
                kmean, qall_ref[0, 0, tile * blk:(tile + 1) * blk, :], tile, nb)

    t2 = lax.broadcasted_iota(jnp.int32, (2 * blk, blk), 1)
    j2 = lax.broadcasted_iota(jnp.int32, (2 * blk, blk), 0)
    bias_pair = (t2 - j2).astype(F32) * (-slope2)
    t_idx = lax.broadcasted_iota(jnp.int32, (blk, blk), 1)
    j_idx = lax.broadcasted_iota(jnp.int32, (blk, blk), 0)
    dist_own = (t_idx - j_idx).astype(F32)
    last_pair = nb // 2 - 1

    first_pair = first_ref[h, q_tiles * step]
    end_pair = (q_tiles * step + q_tiles) // 2

    def tile_stages(k):
        qi = q_tiles * step + k
        qb = q_ref[0, 0, k * blk:(k + 1) * blk, :].astype(BF16)
        sel_ref = sel_all_ref.at[qi]
        m_k, l_k, acc_k, s_k = m_ref.at[k], l_ref.at[k], acc_ref.at[k], s_refs[k]

        def pair_scores(pair):
            n0 = 2 * pair
            sel0 = sel_ref[pl.ds(n0, 1), :]
            sel1 = sel_ref[pl.ds(n0 + 1, 1), :]
            kpair = jnp.concatenate([kb_ref[n0], kb_ref[n0 + 1]], axis=0)
            s = lax.dot_general(kpair, qb, _NT, preferred_element_type=F32) * scale2 + bias_pair
            return jnp.concatenate([jnp.where(sel0 > 0, s[:blk], NEG_INF),
                                    jnp.where(sel1 > 0, s[blk:], NEG_INF)], axis=0)

        def start():
            s_k[first_pair % 2] = pair_scores(jnp.minimum(first_pair, last_pair))
            s = lax.dot_general(kb_ref[qi], qb, _NT, preferred_element_type=F32) * scale2
            s = s - slope2 * dist_own
            s = jnp.where(dist_own >= 0, s, NEG_INF)
            m0 = jnp.max(s, axis=0, keepdims=True)
            p = jnp.exp2(s - m0)
            m_k[...] = m0
            l_k[...] = jnp.sum(p, axis=0, keepdims=True)
            acc_k[...] = jnp.dot(vt_ref[qi], p.astype(BF16), preferred_element_type=F32)

        def sweep_step(pair):
            n0 = 2 * pair
            s = s_k[pair % 2]
            s_k[(pair + 1) % 2] = pair_scores(jnp.minimum(pair + 1, last_pair))
            off = slope2 * ((qi - n0) * blk).astype(F32)
            m_old = m_k[...]
            m_new = jnp.maximum(m_old, jnp.max(s, axis=0, keepdims=True) - off)
            alpha = jnp.exp2(m_old - m_new)
            p = jnp.exp2(s - (m_new + off))
            pb = p.astype(BF16)
            l_k[...] = alpha * l_k[...] + jnp.sum(p, axis=0, keepdims=True)
            acc_k[...] = (alpha * acc_k[...]
                          + jnp.dot(vt_ref[n0], pb[:blk], preferred_element_type=F32)
                          + jnp.dot(vt_ref[n0 + 1], pb[blk:], preferred_element_type=F32))
            m_k[...] = m_new

        def finish():
            o_ref[0, k * blk:(k + 1) * blk, :] = (acc_k[...] / l_k[...]).T

        return start, sweep_step, finish

    tiles = [tile_stages(k) for k in range(q_tiles)]
    for start, _, _ in tiles:
        start()

    def sweep_all(pair, carry):
        for _, sweep_step, _ in tiles:
            sweep_step(pair)
        return carry

    lax.fori_loop(first_pair, end_pair, sweep_all, 0)
    for _, _, finish in tiles:
        finish()


def _moba_first_pairs(slopes, qk_norm_bound, scale, nb, blk):
    slope2 = slopes * LOG2E
    room = 2.0 * (qk_norm_bound * NORM_MARGIN * scale * LOG2E) - EXP2_ZERO
    d = jnp.arange(nb, dtype=F32)
    n_live = jnp.sum(d[None, :] * (slope2[:, None] * blk) + slope2[:, None] <= room, axis=1)
    tiles = jnp.arange(nb, dtype=jnp.int32)
    return jnp.maximum(tiles[None, :] - n_live[:, None].astype(jnp.int32), 0) // 2


def moba_attention(qkv, slopes, qk_norm_bound, *, q_head0, k_head0, v_head0, n_heads, q_tiles=4):
    b, _, t, dh = qkv.shape
    blk = MOBA_BLOCK
    assert t % (2 * blk) == 0
    nb = t // blk
    assert nb % q_tiles == 0
    tq = q_tiles * blk
    scale = dh ** -0.5
    first_pairs = _moba_first_pairs(slopes, qk_norm_bound, scale, nb, blk)
    kern = functools.partial(_moba_kernel, nb=nb, blk=blk, q_tiles=q_tiles, scale=scale)
    return pl.pallas_call(
        kern,
        out_shape=jax.ShapeDtypeStruct((b, t, n_heads * dh), F32),
        grid=(b, n_heads, nb // q_tiles),
        in_specs=[
            pl.BlockSpec(memory_space=pltpu.SMEM),
            pl.BlockSpec(memory_space=pltpu.SMEM),
            pl.BlockSpec((1, 1, tq, dh), lambda bi, h, qi: (bi, q_head0 + h, qi, 0)),
            pl.BlockSpec((1, 1, t, dh), lambda bi, h, qi: (bi, q_head0 + h, 0, 0)),
            pl.BlockSpec((1, 1, t, dh), lambda bi, h, qi: (bi, k_head0 + h, 0, 0)),
            pl.BlockSpec((1, 1, t, dh), lambda bi, h, qi: (bi, v_head0 + h, 0, 0)),
        ],
        out_specs=pl.BlockSpec((1, tq, dh), lambda bi, h, qi: (bi, qi, h)),
        scratch_shapes=[
            pltpu.VMEM((nb, blk, dh), BF16),
            pltpu.VMEM((nb, dh, blk), BF16),
            pltpu.VMEM((nb, dh), F32),
            pltpu.VMEM((nb, nb, blk), F32),
            pltpu.VMEM((q_tiles, 1, blk), F32),
            pltpu.VMEM((q_tiles, 1, blk), F32),
            pltpu.VMEM((q_tiles, dh, blk), F32),
        ] + [
            pltpu.VMEM((2, 2 * blk, blk), F32) for _ in range(q_tiles)
        ],
        compiler_params=_params("parallel", "parallel", "arbitrary"),
    )(slopes, first_pairs, qkv, qkv, qkv, qkv)


def _sb_kernel(q_ref, k_ref, v_ref, o_ref, kb_ref, vb_ref, *, nb, blk, q_tiles, scale):
    step = pl.program_id(2)

    @pl.when(step == 0)
    def _prep():
        for n in range(nb):
            kb_ref[n] = k_ref[0, 0, n * blk:(n + 1) * blk, :].astype(BF16)
            vb_ref[n] = v_ref[0, 0, n * blk:(n + 1) * blk, :].astype(BF16)

    row = lax.broadcasted_iota(jnp.int32, (blk, blk), 0)
    col = lax.broadcasted_iota(jnp.int32, (blk, blk), 1)
    after = jnp.where(row > col, 1.0, 0.0).astype(BF16)
    strict = col < row

    def log_sigmoids(z):
        t = jnp.log(1.0 + jnp.exp(-jnp.abs(z)))
        log_beta = jnp.minimum(z, 0.0) - t
        return log_beta, log_beta - z

    def sum_after(x):
        hi, lo = _split_bf16(x)
        return (jnp.dot(hi, after, preferred_element_type=F32)
                + jnp.dot(lo, after, preferred_element_type=F32))

    def row_sum(x):
        return jnp.sum(x, axis=1, keepdims=True)

    def pair_step(qb, n_late, late_is_own, acc, carry):
        has_early = n_late >= 1
        n_early = jnp.maximum(n_late - 1, 0)
        kpair = jnp.concatenate([kb_ref[n_early], kb_ref[n_late]], axis=0)
        z = lax.dot_general(qb, kpair, _NT, preferred_element_type=F32) * scale
        log_beta, log_om = log_sigmoids(z)
        om_early, om_late = log_om[:, :blk], log_om[:, blk:]
        if late_is_own:
            om_late = jnp.where(strict, om_late, 0.0)
            sum_late = row_sum(om_late)
            a_late = jnp.where(strict, jnp.exp(log_beta[:, blk:] + sum_after(om_late)), 0.0)
            left = sum_late
        else:
            sum_late = row_sum(om_late)
            a_late = jnp.exp(log_beta[:, blk:] + sum_after(om_late) + carry)
            left = carry + sum_late
        a_early = jnp.exp(log_beta[:, :blk] + sum_after(om_early) + left)
        v_early = vb_ref[n_early]
        v_early = jnp.where(has_early, v_early, jnp.zeros_like(v_early))
        contrib = (jnp.dot(a_late.astype(BF16), vb_ref[n_late], preferred_element_type=F32)
                   + jnp.dot(a_early.astype(BF16), v_early, preferred_element_type=F32))
        acc = contrib if acc is None else acc + contrib
        return acc, left + row_sum(om_early)

    tiles = [q_tiles * step + k for k in range(q_tiles)]
    qbs = [q_ref[0, 0, k * blk:(k + 1) * blk, :].astype(BF16) for k in range(q_tiles)]
    firsts = [pair_step(qb, qi, True, None, None) for qb, qi in zip(qbs, tiles)]

    for k, (qb, qi, (acc0, carry0)) in enumerate(zip(qbs, tiles, firsts)):
        n_pairs = qi // 2

        def cond(state, n_pairs=n_pairs):
            pair, _, _, carry_max = state
            return jnp.logical_and(pair < n_pairs, carry_max > EXP_ZERO)

        def body(state, qb=qb, qi=qi):
            pair, acc, carry, _ = state
            acc, carry = pair_step(qb, qi - 2 - 2 * pair, False, acc, carry)
            return pair + 1, acc, carry, jnp.max(carry)

        _, acc, _, _ = lax.while_loop(cond, body, (jnp.int32(0), acc0, carry0, jnp.max(carry0)))
        o_ref[0, k * blk:(k + 1) * blk, :] = acc


def stick_breaking_attention(qkv, *, q_head0, k_head0, v_head0, n_heads, blk=256, q_tiles=4):
    b, _, t, dh = qkv.shape
    assert t % (blk * q_tiles) == 0
    nb = t // blk
    tq = blk * q_tiles
    kern = functools.partial(_sb_kernel, nb=nb, blk=blk, q_tiles=q_tiles, scale=dh ** -0.5)
    return pl.pallas_call(
        kern,
        out_shape=jax.ShapeDtypeStruct((b, t, n_heads * dh), F32),
        grid=(b, n_heads, t // tq),
        in_specs=[
            pl.BlockSpec((1, 1, tq, dh), lambda bi, h, qi: (bi, q_head0 + h, qi, 0)),
            pl.BlockSpec((1, 1, t, dh), lambda bi, h, qi: (bi, k_head0 + h, 0, 0)),
            pl.BlockSpec((1, 1, t, dh), lambda bi, h, qi: (bi, v_head0 + h, 0, 0)),
        ],
        out_specs=pl.BlockSpec((1, tq, dh), lambda bi, h, qi: (bi, qi, h)),
        scratch_shapes=[
            pltpu.VMEM((nb, blk, dh), BF16),
            pltpu.VMEM((nb, blk, dh), BF16),
        ],
        compiler_params=_params("parallel", "parallel", "arbitrary"),
    )(qkv, qkv, qkv)


def _xattn_kernel(q_ref, kv_ref, o_ref, *, n_heads, dh, scale):
    width = n_heads * dh
    for hh in range(n_heads):
        qh = q_ref[0, :, hh * dh:(hh + 1) * dh].astype(BF16)
        kh = kv_ref[0, :, hh * dh:(hh + 1) * dh].astype(BF16)
        vh = kv_ref[0, :, width + hh * dh:width + (hh + 1) * dh].astype(BF16)
        s = lax.dot_general(qh, kh, _NT, preferred_element_type=F32) * scale
        e = jnp.exp(s - jnp.max(s, axis=-1, keepdims=True))
        p = e / jnp.sum(e, axis=-1, keepdims=True)
        o_ref[0, :, hh * dh:(hh + 1) * dh] = jnp.dot(p.astype(BF16), vh, preferred_element_type=F32)


def cross_attention_core(q, kv, *, n_heads, tq=512):
    b, t, width = q.shape
    mlen = kv.shape[1]
    dh = width // n_heads
    tq = min(tq, t)
    kern = functools.partial(_xattn_kernel, n_heads=n_heads, dh=dh, scale=dh ** -0.5)
    return pl.pallas_call(
        kern,
        out_shape=jax.ShapeDtypeStruct((b, t, width), F32),
        grid=(b, t // tq),
        in_specs=[
            pl.BlockSpec((1, tq, width), lambda bi, i: (bi, i, 0)),
            pl.BlockSpec((1, mlen, 2 * width), lambda bi, i: (bi, 0, 0)),
        ],
        out_specs=pl.BlockSpec((1, tq, width), lambda bi, i: (bi, i, 0)),
        compiler_params=_params("parallel", "arbitrary"),
    )(q, kv)


def _extract_topk(work, n_rounds, *, exact, want_rank=False, n_masked=0):
    rows = lax.broadcasted_iota(jnp.int32, work.shape, 0).astype(F32)
    n_rows = float(work.shape[0])
    rank = jnp.full(work.shape, float(n_rounds), F32) if want_rank else None
    vals, firsts = [], []
    for r in range(n_rounds):
        mx = jnp.max(work, axis=0, keepdims=True)
        if exact:
            first = jnp.min(jnp.where(work == mx, rows, n_rows), axis=0, keepdims=True)
            hit = rows == first
            firsts.append(first)
        else:
            hit = work == mx
        work = jnp.where(hit, -jnp.inf, work)
        if want_rank:
            rank = jnp.where(hit, float(r), rank)
        vals.append(mx)
    if exact:
        tie = jnp.zeros_like(vals[0])
    else:
        removed = jnp.sum(jnp.where(work == -jnp.inf, 1.0, 0.0), axis=0, keepdims=True)
        tie = jnp.where(removed == float(n_rounds + n_masked), 0.0, 1.0)
    return vals, firsts, work, rank, tie


def _staircase_candidates(vals0, vals1, topk):
    sub = 8
    val1_all = jnp.concatenate(vals1, axis=0)
    val1_head = val1_all[:sub]
    row = lax.broadcasted_iota(jnp.int32, val1_head.shape, 0)
    slabs = [val1_all + vals0[0]]
    n_masked = 0
    for a in range(1, sub):
        b_max = topk // (a + 1) - 1
        slab = val1_head + vals0[a]
        if b_max < sub - 1:
            slab = jnp.where(row <= b_max, slab, -jnp.inf)
            n_masked += sub - 1 - b_max
        slabs.append(slab)
    slabs.append(jnp.concatenate(vals0[sub:], axis=0) + vals1[0])
    return jnp.concatenate(slabs, axis=0), n_masked


def _route_tables(scores0, scores1, topk, exact):
    vals0, firsts0, left0, _, tie0 = _extract_topk(scores0, topk, exact=exact)
    vals1, _, left1, rank1, tie1 = _extract_topk(scores1, topk, exact=exact, want_rank=True)
    cand, n_masked = _staircase_candidates(vals0, vals1, topk)
    tops, firsts2, _, _, tie2 = _extract_topk(cand, topk, exact=exact, n_masked=n_masked)
    tau = tops[topk - 1]
    z = jnp.zeros_like(tau)
    for tv in tops:
        z = z + jnp.exp(tv - tops[0])
    val1_all = jnp.concatenate(vals1, axis=0)
    rows = lax.broadcasted_iota(jnp.int32, scores0.shape, 0).astype(F32)
    count = jnp.zeros(scores0.shape, F32)
    sub = 8
    for a in range(topk):
        if exact:
            lo = 0 if a == 0 else (topk + (a - 1) * sub if a < sub else topk + (sub - 1) * sub + a - sub)
            hi = lo + (topk if a == 0 else (sub if a < sub else 1))
            n_sel = jnp.zeros_like(tau)
            for f in firsts2:
                n_sel = n_sel + jnp.where(f >= float(lo), jnp.where(f < float(hi), 1.0, 0.0), 0.0)
            at_rank_a = rows == firsts0[a]
        else:
            n_sel = jnp.sum(jnp.where(val1_all + vals0[a] >= tau, 1.0, 0.0), axis=0, keepdims=True)
            at_rank_a = scores0 == vals0[a]
        count = jnp.where(at_rank_a, n_sel, count)
    e0 = jnp.where(left0 == -jnp.inf, jnp.exp(scores0 - vals0[0]), 0.0) / z
    e1 = jnp.where(left1 == -jnp.inf, jnp.exp(scores1 - vals1[0]), 0.0)
    tie = jnp.maximum(jnp.maximum(tie0, tie1), tie2)
    return count, e0, rank1.astype(BF16), e1.astype(BF16), tie


def _peer_select_kernel(q_ref, keys_ref, cnt_ref, e0_ref, rank_ref, e1_ref, *, topk, heads):
    def head_scores(hh):
        out = []
        for p in range(2):
            c0 = (2 * hh + p) * LANES
            qs = q_ref[:, c0:c0 + LANES].astype(BF16)
            out.append(lax.dot_general(keys_ref[2 * hh + p], qs, _NT,
                                       preferred_element_type=F32))
        return out

    def write_tables(hh, exact):
        s0, s1 = head_scores(hh)
        cnt_ref[hh], e0_ref[hh], rank_ref[hh], e1_ref[hh], tie = _route_tables(s0, s1, topk, exact)
        return tie

    ties = [write_tables(hh, exact=False) for hh in range(heads)]
    for hh in range(heads):
        @pl.when(jnp.max(ties[hh]) > 0.0)
        def _redo_exact(hh=hh):
            write_tables(hh, exact=True)


def peer_select(q, keys_bf16, *, tm=256, heads_per_step=4):
    n = q.shape[0]
    n_heads = keys_bf16.shape[0] // 2
    n_keys = keys_bf16.shape[1]
    tm = min(tm, n)
    hs = heads_per_step
    assert n_heads % hs == 0
    big_f32 = jax.ShapeDtypeStruct((n_heads, n_keys, n), F32)
    big_bf16 = jax.ShapeDtypeStruct((n_heads, n_keys, n), BF16)
    big_spec = pl.BlockSpec((hs, n_keys, tm), lambda i, h: (h, 0, i))
    kern = functools.partial(_peer_select_kernel, topk=PEER_TOPK, heads=hs)
    return pl.pallas_call(
        kern,
        out_shape=(big_f32, big_f32, big_bf16, big_bf16),
        grid=(n // tm, n_heads // hs),
        in_specs=[
            pl.BlockSpec((tm, hs * 2 * LANES), lambda i, h: (i, h)),
            pl.BlockSpec((hs * 2, n_keys, keys_bf16.shape[2]), lambda i, h: (h, 0, 0)),
        ],
        out_specs=(big_spec, big_spec, big_spec, big_spec),
        compiler_params=_params("parallel", "arbitrary"),
    )(q, keys_bf16)


def _peer_main_kernel(x_ref, g_ref, u_ref, v_ref, cnt_ref, e0_ref, rank_ref, e1_ref, o_ref,
                      h_ref, gate_ref, *, n_heads, n_keys, row_chunk):
    e = pl.program_id(1)
    tm = h_ref.shape[0]
    te = u_ref.shape[0]
    groups = te // n_keys

    @pl.when(e == 0)
    def _prologue():
        for r0 in range(0, tm, row_chunk):
            x = x_ref[r0:r0 + row_chunk, :]
            ms = jnp.mean(x * x, axis=-1, keepdims=True)
            h_ref[r0:r0 + row_chunk, :] = ((x * lax.rsqrt(ms + RMS_EPS)) * g_ref[...]).astype(BF16)
            o_ref[r0:r0 + row_chunk, :] = x

    act = lax.dot_general(u_ref[...], h_ref[...], _NT, preferred_element_type=F32)
    for gi in range(groups):
        i = e * groups + gi
        route = jnp.zeros((n_keys, tm), BF16)
        for h in range(n_heads):
            cnt = cnt_ref[h, pl.ds(i, 1), :].astype(BF16)
            w0 = e0_ref[h, pl.ds(i, 1), :].astype(BF16)
            route = route + jnp.where(rank_ref[h] < cnt, e1_ref[h] * w0, jnp.zeros((), BF16))
        a = act[gi * n_keys:(gi + 1) * n_keys, :]
        gelu = 0.5 * a * (1.0 + lax.erf(a * np.float32(np.sqrt(0.5))))
        gate_ref[gi * n_keys:(gi + 1) * n_keys, :] = gelu.astype(BF16) * route
    o_ref[...] += lax.dot_general(gate_ref[...], v_ref[...], _TN, preferred_element_type=F32)


def peer_main(x, g, u_bf16, v_bf16, count, e0, rank, e1, *, tm=512, te=512):
    n, d = x.shape
    n_exp = u_bf16.shape[0]
    n_heads, n_keys, _ = count.shape
    tm = min(tm, n)
    assert n % tm == 0 and n_exp % te == 0 and te % n_keys == 0 and tm % LANES == 0
    once = pl.Buffered(1)
    big_spec = pl.BlockSpec((n_heads, n_keys, tm), lambda i, e: (0, 0, i))
    kern = functools.partial(_peer_main_kernel, n_heads=n_heads, n_keys=n_keys,
                             row_chunk=min(128, tm))
    return pl.pallas_call(
        kern,
        out_shape=jax.ShapeDtypeStruct((n, d), F32),
        grid=(n // tm, n_exp // te),
        in_specs=[
            pl.BlockSpec((tm, d), lambda i, e: (i, 0), pipeline_mode=once),
            pl.BlockSpec((1, d), lambda i, e: (0, 0)),
            pl.BlockSpec((te, d), lambda i, e: (e, 0)),
            pl.BlockSpec((te, d), lambda i, e: (e, 0)),
            big_spec, big_spec, big_spec, big_spec,
        ],
        out_specs=pl.BlockSpec((tm, d), lambda i, e: (i, 0), pipeline_mode=once),
        scratch_shapes=[
            pltpu.VMEM((tm, d), BF16),
            pltpu.VMEM((te, tm), BF16),
        ],
        compiler_params=_params("parallel", "arbitrary"),
    )(x, g.reshape(1, d).astype(F32), u_bf16, v_bf16, count, e0, rank, e1)


def _layer(x, mem, norm_mix_g, w_in, moba_q_norm_g, moba_k_norm_g, moba_out_norm_g,
           sb_out_norm_g, w_out, norm_xattn_g, norm_mem_g, w_xq, w_xkv, xattn_q_norm_g,
           xattn_k_norm_g, w_xo, norm_ffn_g, w_peer_q, peer_sub_keys, peer_u, peer_v):
    b, t, d = x.shape
    n = b * t
    xf = x.reshape(n, d)
    mix_width = w_in.shape[1] // 3
    n_heads = mix_width // (2 * HEAD_DIM)
    grp = n_heads * HEAD_DIM
    slopes = jnp.asarray(2.0 ** (-8.0 * np.arange(1, n_heads + 1) / n_heads), dtype=F32)

    col_gain = jnp.concatenate([jnp.tile(moba_q_norm_g, n_heads), jnp.tile(moba_k_norm_g, n_heads),
                                jnp.ones((w_in.shape[1] - 2 * grp,), F32)])
    qkv = norm_matmul([xf], [norm_mix_g], w_in.astype(BF16), col_gain=col_gain,
                      n_norm_cols=2 * grp, head_major_bt=(b, t), tn=1024)
    qk_norm_bound = HEAD_DIM * jnp.max(jnp.abs(moba_q_norm_g)) * jnp.max(jnp.abs(moba_k_norm_g))
    o_moba = moba_attention(qkv, slopes, qk_norm_bound, q_head0=0, k_head0=n_heads,
                            v_head0=2 * n_heads, n_heads=n_heads)
    o_sb = stick_breaking_attention(qkv, q_head0=3 * n_heads, k_head0=4 * n_heads,
                                    v_head0=5 * n_heads, n_heads=n_heads)
    x1 = norm_matmul([o_moba.reshape(n, grp), o_sb.reshape(n, grp)],
                     [moba_out_norm_g, sb_out_norm_g], w_out.astype(BF16), residual=xf, tn=1024)

    xw = w_xq.shape[1]
    n_mem = mem.shape[1]
    kv_gain = jnp.concatenate([jnp.tile(xattn_k_norm_g, N_XATTN_HEADS), jnp.ones((xw,), F32)])
    kv = norm_matmul([mem.reshape(b * n_mem, d)], [norm_mem_g], w_xkv.astype(BF16),
                     col_gain=kv_gain, n_norm_cols=xw)
    xq = norm_matmul([x1], [norm_xattn_g], w_xq.astype(BF16),
                     col_gain=jnp.tile(xattn_q_norm_g, N_XATTN_HEADS), n_norm_cols=xw)
    o_x = cross_attention_core(xq.reshape(b, t, xw), kv.reshape(b, n_mem, 2 * xw),
                               n_heads=N_XATTN_HEADS)
    x2 = norm_matmul([o_x.reshape(n, xw)], [None], w_xo.astype(BF16), residual=x1, tn=4096)

    pq = norm_matmul([x2], [norm_ffn_g], w_peer_q.astype(BF16), tn=1024)
    keys = peer_sub_keys.reshape(PEER_HEADS * 2, PEER_N_KEYS, -1).astype(BF16)
    count, e0, rank, e1 = peer_select(pq, keys)
    x3 = peer_main(x2, norm_ffn_g, peer_u.astype(BF16), peer_v.astype(BF16), count, e0, rank, e1)
    return x3.reshape(b, t, d)


def kernel(x, mem, norm_mix_g, w_in, moba_q_norm_g, moba_k_norm_g, moba_out_norm_g, sb_out_norm_g,
           w_out, norm_xattn_g, norm_mem_g, w_xq, w_xkv, xattn_q_norm_g, xattn_k_norm_g, w_xo,
           norm_ffn_g, w_peer_q, peer_sub_keys, peer_u, peer_v):
    depth = w_in.shape[0]
    for l in range(depth):
        x = _layer(x, mem, norm_mix_g[l], w_in[l], moba_q_norm_g[l], moba_k_norm_g[l],
                   moba_out_norm_g[l], sb_out_norm_g[l], w_out[l], norm_xattn_g[l], norm_mem_g[l],
                   w_xq[l], w_xkv[l], xattn_q_norm_g[l], xattn_k_norm_g[l], w_xo[l], norm_ffn_g[l],
                   w_peer_q[l], peer_sub_keys[l], peer_u[l], peer_v[l])
    return x
```
